```python
import math
import jax
import jax.numpy as jnp
from jax import lax
import numpy as np

D_MODEL = 1024
BATCH = 8
SEQ = 4096
DEPTH = 2
DEC_BATCH = 8
DEC_SEQ = 32
PAST_LEN = 2048

CHUNK = 64
Q_BLOCK = 128
H_A = 4
D_A = D_MODEL // 16
DV_A = 2 * D_A
ATT_WIDTH = H_A * DV_A
H_M = 4
D_M = D_MODEL // 16
ML_WIDTH = H_M * D_M
CONV_CH = D_MODEL // 4
CONV_K = 3
MIX_WIDTH = ATT_WIDTH + ML_WIDTH + CONV_CH
H_X = 4
D_X = D_MODEL // H_X
N_MEM = 256
D_FF = -(-8 * D_MODEL // (3 * 256)) * 256
NUM_BUCKETS = 32
MAX_DISTANCE = 128
EPS = 1e-6
IN_WIDTHS = (H_A * 2 * D_A, H_A * 2 * D_A, H_A * DV_A, ML_WIDTH, ML_WIDTH, ML_WIDTH, ML_WIDTH, H_M, H_M, CONV_CH, CONV_CH, CONV_CH)
IN_COLS = sum(IN_WIDTHS)

kernel_name = 'hybrid_streaming_encoder_step'


def rmsnorm(x, g):
    xf = x.astype(jnp.float32)
    y = xf * lax.rsqrt(jnp.mean(xf * xf, axis=-1, keepdims=True) + EPS)
    return (y * g.astype(jnp.float32)).astype(x.dtype)


def rel_bucket(rel):
    half = NUM_BUCKETS // 2
    max_exact = half // 2
    n = jnp.abs(rel)
    large = max_exact + (jnp.log(jnp.maximum(n, 1).astype(jnp.float32) / max_exact)
                         / math.log(MAX_DISTANCE / max_exact) * (half - max_exact)).astype(jnp.int32)
    large = jnp.minimum(large, half - 1)
    return jnp.where(rel > 0, half, 0) + jnp.where(n < max_exact, n, large)


def diff_attn_block(q, qpos, k, v, kpos, lam, rel_bias):
    s = jnp.einsum('bqhcd,bkhcd->bchqk', q, k).astype(jnp.float32) * (D_A ** -0.5)
    bias = jnp.transpose(rel_bias[rel_bucket(kpos[None, :] - qpos[:, None])], (2, 0, 1)).astype(jnp.float32)
    visible = (kpos[None, :] // CHUNK) <= (qpos[:, None] // CHUNK)
    s = jnp.where(visible, s + bias, -jnp.inf)
    p = jax.nn.softmax(s, axis=-1)
    a = p[:, 0] - lam * p[:, 1]
    return jnp.einsum('bhqk,bkhv->bqhv', a.astype(v.dtype), v)


def diff_attn_prompt(q, k, v, lam, rel_bias):
    B, S = q.shape[0], q.shape[1]
    nb = S // Q_BLOCK
    qb = jnp.swapaxes(q.reshape(B, nb, Q_BLOCK, H_A, 2, D_A), 0, 1)
    starts = jnp.arange(nb, dtype=jnp.int32) * Q_BLOCK
    kpos = jnp.arange(S, dtype=jnp.int32)

    def one(args):
        qblk, st = args
        return diff_attn_block(qblk, st + jnp.arange(Q_BLOCK, dtype=jnp.int32), k, v, kpos, lam, rel_bias)

    o = lax.map(one, (qb, starts))
    return jnp.swapaxes(o, 0, 1).reshape(B, S, H_A, DV_A)


def mlstm_chunk(carry, xs):
    C, n, m = carry
    q, k, v, ig, lf = xs
    L = q.shape[2]
    F = jnp.cumsum(lf, axis=-1)
    causal = jnp.tril(jnp.ones((L, L), dtype=bool))
    D = jnp.where(causal, F[..., :, None] - F[..., None, :] + ig[..., None, :], -jnp.inf)
    inter = F + m[..., None]
    mt = jnp.maximum(inter, jnp.max(D, axis=-1))
    w_intra = jnp.exp(D - mt[..., None])
    w_inter = jnp.exp(inter - mt)
    s = jnp.einsum('bhtd,bhjd->bhtj', q, k) * w_intra
    num = jnp.einsum('bhtj,bhjv->bhtv', s, v) + w_inter[..., None] * jnp.einsum('bhtd,bhdv->bhtv', q, C)
    den = jnp.sum(s, axis=-1) + w_inter * jnp.einsum('bhtd,bhd->bht', q, n)
    h = num / jnp.maximum(jnp.abs(den), jnp.exp(-mt))[..., None]
    m_new = mt[..., -1]
    w_state = jnp.exp(F[..., -1:] - F + ig - m_new[..., None])
    decay = jnp.exp(F[..., -1] + m - m_new)
    C_new = decay[..., None, None] * C + jnp.einsum('bhj,bhjd,bhjv->bhdv', w_state, k, v)
    n_new = decay[..., None] * n + jnp.einsum('bhj,bhjd->bhd', w_state, k)
    return (C_new, n_new, m_new), h


def mlstm_run(q, k, v, ig, lf, state, chunk_len):
    B, L, H, d = q.shape
    nc = L // chunk_len

    def to_chunks(a):
        a = jnp.swapaxes(a.astype(jnp.float32), 1, 2)
        a = a.reshape((B, H, nc, chunk_len) + a.shape[3:])
        return jnp.moveaxis(a, 2, 0)

    xs = (to_chunks(q), to_chunks(k), to_chunks(v), to_chunks(ig), to_chunks(lf))
    state, h = lax.scan(mlstm_chunk, state, xs)
    h = jnp.moveaxis(h, 0, 2).reshape(B, H, L, d)
    return jnp.swapaxes(h, 1, 2), state


def short_conv(b, c, h, prev, w):
    L = b.shape[1]
    u = jnp.concatenate([prev.astype(b.dtype), c * h], axis=1)
    y = w[0] * u[:, 0:L]
    for j in range(1, CONV_K):
        y = y + w[j] * u[:, j:j + L]
    return b * y, u[:, L:]


def trunk_layer(x, layer, lp, rel_bias, mem_k, mem_v, past_k, past_v, ml_state, conv_prev):
    (g_mix, w_in, gate_b, conv_w, lam_p, g_att, g_ml, w_out,
     g_cross, w_cq, w_co, g_ffn, w_gate, w_up, w_down) = lp
    B, L, _ = x.shape
    prompt = past_k is None
    f32 = jnp.float32
    xn = rmsnorm(x, g_mix)
    offs = []
    acc = 0
    for wd in IN_WIDTHS[:-1]:
        acc += wd
        offs.append(acc)
    (q_a, k_a, v_a, q_m, k_m, v_m, o_m, i_m, f_m, b_c, c_c, h_c) = jnp.split(xn @ w_in, offs, axis=-1)

    q_a = q_a.reshape(B, L, H_A, 2, D_A)
    k_a = k_a.reshape(B, L, H_A, 2, D_A)
    v_a = v_a.reshape(B, L, H_A, DV_A)
    lam_init = 0.8 - 0.6 * math.exp(-0.3 * layer)
    lp32 = lam_p.astype(f32)
    lam = jnp.exp(jnp.sum(lp32[0] * lp32[1])) - jnp.exp(jnp.sum(lp32[2] * lp32[3])) + lam_init
    if prompt:
        o_a = diff_attn_prompt(q_a, k_a, v_a, lam, rel_bias)
    else:
        P = past_k.shape[1]
        k_all = jnp.concatenate([past_k.reshape(B, P, H_A, 2, D_A).astype(k_a.dtype), k_a], axis=1)
        v_all = jnp.concatenate([past_v.astype(v_a.dtype), v_a], axis=1)
        o_a = diff_attn_block(q_a, P + jnp.arange(L, dtype=jnp.int32), k_all, v_all,
                              jnp.arange(P + L, dtype=jnp.int32), lam, rel_bias)
    o_a = rmsnorm(o_a, g_att.reshape(H_A, DV_A)) * (1.0 - lam_init)

    q_m = q_m.reshape(B, L, H_M, D_M)
    k_m = k_m.reshape(B, L, H_M, D_M) * (D_M ** -0.5)
    v_m = v_m.reshape(B, L, H_M, D_M)
    gb = gate_b.astype(f32)
    ig = i_m.astype(f32) + gb[0]
    lf = jax.nn.log_sigmoid(f_m.astype(f32) + gb[1])
    state0 = (ml_state[0].astype(f32), ml_state[1].astype(f32), ml_state[2].astype(f32))
    h_m, (C1, n1, m1) = mlstm_run(q_m, k_m, v_m, ig, lf, state0, CHUNK if prompt else L)
    h_m = h_m * jax.nn.sigmoid(o_m.reshape(B, L, H_M, D_M).astype(f32))
    h_m = rmsnorm(h_m, g_ml.reshape(H_M, D_M))

    o_c, conv_new = short_conv(b_c, c_c, h_c, conv_prev, conv_w)

    mixed = jnp.concatenate([o_a.reshape(B, L, ATT_WIDTH).astype(x.dtype),
                             h_m.reshape(B, L, ML_WIDTH).astype(x.dtype),
                             o_c.astype(x.dtype)], axis=-1)
    x = x + mixed @ w_out

    xn = rmsnorm(x, g_cross)
    qc = (xn @ w_cq).reshape(B, L, H_X, D_X)
    s = jnp.einsum('bqhd,bkhd->bhqk', qc, mem_k.astype(qc.dtype)).astype(f32) * (D_X ** -0.5)
    p = jax.nn.softmax(s, axis=-1)
    oc = jnp.einsum('bhqk,bkhd->bqhd', p.astype(x.dtype), mem_v.astype(x.dtype)).reshape(B, L, H_X * D_X)
    x = x + oc @ w_co

    xn = rmsnorm(x, g_ffn)
    x = x + (jax.nn.silu(xn @ w_gate) * (xn @ w_up)) @ w_down
    return x, (k_a.reshape(B, L, H_A, 2 * D_A), v_a, C1, n1, m1, conv_new)


def setup_inputs(seed: int = 0) -> dict:
    key = jax.random.key(seed)
    ks = jax.random.split(key, 32)
    f32 = jnp.float32

    def nrm(k, shape, scale):
        return jax.random.normal(k, shape, f32) * scale

    return {
        'x_prompt': nrm(ks[0], (BATCH, SEQ, D_MODEL), 1.0),
        'x_sample': nrm(ks[1], (DEC_BATCH, DEC_SEQ, D_MODEL), 1.0),
        'mem_prompt': nrm(ks[2], (BATCH, N_MEM, D_MODEL), 1.0),
        'cache_att_k': nrm(ks[3], (DEPTH, DEC_BATCH, PAST_LEN, H_A, 2 * D_A), 1.0),
        'cache_att_v': nrm(ks[4], (DEPTH, DEC_BATCH, PAST_LEN, H_A, DV_A), 1.0),
        'cache_mem_k': nrm(ks[5], (DEPTH, DEC_BATCH, N_MEM, H_X, D_X), 1.0),
        'cache_mem_v': nrm(ks[6], (DEPTH, DEC_BATCH, N_MEM, H_X, D_X), 1.0),
        'state_mlstm_C': nrm(ks[7], (DEPTH, DEC_BATCH, H_M, D_M, D_M), 0.3),
        'state_mlstm_n': nrm(ks[8], (DEPTH, DEC_BATCH, H_M, D_M), 0.3),
        'state_mlstm_m': nrm(ks[9], (DEPTH, DEC_BATCH, H_M), 0.5),
        'state_conv': nrm(ks[10], (DEPTH, DEC_BATCH, CONV_K - 1, CONV_CH), 1.0),
        'norm_mix': 1.0 + nrm(ks[11], (DEPTH, D_MODEL), 0.1),
        'w_in': nrm(ks[12], (DEPTH, D_MODEL, IN_COLS), D_MODEL ** -0.5),
        'mlstm_gate_bias': jnp.stack([nrm(ks[13], (DEPTH, H_M), 0.1),
                                      3.0 + nrm(ks[14], (DEPTH, H_M), 0.1)], axis=1),
        'conv_w': nrm(ks[15], (DEPTH, CONV_K, CONV_CH), CONV_K ** -0.5),
        'lambda_params': nrm(ks[16], (DEPTH, 4, D_A), 0.1),
        'norm_att_heads': 1.0 + nrm(ks[17], (DEPTH, ATT_WIDTH), 0.1),
        'norm_mlstm_heads': 1.0 + nrm(ks[18], (DEPTH, ML_WIDTH), 0.1),
        'w_out': nrm(ks[19], (DEPTH, MIX_WIDTH, D_MODEL), MIX_WIDTH ** -0.5),
        'norm_cross': 1.0 + nrm(ks[20], (DEPTH, D_MODEL), 0.1),
        'w_cq': nrm(ks[21], (DEPTH, D_MODEL, H_X * D_X), D_MODEL ** -0.5),
        'w_ck': nrm(ks[22], (DEPTH, D_MODEL, H_X * D_X), D_MODEL ** -0.5),
        'w_cv': nrm(ks[23], (DEPTH, D_MODEL, H_X * D_X), D_MODEL ** -0.5),
        'w_co': nrm(ks[24], (DEPTH, H_X * D_X, D_MODEL), (H_X * D_X) ** -0.5),
        'norm_ffn': 1.0 + nrm(ks[25], (DEPTH, D_MODEL), 0.1),
        'w_gate': nrm(ks[26], (DEPTH, D_MODEL, D_FF), D_MODEL ** -0.5),
        'w_up': nrm(ks[27], (DEPTH, D_MODEL, D_FF), D_MODEL ** -0.5),
        'w_down': nrm(ks[28], (DEPTH, D_FF, D_MODEL), D_FF ** -0.5),
        'rel_bias': nrm(ks[29], (NUM_BUCKETS, H_A), 0.5),
        'norm_final': 1.0 + nrm(ks[30], (D_MODEL,), 0.1),
    }


def reference(x_prompt, x_sample, mem_prompt, cache_att_k, cache_att_v, cache_mem_k, cache_mem_v,
              state_mlstm_C, state_mlstm_n, state_mlstm_m, state_conv,
              norm_mix, w_in, mlstm_gate_bias, conv_w, lambda_params, norm_att_heads, norm_mlstm_heads,
              w_out, norm_cross, w_cq, w_ck, w_cv, w_co, norm_ffn, w_gate, w_up, w_down,
              rel_bias, norm_final):
    Bp = x_prompt.shape[0]
    xp = x_prompt
    xs = x_sample
    p_k, p_v, p_mk, p_mv, p_C, p_n, p_m, p_cv = [], [], [], [], [], [], [], []
    s_k, s_v, s_C, s_n, s_m, s_cv = [], [], [], [], [], []
    for l in range(DEPTH):
        lp = (norm_mix[l], w_in[l], mlstm_gate_bias[l], conv_w[l], lambda_params[l],
              norm_att_heads[l], norm_mlstm_heads[l], w_out[l], norm_cross[l], w_cq[l], w_co[l],
              norm_ffn[l], w_gate[l], w_up[l], w_down[l])
        mk = (mem_prompt @ w_ck[l]).reshape(Bp, N_MEM, H_X, D_X)
        mv = (mem_prompt @ w_cv[l]).reshape(Bp, N_MEM, H_X, D_X)
        zero_state = (jnp.zeros((Bp, H_M, D_M, D_M), jnp.float32),
                      jnp.zeros((Bp, H_M, D_M), jnp.float32),
                      jnp.zeros((Bp, H_M), jnp.float32))
        zero_conv = jnp.zeros((Bp, CONV_K - 1, CONV_CH), x_prompt.dtype)
        xp, (pk, pv, pC, pn, pm, pcv) = trunk_layer(xp, l, lp, rel_bias, mk, mv, None, None,
                                                    zero_state, zero_conv)
        p_k.append(pk)
        p_v.append(pv)
        p_mk.append(mk)
        p_mv.append(mv)
        p_C.append(pC)
        p_n.append(pn)
        p_m.append(pm)
        p_cv.append(pcv)
        xs, (sk, sv, sC, sn, sm, scv) = trunk_layer(
            xs, l, lp, rel_bias, cache_mem_k[l], cache_mem_v[l], cache_att_k[l], cache_att_v[l],
            (state_mlstm_C[l], state_mlstm_n[l], state_mlstm_m[l]), state_conv[l])
        s_k.append(sk)
        s_v.append(sv)
        s_C.append(sC)
        s_n.append(sn)
        s_m.append(sm)
        s_cv.append(scv)
    y_prompt = rmsnorm(xp, norm_final)
    y_sample = rmsnorm(xs, norm_final)
    return (y_prompt, y_sample,
            jnp.stack(p_k), jnp.stack(p_v), jnp.stack(p_mk), jnp.stack(p_mv),
            jnp.stack(p_C), jnp.stack(p_n), jnp.stack(p_m), jnp.stack(p_cv),
            jnp.stack(s_k), jnp.stack(s_v), jnp.stack(s_C), jnp.stack(s_n), jnp.stack(s_m), jnp.stack(s_cv))
```

```python
import functools
import math

import jax
import jax.numpy as jnp
from jax import lax
from jax.experimental import pallas as pl
from jax.experimental.pallas import tpu as pltpu

F32 = jnp.float32
BF16 = jnp.bfloat16

D_MODEL = 1024
CHUNK = 64
H_A = 4
D_A = 64
DV_A = 128
ATT_WIDTH = H_A * DV_A
H_M = 4
D_M = 64
ML_WIDTH = H_M * D_M
CONV_CH = 256
CONV_K = 3
H_X = 4
D_X = 256
D_FF = 2816
NUM_BUCKETS = 32
MAX_DISTANCE = 128
EPS = 1e-6

N_PROJ = 3 * ATT_WIDTH + 4 * ML_WIDTH + 3 * CONV_CH
ML_STATE_W = ML_WIDTH + 128

V7X_VMEM_LIMIT = 56 * 1024 * 1024

_NT = (((1,), (1,)), ((), ()))
_TN = (((0,), (0,)), ((), ()))


def _const_spec(shape):
    nd = len(shape)
    return pl.BlockSpec(shape, lambda *_: (0,) * nd, pipeline_mode=pl.Buffered(1))


def _rms(x, g):
    ms = jnp.mean(x * x, axis=-1, keepdims=True)
    return x * lax.rsqrt(ms + EPS) * g


def _in_proj_kernel(x_ref, g_ref, w_ref, wgt_ref,
                    qa_ref, kf_ref, kb_ref, vf_ref, vb_ref, qkvm_ref, om_ref, bch_ref, gt_ref):
    xn = _rms(x_ref[...], g_ref[...]).astype(BF16)

    def proj(lo, hi):
        return jnp.dot(xn, w_ref[:, lo:hi], preferred_element_type=F32)

    a = ATT_WIDTH
    qa_ref[...] = (proj(0, a) * (D_A ** -0.5)).astype(BF16)
    k = proj(a, 2 * a)
    kf_ref[...] = k
    kb_ref[...] = k.astype(BF16)
    v = proj(2 * a, 3 * a)
    vf_ref[...] = v
    vb_ref[...] = v.astype(BF16)
    o = 3 * a
    m = ML_WIDTH
    qkvm_ref[:, 0:m] = proj(o, o + m).astype(BF16)
    qkvm_ref[:, m:2 * m] = (proj(o + m, o + 2 * m) * (D_M ** -0.5)).astype(BF16)
    qkvm_ref[:, 2 * m:3 * m] = proj(o + 2 * m, o + 3 * m).astype(BF16)
    om_ref[...] = proj(o + 3 * m, o + 4 * m)
    bch_ref[...] = proj(o + 4 * m, N_PROJ)
    gt_ref[...] = lax.dot_general(wgt_ref[...], xn, _NT, preferred_element_type=F32)


def _in_proj(x, g, w, wgt, tm):
    t = x.shape[0]
    a, m = ATT_WIDTH, ML_WIDTH
    row = lambda n: pl.BlockSpec((tm, n), lambda i: (i, 0))
    outs = [
        (jax.ShapeDtypeStruct((t, a), BF16), row(a)),
        (jax.ShapeDtypeStruct((t, a), F32), row(a)),
        (jax.ShapeDtypeStruct((t, a), BF16), row(a)),
        (jax.ShapeDtypeStruct((t, a), F32), row(a)),
        (jax.ShapeDtypeStruct((t, a), BF16), row(a)),
        (jax.ShapeDtypeStruct((t, 3 * m), BF16), row(3 * m)),
        (jax.ShapeDtypeStruct((t, m), F32), row(m)),
        (jax.ShapeDtypeStruct((t, 3 * CONV_CH), F32), row(3 * CONV_CH)),
        (jax.ShapeDtypeStruct((8, t), F32), pl.BlockSpec((8, tm), lambda i: (0, i))),
    ]
    return pl.pallas_call(
        _in_proj_kernel,
        grid=(t // tm,),
        in_specs=[row(D_MODEL), _const_spec((1, D_MODEL)), _const_spec((D_MODEL, N_PROJ)),
                  _const_spec((8, D_MODEL))],
        out_specs=[s for _, s in outs],
        out_shape=[s for s, _ in outs],
        compiler_params=pltpu.CompilerParams(dimension_semantics=("parallel",),
                                             vmem_limit_bytes=V7X_VMEM_LIMIT),
        name="in_proj",
    )(x, g, w, wgt)


def _lambda_value(lp, lam_init):
    a = jnp.sum(lp[0:1] * lp[1:2], axis=1, keepdims=True)
    b = jnp.sum(lp[2:3] * lp[3:4], axis=1, keepdims=True)
    return jnp.exp(a) - jnp.exp(b) + lam_init


def _stack_maps(q):
    lane = lax.broadcasted_iota(jnp.int32, q.shape, 1)
    zero = jnp.zeros_like(q)
    return jnp.concatenate([jnp.where(lane < D_A, q, zero), jnp.where(lane >= D_A, q, zero)], axis=0)


def _attn_finish(acc, l, lam, g, lam_init, t):
    o = acc * (1.0 / l)
    d = o[:t] - lam * o[t:]
    return (_rms(d, g) * (1.0 - lam_init)).astype(BF16)


def _attn_prompt_kernel(lam_ref, g_ref, q_ref, k_ref, v_ref, bias_ref, o_ref,
                        m_scr, l_scr, acc_scr, *, t, lam_init):
    i = pl.program_id(2)
    qz = _stack_maps(q_ref[...])

    def block(kb):
        r = pl.ds(pl.multiple_of(kb * t, t), t)
        s = lax.dot_general(qz, k_ref[r, :], _NT, preferred_element_type=F32)
        return s, v_ref[r, :]

    def update(s, v):
        m_prev = m_scr[...]
        m_new = jnp.maximum(m_prev, jnp.max(s, axis=1, keepdims=True))
        alpha = jnp.exp(m_prev - m_new)
        p = jnp.exp(s - m_new)
        l_scr[...] = alpha * l_scr[...] + jnp.sum(p, axis=1, keepdims=True)
        acc_scr[...] = alpha * acc_scr[...] + jnp.dot(p.astype(BF16), v, preferred_element_type=F32)
        m_scr[...] = m_new

    s, v = block(i)
    s = s + bias_ref[0, 0]
    m0 = jnp.max(s, axis=1, keepdims=True)
    p = jnp.exp(s - m0)
    m_scr[...] = m0
    l_scr[...] = jnp.sum(p, axis=1, keepdims=True)
    acc_scr[...] = jnp.dot(p.astype(BF16), v, preferred_element_type=F32)

    @pl.when(i >= 1)
    def _():
        s1, v1 = block(i - 1)
        update(s1 + bias_ref[0, 1], v1)

    def far(kb, carry):
        update(*block(kb))
        return carry

    lax.fori_loop(0, i - 1, far, 0)

    lam = _lambda_value(lam_ref[...], lam_init)
    o_ref[...] = _attn_finish(acc_scr[...], l_scr[...], lam, g_ref[...], lam_init, t)


def _attn_prompt(q, k, v, bias, lam_p, g_att, batch, seq, t, lam_init):
    nq = seq // t
    kv_spec = pl.BlockSpec((seq, DV_A), lambda b, h, i: (b, h))
    return pl.pallas_call(
        functools.partial(_attn_prompt_kernel, t=t, lam_init=lam_init),
        grid=(batch, H_A, nq),
        in_specs=[_const_spec((4, D_A)),
                  pl.BlockSpec((1, DV_A), lambda b, h, i: (0, h)),
                  pl.BlockSpec((t, DV_A), lambda b, h, i: (b * nq + i, h)),
                  kv_spec, kv_spec,
                  pl.BlockSpec((1, 2, 2 * t, t), lambda b, h, i: (h, 0, 0, 0))],
        out_specs=pl.BlockSpec((t, DV_A), lambda b, h, i: (b * nq + i, h)),
        out_shape=jax.ShapeDtypeStruct((batch * seq, ATT_WIDTH), BF16),
        scratch_shapes=[pltpu.VMEM((2 * t, 1), F32), pltpu.VMEM((2 * t, 1), F32),
                        pltpu.VMEM((2 * t, DV_A), F32)],
        compiler_params=pltpu.CompilerParams(dimension_semantics=("parallel", "parallel", "arbitrary"),
                                             vmem_limit_bytes=V7X_VMEM_LIMIT),
        name="attn_prompt",
    )(lam_p, g_att, q, k, v, bias)


def _attn_sample_kernel(lam_ref, g_ref, q_ref, kp_ref, vp_ref, kn_ref, vn_ref, bp_ref, bn_ref, o_ref,
                        *, lq, lam_init):
    qz = _stack_maps(q_ref[...])
    bp = bp_ref[0]
    bn = bn_ref[0]
    sp = lax.dot_general(qz, kp_ref[0].astype(BF16), _NT, preferred_element_type=F32)
    sp = sp + jnp.concatenate([bp, bp], axis=0)
    sn = lax.dot_general(qz, kn_ref[...], _NT, preferred_element_type=F32)
    sn = sn + jnp.concatenate([bn, bn], axis=0)
    m = jnp.maximum(jnp.max(sp, axis=1, keepdims=True), jnp.max(sn, axis=1, keepdims=True))
    pp = jnp.exp(sp - m)
    pn = jnp.exp(sn - m)
    l = jnp.sum(pp, axis=1, keepdims=True) + jnp.sum(pn, axis=1, keepdims=True)
    acc = (jnp.dot(pp.astype(BF16), vp_ref[0].astype(BF16), preferred_element_type=F32)
           + jnp.dot(pn.astype(BF16), vn_ref[...], preferred_element_type=F32))
    lam = _lambda_value(lam_ref[...], lam_init)
    o_ref[...] = _attn_finish(acc, l, lam, g_ref[...], lam_init, lq)


def _attn_sample(q, k_new, v_new, k_past, v_past, bias_past, bias_new, lam_p, g_att, batch, lq, lam_init):
    past = k_past.shape[1]
    new_spec = pl.BlockSpec((lq, DV_A), lambda b, h: (b, h))
    past_spec = pl.BlockSpec((1, past, DV_A), lambda b, h: (b, 0, h))
    return pl.pallas_call(
        functools.partial(_attn_sample_kernel, lq=lq, lam_init=lam_init),
        grid=(batch, H_A),
        in_specs=[_const_spec((4, D_A)),
                  pl.BlockSpec((1, DV_A), lambda b, h: (0, h)),
                  new_spec, past_spec, past_spec, new_spec, new_spec,
                  pl.BlockSpec((1, lq, past), lambda b, h: (h, 0, 0)),
                  pl.BlockSpec((1, lq, lq), lambda b, h: (h, 0, 0))],
        out_specs=new_spec,
        out_shape=jax.ShapeDtypeStruct((batch * lq, ATT_WIDTH), BF16),
        compiler_params=pltpu.CompilerParams(dimension_semantics=("parallel", "parallel"),
                                             vmem_limit_bytes=V7X_VMEM_LIMIT),
        name="attn_sample",
    )(lam_p, g_att, q, k_past, v_past, k_new, v_new, bias_past, bias_new)


def _recurrent_kernel(gb_ref, q_ref, k_ref, v_ref, og_ref, gt_ref, b_ref, c_ref, hc_ref, cw_ref, gml_ref,
                      s0_ref, m0_ref, cv0_ref,
                      hm_ref, oc_ref, s_ref, m_ref, cv_ref, *, L):
    @pl.when(pl.program_id(1) == 0)
    def _():
        s_ref[...] = s0_ref[...]
        m_ref[...] = m0_ref[...]
        cv_ref[...] = cv0_ref[...]

    q = q_ref[...]
    k = k_ref[...]
    v = v_ref[...]
    state = s_ref[0]
    q_state = jnp.dot(q, state.astype(BF16), preferred_element_type=F32)
    q_c = q_state[:, :ML_WIDTH]

    lane_head = lax.broadcasted_iota(jnp.int32, (L, ML_WIDTH), 1) // D_M
    row = lax.broadcasted_iota(jnp.int32, (L, L), 0)
    col = lax.broadcasted_iota(jnp.int32, (L, L), 1)
    causal = col <= row
    eye = col == row
    st_lane = lax.broadcasted_iota(jnp.int32, (1, ML_STATE_W), 1)
    st_lane_head = jnp.where(st_lane < ML_WIDTH, st_lane // D_M, st_lane - ML_WIDTH)
    m_lane = lax.broadcasted_iota(jnp.int32, (1, 128), 1)

    gt = gt_ref[0]
    m_all = m_ref[0]
    num = jnp.zeros((L, ML_WIDTH), F32)
    w_state = jnp.zeros((L, ML_WIDTH), F32)
    decay = jnp.zeros((1, ML_STATE_W), F32)
    m_next = jnp.zeros((1, 128), F32)
    for h in range(H_M):
        ig = gt[h:h + 1, :] + gb_ref[0, h]
        fz = gt[H_M + h:H_M + h + 1, :] + gb_ref[1, h]
        lf = jnp.minimum(fz, 0.0) - jnp.log1p(jnp.exp(-jnp.abs(fz)))
        f_col = jnp.sum(jnp.where(causal, lf, 0.0), axis=1, keepdims=True)
        f_row = jnp.sum(jnp.where(eye, f_col, 0.0), axis=0, keepdims=True)
        d = jnp.where(causal, f_col + (ig - f_row), -jnp.inf)
        m_prev = m_all[:, h:h + 1]
        inter = f_col + m_prev
        mt = jnp.maximum(inter, jnp.max(d, axis=1, keepdims=True))
        w_intra = jnp.exp(d - mt)
        w_inter = jnp.exp(inter - mt)
        head = lane_head == h
        qk = lax.dot_general(jnp.where(head, q, jnp.zeros_like(q)), k, _NT, preferred_element_type=F32)
        s = qk * w_intra
        den = jnp.sum(s, axis=1, keepdims=True) + w_inter * q_state[:, ML_WIDTH + h:ML_WIDTH + h + 1]
        inv = 1.0 / jnp.maximum(jnp.abs(den), jnp.exp(-mt))
        sv = jnp.dot(s.astype(BF16), v, preferred_element_type=F32)
        num = jnp.where(head, (sv + w_inter * q_c) * inv, num)
        m_new = mt[L - 1:L, :]
        w_col = jnp.sum(jnp.where(eye, w_intra[L - 1:L, :], 0.0), axis=1, keepdims=True)
        w_state = jnp.where(head, w_col, w_state)
        dec = jnp.exp(f_col[L - 1:L, :] + m_prev - m_new)
        decay = jnp.where(st_lane_head == h, dec, decay)
        m_next = jnp.where(m_lane == h, m_new, m_next)

    kw = (k.astype(F32) * w_state).astype(BF16)
    v_aug = jnp.concatenate([v, jnp.ones((L, ML_STATE_W - ML_WIDTH), BF16)], axis=1)
    upd = lax.dot_general(kw, v_aug, _TN, preferred_element_type=F32)
    s_row_head = lax.broadcasted_iota(jnp.int32, (ML_WIDTH, ML_STATE_W), 0) // D_M
    s_col = lax.broadcasted_iota(jnp.int32, (ML_WIDTH, ML_STATE_W), 1)
    s_col_head = jnp.where(s_col < ML_WIDTH, s_col // D_M, s_col - ML_WIDTH)
    s_ref[0] = decay * state + jnp.where(s_row_head == s_col_head, upd, 0.0)
    m_ref[0] = m_next

    hg = num * jax.nn.sigmoid(og_ref[...])
    scale = jnp.zeros((L, ML_WIDTH), F32)
    for h in range(H_M):
        head = lane_head == h
        ss = jnp.sum(jnp.where(head, hg * hg, 0.0), axis=1, keepdims=True)
        scale = jnp.where(head, lax.rsqrt(ss * (1.0 / D_M) + EPS), scale)
    hm_ref[...] = (hg * scale * gml_ref[...]).astype(BF16)

    u = c_ref[...] * hc_ref[...]
    prev = cv_ref[0]
    ri = lax.broadcasted_iota(jnp.int32, u.shape, 0)
    u1 = jnp.where(ri == 0, prev[1:2], pltpu.roll(u, 1, 0))
    u2 = jnp.where(ri == 0, prev[0:1], jnp.where(ri == 1, prev[1:2], pltpu.roll(u, 2, 0)))
    w = cw_ref[...]
    oc_ref[...] = (b_ref[...] * (w[0:1] * u2 + w[1:2] * u1 + w[2:3] * u)).astype(BF16)
    cv_ref[0, 0:1, :] = u[L - 2:L - 1]
    cv_ref[0, 1:2, :] = u[L - 1:L]


def _recurrent(gate_b, qkvm, og, gt, bch, conv_w, g_ml, s0, m0, cv0, batch, seq, L):
    nc = seq // L
    t = batch * seq
    tok = lambda j: pl.BlockSpec((L, ML_WIDTH), lambda b, c: (b * nc + c, j))
    per_b = lambda shape: pl.BlockSpec((1,) + shape, lambda b, c: (b, 0, 0))
    return pl.pallas_call(
        functools.partial(_recurrent_kernel, L=L),
        grid=(batch, nc),
        in_specs=[pl.BlockSpec(memory_space=pltpu.SMEM),
                  tok(0), tok(1), tok(2), tok(0),
                  pl.BlockSpec((1, 8, L), lambda b, c: (b * nc + c, 0, 0)),
                  tok(0), tok(1), tok(2),
                  _const_spec((CONV_K, CONV_CH)), _const_spec((1, ML_WIDTH)),
                  per_b((ML_WIDTH, ML_STATE_W)), per_b((1, 128)), per_b((CONV_K - 1, CONV_CH))],
        out_specs=[tok(0), tok(0),
                   per_b((ML_WIDTH, ML_STATE_W)), per_b((1, 128)), per_b((CONV_K - 1, CONV_CH))],
        out_shape=[jax.ShapeDtypeStruct((t, ML_WIDTH), BF16),
                   jax.ShapeDtypeStruct((t, CONV_CH), BF16),
                   jax.ShapeDtypeStruct((batch, ML_WIDTH, ML_STATE_W), F32),
                   jax.ShapeDtypeStruct((batch, 1, 128), F32),
                   jax.ShapeDtypeStruct((batch, CONV_K - 1, CONV_CH), F32)],
        compiler_params=pltpu.CompilerParams(dimension_semantics=("parallel", "arbitrary"),
                                             vmem_limit_bytes=V7X_VMEM_LIMIT),
        name="recurrent",
    )(gate_b, qkvm, qkvm, qkvm, og, gt, bch, bch, bch, conv_w, g_ml, s0, m0, cv0)


def _mem_kv_kernel(x_ref, w_ref, kf_ref, vf_ref, kb_ref, vb_ref):
    x = x_ref[...].astype(BF16)
    n = H_X * D_X
    k = jnp.dot(x, w_ref[:, :n], preferred_element_type=F32)
    v = jnp.dot(x, w_ref[:, n:], preferred_element_type=F32)
    kf_ref[...] = k
    vf_ref[...] = v
    kb_ref[...] = k.astype(BF16)
    vb_ref[...] = v.astype(BF16)


def _mem_kv(mem, w_kv, tm):
    t = mem.shape[0]
    n = H_X * D_X
    row = pl.BlockSpec((tm, n), lambda i: (i, 0))
    return pl.pallas_call(
        _mem_kv_kernel,
        grid=(t // tm,),
        in_specs=[pl.BlockSpec((tm, D_MODEL), lambda i: (i, 0)), _const_spec((D_MODEL, 2 * n))],
        out_specs=[row, row, row, row],
        out_shape=[jax.ShapeDtypeStruct((t, n), F32), jax.ShapeDtypeStruct((t, n), F32),
                   jax.ShapeDtypeStruct((t, n), BF16), jax.ShapeDtypeStruct((t, n), BF16)],
        compiler_params=pltpu.CompilerParams(dimension_semantics=("parallel",),
                                             vmem_limit_bytes=V7X_VMEM_LIMIT),
        name="mem_kv",
    )(mem, w_kv)


FF_CHUNK = 256


def _post_kernel(x_ref, oa_ref, hm_ref, oc_ref, mk_ref, mv_ref, wout_ref, gc_ref, wcq_ref, wco_ref,
                 gf_ref, wg_ref, wu_ref, wd_ref, gfin_ref, o_ref, *, final):
    a, m = ATT_WIDTH, ML_WIDTH
    x = x_ref[...]
    x = x + (jnp.dot(oa_ref[...], wout_ref[0:a, :], preferred_element_type=F32)
             + jnp.dot(hm_ref[...], wout_ref[a:a + m, :], preferred_element_type=F32)
             + jnp.dot(oc_ref[...], wout_ref[a + m:, :], preferred_element_type=F32))

    xn = _rms(x, gc_ref[...]).astype(BF16)
    qc = (jnp.dot(xn, wcq_ref[...], preferred_element_type=F32) * (D_X ** -0.5)).astype(BF16)
    cross = jnp.zeros_like(x)
    for h in range(H_X):
        sl = slice(h * D_X, (h + 1) * D_X)
        s = lax.dot_general(qc[:, sl], mk_ref[0, :, sl], _NT, preferred_element_type=F32)
        p = jnp.exp(s - jnp.max(s, axis=1, keepdims=True))
        p = p * (1.0 / jnp.sum(p, axis=1, keepdims=True))
        o = jnp.dot(p.astype(BF16), mv_ref[0, :, sl], preferred_element_type=F32)
        cross = cross + jnp.dot(o.astype(BF16), wco_ref[sl, :], preferred_element_type=F32)
    x = x + cross

    xn = _rms(x, gf_ref[...]).astype(BF16)
    ff = jnp.zeros_like(x)
    for j in range(D_FF // FF_CHUNK):
        sl = slice(j * FF_CHUNK, (j + 1) * FF_CHUNK)
        g = jnp.dot(xn, wg_ref[:, sl], preferred_element_type=F32)
        u = jnp.dot(xn, wu_ref[:, sl], preferred_element_type=F32)
        act = (g * jax.nn.sigmoid(g) * u).astype(BF16)
        ff = ff + jnp.dot(act, wd_ref[sl, :], preferred_element_type=F32)
    x = x + ff
    if final:
        x = _rms(x, gfin_ref[...])
    o_ref[...] = x


def _post(x, oa, hm, oc, mk, mv, w_out, g_cross, w_cq, w_co, g_ffn, w_gate, w_up, w_down, g_final,
          tm, tiles_per_stream, final):
    t = x.shape[0]
    row = lambda n: pl.BlockSpec((tm, n), lambda i: (i, 0))
    mem = pl.BlockSpec((1,) + mk.shape[1:], lambda i: (i // tiles_per_stream, 0, 0))
    vec = _const_spec((1, D_MODEL))
    return pl.pallas_call(
        functools.partial(_post_kernel, final=final),
        grid=(t // tm,),
        in_specs=[row(D_MODEL), row(ATT_WIDTH), row(ML_WIDTH), row(CONV_CH), mem, mem,
                  _const_spec(w_out.shape), vec, _const_spec(w_cq.shape), _const_spec(w_co.shape),
                  vec, _const_spec(w_gate.shape), _const_spec(w_up.shape), _const_spec(w_down.shape), vec],
        out_specs=row(D_MODEL),
        out_shape=jax.ShapeDtypeStruct((t, D_MODEL), F32),
        compiler_params=pltpu.CompilerParams(dimension_semantics=("parallel",),
                                             vmem_limit_bytes=V7X_VMEM_LIMIT),
        name="post",
    )(x, oa, hm, oc, mk, mv, w_out, g_cross, w_cq, w_co, g_ffn, w_gate, w_up, w_down, g_final)


def _rel_bucket(rel):
    half = NUM_BUCKETS // 2
    max_exact = half // 2
    n = jnp.abs(rel)
    large = max_exact + (jnp.log(jnp.maximum(n, 1).astype(F32) / max_exact)
                         / math.log(MAX_DISTANCE / max_exact) * (half - max_exact)).astype(jnp.int32)
    large = jnp.minimum(large, half - 1)
    return jnp.where(rel > 0, half, 0) + jnp.where(n < max_exact, n, large)


def _bias_table(rel_bias, rel):
    return jnp.transpose(rel_bias[_rel_bucket(rel)], (2, 0, 1)).astype(F32)


def _prompt_bias_tiles(rel_bias, t):
    i = jnp.arange(t, dtype=jnp.int32)[:, None]
    j = jnp.arange(t, dtype=jnp.int32)[None, :]
    far = rel_bias[_rel_bucket(jnp.int32(-MAX_DISTANCE))].astype(F32)[:, None, None]
    diag = jnp.where((j // CHUNK) <= (i // CHUNK), _bias_table(rel_bias, j - i) - far, -jnp.inf)
    before = _bias_table(rel_bias, j - i - t) - far
    tiles = jnp.stack([diag, before], axis=1)
    return jnp.concatenate([tiles, tiles], axis=2)


def _block_diag_state(c, n):
    b = c.shape[0]
    eye = jnp.eye(H_M, dtype=F32)
    cbd = jnp.einsum('bhdv,hg->bhdgv', c.astype(F32), eye).reshape(b, ML_WIDTH, ML_WIDTH)
    ncol = jnp.einsum('bhd,hg->bhdg', n.astype(F32), eye).reshape(b, ML_WIDTH, H_M)
    pad = jnp.zeros((b, ML_WIDTH, ML_STATE_W - ML_WIDTH - H_M), F32)
    return jnp.concatenate([cbd, ncol, pad], axis=2)


def _unpack_state(s, m):
    b = s.shape[0]
    blocks = s[:, :, :ML_WIDTH].reshape(b, H_M, D_M, H_M, D_M)
    c = jnp.stack([blocks[:, h, :, h, :] for h in range(H_M)], axis=1)
    ncols = s[:, :, ML_WIDTH:ML_WIDTH + H_M].reshape(b, H_M, D_M, H_M)
    n = jnp.stack([ncols[:, h, :, h] for h in range(H_M)], axis=1)
    return c, n, m[:, 0, :H_M]


def _layer(x, layer, wts, attn_fn, mem_k, mem_v, ml_state, conv_prev, g_final, final,
           batch, seq, tm_proj, L, tm_post):
    (g_mix, w_proj, w_gt, gate_b, conv_w, lam_p, g_att, g_ml, w_out,
     g_cross, w_cq, w_co, g_ffn, w_gate, w_up, w_down) = wts
    t = batch * seq
    lam_init = 0.8 - 0.6 * math.exp(-0.3 * layer)
    qa, kf, kb, vf, vb, qkvm, og, bch, gt = _in_proj(x, g_mix, w_proj, w_gt, tm_proj)
    oa = attn_fn(qa, kb, vb, lam_p, g_att, lam_init)
    gt = jnp.transpose(gt.reshape(8, t // L, L), (1, 0, 2))
    hm, oc, s1, m1, cv1 = _recurrent(gate_b, qkvm, og, gt, bch, conv_w, g_ml,
                                     ml_state[0], ml_state[1], conv_prev, batch, seq, L)
    x = _post(x, oa, hm, oc, mem_k, mem_v, w_out, g_cross, w_cq, w_co, g_ffn, w_gate, w_up, w_down,
              g_final, tm_post, seq // tm_post, final)
    c1, n1, mm1 = _unpack_state(s1, m1)
    return x, kf, vf, c1, n1, mm1, cv1


def kernel(x_prompt, x_sample, mem_prompt, cache_att_k, cache_att_v, cache_mem_k, cache_mem_v,
           state_mlstm_C, state_mlstm_n, state_mlstm_m, state_conv,
           norm_mix, w_in, mlstm_gate_bias, conv_w, lambda_params, norm_att_heads, norm_mlstm_heads,
           w_out, norm_cross, w_cq, w_ck, w_cv, w_co, norm_ffn, w_gate, w_up, w_down,
           rel_bias, norm_final):
    bp, sp, _ = x_prompt.shape
    bs, ss, _ = x_sample.shape
    depth = w_in.shape[0]
    past = cache_att_k.shape[2]
    n_mem = mem_prompt.shape[1]
    t_attn = 128
    l_prompt = 128

    xp = x_prompt.reshape(bp * sp, D_MODEL)
    xs = x_sample.reshape(bs * ss, D_MODEL)
    mem = mem_prompt.reshape(bp * n_mem, D_MODEL)
    g_final = norm_final.reshape(1, D_MODEL)

    bias_tiles = _prompt_bias_tiles(rel_bias, t_attn)
    rel_s = jnp.arange(past + ss, dtype=jnp.int32)[None, :] - (past + jnp.arange(ss, dtype=jnp.int32))[:, None]
    bias_s = _bias_table(rel_bias, rel_s)
    bias_s_past, bias_s_new = bias_s[:, :, :past], bias_s[:, :, past:]

    zero_state = (jnp.zeros((bp, ML_WIDTH, ML_STATE_W), F32), jnp.zeros((bp, 1, 128), F32))
    zero_conv = jnp.zeros((bp, CONV_K - 1, CONV_CH), F32)

    outs = {k: [] for k in ('pk', 'pv', 'pmk', 'pmv', 'pC', 'pn', 'pm', 'pcv', 'sk', 'sv', 'sC', 'sn', 'sm', 'scv')}
    gate_lo = 3 * ATT_WIDTH + 4 * ML_WIDTH
    for l in range(depth):
        w = w_in[l]
        wts = (norm_mix[l].reshape(1, D_MODEL),
               jnp.concatenate([w[:, :gate_lo], w[:, gate_lo + 2 * H_M:]], axis=1).astype(BF16),
               w[:, gate_lo:gate_lo + 2 * H_M].T.astype(BF16),
               mlstm_gate_bias[l].astype(F32), conv_w[l].astype(F32), lambda_params[l].astype(F32),
               norm_att_heads[l].reshape(1, ATT_WIDTH), norm_mlstm_heads[l].reshape(1, ML_WIDTH),
               w_out[l].astype(BF16), norm_cross[l].reshape(1, D_MODEL),
               w_cq[l].astype(BF16), w_co[l].astype(BF16), norm_ffn[l].reshape(1, D_MODEL),
               w_gate[l].astype(BF16), w_up[l].astype(BF16), w_down[l].astype(BF16))
        final = l == depth - 1

        w_kv = jnp.concatenate([w_ck[l], w_cv[l]], axis=1).astype(BF16)
        mkf, mvf, mkb, mvb = _mem_kv(mem, w_kv, 512)
        attn_p = lambda q, k, v, lam_p, g, li: _attn_prompt(q, k, v, bias_tiles, lam_p, g, bp, sp, t_attn, li)
        xp, kf, vf, c1, n1, m1, cv1 = _layer(
            xp, l, wts, attn_p, mkb.reshape(bp, n_mem, -1), mvb.reshape(bp, n_mem, -1),
            zero_state, zero_conv, g_final, final, bp, sp, 512, l_prompt, 512)
        outs['pk'].append(kf.reshape(bp, sp, H_A, 2 * D_A))
        outs['pv'].append(vf.reshape(bp, sp, H_A, DV_A))
        outs['pmk'].append(mkf.reshape(bp, n_mem, H_X, D_X))
        outs['pmv'].append(mvf.reshape(bp, n_mem, H_X, D_X))
        outs['pC'].append(c1); outs['pn'].append(n1); outs['pm'].append(m1); outs['pcv'].append(cv1)

        k_past = cache_att_k[l].reshape(bs, past, ATT_WIDTH)
        v_past = cache_att_v[l].reshape(bs, past, ATT_WIDTH)
        attn_s = lambda q, k, v, lam_p, g, li: _attn_sample(q, k, v, k_past, v_past, bias_s_past, bias_s_new,
                                                            lam_p, g, bs, ss, li)
        s0 = _block_diag_state(state_mlstm_C[l], state_mlstm_n[l])
        m0 = jnp.pad(state_mlstm_m[l].astype(F32), ((0, 0), (0, 128 - H_M))).reshape(bs, 1, 128)
        xs, kf, vf, c1, n1, m1, cv1 = _layer(
            xs, l, wts, attn_s,
            cache_mem_k[l].reshape(bs, n_mem, -1).astype(BF16), cache_mem_v[l].reshape(bs, n_mem, -1).astype(BF16),
            (s0, m0), state_conv[l].astype(F32), g_final, final, bs, ss, bs * ss, ss, ss)
        outs['sk'].append(kf.reshape(bs, ss, H_A, 2 * D_A))
        outs['sv'].append(vf.reshape(bs, ss, H_A, DV_A))
        outs['sC'].append(c1); outs['sn'].append(n1); outs['sm'].append(m1); outs['scv'].append(cv1)

    st = lambda k: jnp.stack(outs[k])
    return (xp.reshape(bp, sp, D_MODEL), xs.reshape(bs, ss, D_MODEL),
            st('pk'), st('pv'), st('pmk'), st('pmv'), st('pC'), st('pn'), st('pm'), st('pcv'),
            st('sk'), st('sv'), st('sC'), st('sn'), st('sm'), st('scv'))
```

```python
import functools
import math

import jax
import jax.numpy as jnp
from jax import lax
from jax.experimental import pallas as pl
from jax.experimental.pallas import tpu as pltpu

F32 = jnp.float32
BF16 = jnp.bfloat16

D_MODEL = 1024
CHUNK = 64
H_A = 4
D_A = 64
DV_A = 128
ATT_WIDTH = H_A * DV_A
H_M = 4
D_M = 64
ML_WIDTH = H_M * D_M
CONV_CH = 256
CONV_K = 3
H_X = 4
D_X = 256
D_FF = 2816
NUM_BUCKETS = 32
MAX_DISTANCE = 128
EPS = 1e-6

N_PROJ = 3 * ATT_WIDTH + 4 * ML_WIDTH + 3 * CONV_CH
ML_STATE_W = ML_WIDTH + 128

V7X_VMEM_LIMIT = 56 * 1024 * 1024

_NT = (((1,), (1,)), ((), ()))
_TN = (((0,), (0,)), ((), ()))


def _const_spec(shape):
    nd = len(shape)
    return pl.BlockSpec(shape, lambda *_: (0,) * nd, pipeline_mode=pl.Buffered(1))


def _rms(x, g):
    ms = jnp.mean(x * x, axis=-1, keepdims=True)
    return x * lax.rsqrt(ms + EPS) * g


def _in_proj_kernel(x_ref, g_ref, w_ref, wgt_ref,
                    qa_ref, kf_ref, kb_ref, vf_ref, vb_ref, qkvm_ref, om_ref, bch_ref, gt_ref):
    xn = _rms(x_ref[...], g_ref[...]).astype(BF16)

    def proj(lo, hi):
        return jnp.dot(xn, w_ref[:, lo:hi], preferred_element_type=F32)

    a = ATT_WIDTH
    qa_ref[...] = (proj(0, a) * (D_A ** -0.5)).astype(BF16)
    k = proj(a, 2 * a)
    kf_ref[...] = k
    kb_ref[...] = k.astype(BF16)
    v = proj(2 * a, 3 * a)
    vf_ref[...] = v
    vb_ref[...] = v.astype(BF16)
    o = 3 * a
    m = ML_WIDTH
    qkvm_ref[:, 0:m] = proj(o, o + m).astype(BF16)
    qkvm_ref[:, m:2 * m] = (proj(o + m, o + 2 * m) * (D_M ** -0.5)).astype(BF16)
    qkvm_ref[:, 2 * m:3 * m] = proj(o + 2 * m, o + 3 * m).astype(BF16)
    om_ref[...] = proj(o + 3 * m, o + 4 * m)
    bch_ref[...] = proj(o + 4 * m, N_PROJ)
    gt_ref[...] = lax.dot_general(wgt_ref[...], xn, _NT, preferred_element_type=F32)


def _in_proj(x, g, w, wgt, tm):
    t = x.shape[0]
    a, m = ATT_WIDTH, ML_WIDTH
    row = lambda n: pl.BlockSpec((tm, n), lambda i: (i, 0))
    outs = [
        (jax.ShapeDtypeStruct((t, a), BF16), row(a)),
        (jax.ShapeDtypeStruct((t, a), F32), row(a)),
        (jax.ShapeDtypeStruct((t, a), BF16), row(a)),
        (jax.ShapeDtypeStruct((t, a), F32), row(a)),
        (jax.ShapeDtypeStruct((t, a), BF16), row(a)),
        (jax.ShapeDtypeStruct((t, 3 * m), BF16), row(3 * m)),
        (jax.ShapeDtypeStruct((t, m), F32), row(m)),
        (jax.ShapeDtypeStruct((t, 3 * CONV_CH), F32), row(3 * CONV_CH)),
        (jax.ShapeDtypeStruct((8, t), F32), pl.BlockSpec((8, tm), lambda i: (0, i))),
    ]
    return pl.pallas_call(
        _in_proj_kernel,
        grid=(t // tm,),
        in_specs=[row(D_MODEL), _const_spec((1, D_MODEL)), _const_spec((D_MODEL, N_PROJ)),
                  _const_spec((8, D_MODEL))],
        out_specs=[s for _, s in outs],
        out_shape=[s for s, _ in outs],
        compiler_params=pltpu.CompilerParams(dimension_semantics=("parallel",),
                                             vmem_limit_bytes=V7X_VMEM_LIMIT),
        name="in_proj",
    )(x, g, w, wgt)


def _lambda_value(lp, lam_init):
    a = jnp.sum(lp[0:1] * lp[1:2], axis=1, keepdims=True)
    b = jnp.sum(lp[2:3] * lp[3:4], axis=1, keepdims=True)
    return jnp.exp(a) - jnp.exp(b) + lam_init


def _stack_maps(q):
    lane = lax.broadcasted_iota(jnp.int32, q.shape, 1)
    zero = jnp.zeros_like(q)
    return jnp.concatenate([jnp.where(lane < D_A, q, zero), jnp.where(lane >= D_A, q, zero)], axis=0)


def _attn_finish(acc, lam, g, lam_init, t):
    o = acc[:, :DV_A] / acc[:, DV_A:]
    d = o[:t] - lam * o[t:]
    return (_rms(d, g) * (1.0 - lam_init)).astype(BF16)


def _with_ones(v):
    return jnp.concatenate([v, jnp.ones_like(v)], axis=1)


def _attn_prompt_kernel(lam_ref, g_ref, q_ref, k_ref, v_ref, bias_ref, o_ref,
                        m_scr, l_scr, acc_scr, *, t, lam_init):
    i = pl.program_id(2)
    qz = _stack_maps(q_ref[...])

    def scores(start, width):
        r = pl.ds(pl.multiple_of(start, t), width)
        return lax.dot_general(k_ref[r, :], qz, _NT, preferred_element_type=F32), v_ref[r, :]

    def pv(p, v):
        return lax.dot_general(v, p.astype(BF16), _TN, preferred_element_type=F32)

    def update(s, v):
        m_prev = m_scr[...]
        m_new = jnp.maximum(m_prev, jnp.max(s, axis=0, keepdims=True))
        alpha = jnp.exp(m_prev - m_new)
        p = jnp.exp(s - m_new)
        l_scr[...] = alpha * l_scr[...] + jnp.sum(p, axis=0, keepdims=True)
        acc_scr[...] = alpha * acc_scr[...] + pv(p, v)
        m_scr[...] = m_new

    s, v = scores(jnp.maximum(i - 1, 0) * t, 2 * t)
    s = s + bias_ref[0, 0]
    m0 = jnp.max(s, axis=0, keepdims=True)
    p = jnp.exp(s - m0)
    m_scr[...] = m0
    l_scr[...] = jnp.sum(p, axis=0, keepdims=True)
    acc_scr[...] = pv(p, v)

    n_far = jnp.maximum(i - 1, 0)
    odd = n_far % 2

    @pl.when(odd == 1)
    def _():
        update(*scores(0, t))

    def far_pair(j, carry):
        update(*scores((odd + 2 * j) * t, 2 * t))
        return carry

    lax.fori_loop(0, n_far // 2, far_pair, 0)

    lam = _lambda_value(lam_ref[...], lam_init)
    o = acc_scr[...] * (1.0 / l_scr[...])
    d = o[:, :t] - lam * o[:, t:]
    ms = jnp.mean(d * d, axis=0, keepdims=True)
    y = jnp.transpose(d * lax.rsqrt(ms + EPS))
    o_ref[...] = (y * g_ref[...] * (1.0 - lam_init)).astype(BF16)


def _attn_prompt(q, k, v, bias, lam_p, g_att, batch, seq, t, lam_init):
    nq = seq // t
    kv_spec = pl.BlockSpec((seq, DV_A), lambda b, h, i: (b, h))
    return pl.pallas_call(
        functools.partial(_attn_prompt_kernel, t=t, lam_init=lam_init),
        grid=(batch, H_A, nq),
        in_specs=[_const_spec((4, D_A)),
                  pl.BlockSpec((1, DV_A), lambda b, h, i: (0, h)),
                  pl.BlockSpec((t, DV_A), lambda b, h, i: (b * nq + i, h)),
                  kv_spec, kv_spec,
                  pl.BlockSpec((1, 1, 2 * t, 2 * t), lambda b, h, i: (h, jnp.minimum(i, 1), 0, 0))],
        out_specs=pl.BlockSpec((t, DV_A), lambda b, h, i: (b * nq + i, h)),
        out_shape=jax.ShapeDtypeStruct((batch * seq, ATT_WIDTH), BF16),
        scratch_shapes=[pltpu.VMEM((1, 2 * t), F32), pltpu.VMEM((1, 2 * t), F32),
                        pltpu.VMEM((DV_A, 2 * t), F32)],
        compiler_params=pltpu.CompilerParams(dimension_semantics=("parallel", "parallel", "arbitrary"),
                                             vmem_limit_bytes=V7X_VMEM_LIMIT),
        name="attn_prompt",
    )(lam_p, g_att, q, k, v, bias)


def _attn_sample_kernel(lam_ref, g_ref, q_ref, kp_ref, vp_ref, kn_ref, vn_ref, bp_ref, bn_ref, o_ref,
                        *, lq, lam_init):
    qz = _stack_maps(q_ref[...])
    bp = bp_ref[0]
    bn = bn_ref[0]
    sp = lax.dot_general(qz, kp_ref[0].astype(BF16), _NT, preferred_element_type=F32)
    sp = sp + jnp.concatenate([bp, bp], axis=0)
    sn = lax.dot_general(qz, kn_ref[...], _NT, preferred_element_type=F32)
    sn = sn + jnp.concatenate([bn, bn], axis=0)
    m = jnp.maximum(jnp.max(sp, axis=1, keepdims=True), jnp.max(sn, axis=1, keepdims=True))
    pp = jnp.exp(sp - m)
    pn = jnp.exp(sn - m)
    acc = (jnp.dot(pp.astype(BF16), _with_ones(vp_ref[0].astype(BF16)), preferred_element_type=F32)
           + jnp.dot(pn.astype(BF16), _with_ones(vn_ref[...]), preferred_element_type=F32))
    lam = _lambda_value(lam_ref[...], lam_init)
    o_ref[...] = _attn_finish(acc, lam, g_ref[...], lam_init, lq)


def _attn_sample(q, k_new, v_new, k_past, v_past, bias_past, bias_new, lam_p, g_att, batch, lq, lam_init):
    past = k_past.shape[1]
    new_spec = pl.BlockSpec((lq, DV_A), lambda b, h: (b, h))
    past_spec = pl.BlockSpec((1, past, DV_A), lambda b, h: (b, 0, h))
    return pl.pallas_call(
        functools.partial(_attn_sample_kernel, lq=lq, lam_init=lam_init),
        grid=(batch, H_A),
        in_specs=[_const_spec((4, D_A)),
                  pl.BlockSpec((1, DV_A), lambda b, h: (0, h)),
                  new_spec, past_spec, past_spec, new_spec, new_spec,
                  pl.BlockSpec((1, lq, past), lambda b, h: (h, 0, 0)),
                  pl.BlockSpec((1, lq, lq), lambda b, h: (h, 0, 0))],
        out_specs=new_spec,
        out_shape=jax.ShapeDtypeStruct((batch * lq, ATT_WIDTH), BF16),
        compiler_params=pltpu.CompilerParams(dimension_semantics=("parallel", "parallel"),
                                             vmem_limit_bytes=V7X_VMEM_LIMIT),
        name="attn_sample",
    )(lam_p, g_att, q, k_past, v_past, k_new, v_new, bias_past, bias_new)


def _recurrent_kernel(gb_ref, q_ref, k_ref, v_ref, og_ref, gt_ref, b_ref, c_ref, hc_ref, cw_ref, gml_ref,
                      s0_ref, m0_ref, cv0_ref,
                      hm_ref, oc_ref, s_ref, m_ref, cv_ref, *, L):
    @pl.when(pl.program_id(1) == 0)
    def _():
        s_ref[...] = s0_ref[...]
        m_ref[...] = m0_ref[...]
        cv_ref[...] = cv0_ref[...]

    q = q_ref[...]
    k = k_ref[...]
    v = v_ref[...]
    state = s_ref[0]
    q_state = jnp.dot(q, state.astype(BF16), preferred_element_type=F32)
    q_c = q_state[:, :ML_WIDTH]

    lane_head = lax.broadcasted_iota(jnp.int32, (L, ML_WIDTH), 1) // D_M
    row = lax.broadcasted_iota(jnp.int32, (L, L), 0)
    col = lax.broadcasted_iota(jnp.int32, (L, L), 1)
    causal = col <= row
    eye = col == row
    st_lane = lax.broadcasted_iota(jnp.int32, (1, ML_STATE_W), 1)
    st_lane_head = jnp.where(st_lane < ML_WIDTH, st_lane // D_M, st_lane - ML_WIDTH)
    m_lane = lax.broadcasted_iota(jnp.int32, (1, 128), 1)

    gt = gt_ref[0]
    m_all = m_ref[0]
    num = jnp.zeros((L, ML_WIDTH), F32)
    w_state = jnp.zeros((L, ML_WIDTH), F32)
    decay = jnp.zeros((1, ML_STATE_W), F32)
    m_next = jnp.zeros((1, 128), F32)
    for h in range(H_M):
        ig = gt[h:h + 1, :] + gb_ref[0, h]
        fz = gt[H_M + h:H_M + h + 1, :] + gb_ref[1, h]
        lf = jnp.minimum(fz, 0.0) - jnp.log1p(jnp.exp(-jnp.abs(fz)))
        f_col = jnp.sum(jnp.where(causal, lf, 0.0), axis=1, keepdims=True)
        f_row = jnp.sum(jnp.where(eye, f_col, 0.0), axis=0, keepdims=True)
        d = jnp.where(causal, f_col + (ig - f_row), -jnp.inf)
        m_prev = m_all[:, h:h + 1]
        inter = f_col + m_prev
        mt = jnp.maximum(inter, jnp.max(d, axis=1, keepdims=True))
        w_intra = jnp.exp(d - mt)
        w_inter = jnp.exp(inter - mt)
        head = lane_head == h
        qk = lax.dot_general(jnp.where(head, q, jnp.zeros_like(q)), k, _NT, preferred_element_type=F32)
        s = qk * w_intra
        den = jnp.sum(s, axis=1, keepdims=True) + w_inter * q_state[:, ML_WIDTH + h:ML_WIDTH + h + 1]
        inv = 1.0 / jnp.maximum(jnp.abs(den), jnp.exp(-mt))
        sv = jnp.dot(s.astype(BF16), v, preferred_element_type=F32)
        num = jnp.where(head, (sv + w_inter * q_c) * inv, num)
        m_new = mt[L - 1:L, :]
        w_col = jnp.sum(jnp.where(eye, w_intra[L - 1:L, :], 0.0), axis=1, keepdims=True)
        w_state = jnp.where(head, w_col, w_state)
        dec = jnp.exp(f_col[L - 1:L, :] + m_prev - m_new)
        decay = jnp.where(st_lane_head == h, dec, decay)
        m_next = jnp.where(m_lane == h, m_new, m_next)

    kw = (k.astype(F32) * w_state).astype(BF16)
    v_aug = jnp.concatenate([v, jnp.ones((L, ML_STATE_W - ML_WIDTH), BF16)], axis=1)
    upd = lax.dot_general(kw, v_aug, _TN, preferred_element_type=F32)
    s_row_head = lax.broadcasted_iota(jnp.int32, (ML_WIDTH, ML_STATE_W), 0) // D_M
    s_col = lax.broadcasted_iota(jnp.int32, (ML_WIDTH, ML_STATE_W), 1)
    s_col_head = jnp.where(s_col < ML_WIDTH, s_col // D_M, s_col - ML_WIDTH)
    s_ref[0] = decay * state + jnp.where(s_row_head == s_col_head, upd, 0.0)
    m_ref[0] = m_next

    hg = num * jax.nn.sigmoid(og_ref[...])
    scale = jnp.zeros((L, ML_WIDTH), F32)
    for h in range(H_M):
        head = lane_head == h
        ss = jnp.sum(jnp.where(head, hg * hg, 0.0), axis=1, keepdims=True)
        scale = jnp.where(head, lax.rsqrt(ss * (1.0 / D_M) + EPS), scale)
    hm_ref[...] = (hg * scale * gml_ref[...]).astype(BF16)

    u = c_ref[...] * hc_ref[...]
    prev = cv_ref[0]
    ri = lax.broadcasted_iota(jnp.int32, u.shape, 0)
    u1 = jnp.where(ri == 0, prev[1:2], pltpu.roll(u, 1, 0))
    u2 = jnp.where(ri == 0, prev[0:1], jnp.where(ri == 1, prev[1:2], pltpu.roll(u, 2, 0)))
    w = cw_ref[...]
    oc_ref[...] = (b_ref[...] * (w[0:1] * u2 + w[1:2] * u1 + w[2:3] * u)).astype(BF16)
    cv_ref[0, 0:1, :] = u[L - 2:L - 1]
    cv_ref[0, 1:2, :] = u[L - 1:L]


def _recurrent(gate_b, qkvm, og, gt, bch, conv_w, g_ml, s0, m0, cv0, batch, seq, L):
    nc = seq // L
    t = batch * seq
    tok = lambda j: pl.BlockSpec((L, ML_WIDTH), lambda b, c: (b * nc + c, j))
    per_b = lambda shape: pl.BlockSpec((1,) + shape, lambda b, c: (b, 0, 0))
    return pl.pallas_call(
        functools.partial(_recurrent_kernel, L=L),
        grid=(batch, nc),
        in_specs=[pl.BlockSpec(memory_space=pltpu.SMEM),
                  tok(0), tok(1), tok(2), tok(0),
                  pl.BlockSpec((1, 8, L), lambda b, c: (b * nc + c, 0, 0)),
                  tok(0), tok(1), tok(2),
                  _const_spec((CONV_K, CONV_CH)), _const_spec((1, ML_WIDTH)),
                  per_b((ML_WIDTH, ML_STATE_W)), per_b((1, 128)), per_b((CONV_K - 1, CONV_CH))],
        out_specs=[tok(0), tok(0),
                   per_b((ML_WIDTH, ML_STATE_W)), per_b((1, 128)), per_b((CONV_K - 1, CONV_CH))],
        out_shape=[jax.ShapeDtypeStruct((t, ML_WIDTH), BF16),
                   jax.ShapeDtypeStruct((t, CONV_CH), BF16),
                   jax.ShapeDtypeStruct((batch, ML_WIDTH, ML_STATE_W), F32),
                   jax.ShapeDtypeStruct((batch, 1, 128), F32),
                   jax.ShapeDtypeStruct((batch, CONV_K - 1, CONV_CH), F32)],
        compiler_params=pltpu.CompilerParams(dimension_semantics=("parallel", "arbitrary"),
                                             vmem_limit_bytes=V7X_VMEM_LIMIT),
        name="recurrent",
    )(gate_b, qkvm, qkvm, qkvm, og, gt, bch, bch, bch, conv_w, g_ml, s0, m0, cv0)


def _mem_kv_kernel(x_ref, w_ref, kf_ref, vf_ref, kb_ref, vb_ref):
    x = x_ref[...].astype(BF16)
    n = H_X * D_X
    k = jnp.dot(x, w_ref[:, :n], preferred_element_type=F32)
    v = jnp.dot(x, w_ref[:, n:], preferred_element_type=F32)
    kf_ref[...] = k
    vf_ref[...] = v
    kb_ref[...] = k.astype(BF16)
    vb_ref[...] = v.astype(BF16)


def _mem_kv(mem, w_kv, tm):
    t = mem.shape[0]
    n = H_X * D_X
    row = pl.BlockSpec((tm, n), lambda i: (i, 0))
    return pl.pallas_call(
        _mem_kv_kernel,
        grid=(t // tm,),
        in_specs=[pl.BlockSpec((tm, D_MODEL), lambda i: (i, 0)), _const_spec((D_MODEL, 2 * n))],
        out_specs=[row, row, row, row],
        out_shape=[jax.ShapeDtypeStruct((t, n), F32), jax.ShapeDtypeStruct((t, n), F32),
                   jax.ShapeDtypeStruct((t, n), BF16), jax.ShapeDtypeStruct((t, n), BF16)],
        compiler_params=pltpu.CompilerParams(dimension_semantics=("parallel",),
                                             vmem_limit_bytes=V7X_VMEM_LIMIT),
        name="mem_kv",
    )(mem, w_kv)


FF_CHUNK = 256


def _post_kernel(x_ref, oa_ref, hm_ref, oc_ref, mk_ref, mv_ref, wout_ref, gc_ref, wcq_ref, wco_ref,
                 gf_ref, wg_ref, wu_ref, wd_ref, gfin_ref, o_ref, *, final):
    a, m = ATT_WIDTH, ML_WIDTH
    x = x_ref[...]
    x = x + (jnp.dot(oa_ref[...], wout_ref[0:a, :], preferred_element_type=F32)
             + jnp.dot(hm_ref[...], wout_ref[a:a + m, :], preferred_element_type=F32)
             + jnp.dot(oc_ref[...], wout_ref[a + m:, :], preferred_element_type=F32))

    xn = _rms(x, gc_ref[...]).astype(BF16)
    qc = (jnp.dot(xn, wcq_ref[...], preferred_element_type=F32) * (D_X ** -0.5)).astype(BF16)
    cross = jnp.zeros_like(x)
    for h in range(H_X):
        sl = slice(h * D_X, (h + 1) * D_X)
        s = lax.dot_general(qc[:, sl], mk_ref[0, :, sl], _NT, preferred_element_type=F32)
        p = jnp.exp(s - jnp.max(s, axis=1, keepdims=True))
        p = p * (1.0 / jnp.sum(p, axis=1, keepdims=True))
        o = jnp.dot(p.astype(BF16), mv_ref[0, :, sl], preferred_element_type=F32)
        cross = cross + jnp.dot(o.astype(BF16), wco_ref[sl, :], preferred_element_type=F32)
    x = x + cross

    xn = _rms(x, gf_ref[...]).astype(BF16)
    ff = jnp.zeros_like(x)
    for j in range(D_FF // FF_CHUNK):
        sl = slice(j * FF_CHUNK, (j + 1) * FF_CHUNK)
        g = jnp.dot(xn, wg_ref[:, sl], preferred_element_type=F32)
        u = jnp.dot(xn, wu_ref[:, sl], preferred_element_type=F32)
        act = (g * jax.nn.sigmoid(g) * u).astype(BF16)
        ff = ff + jnp.dot(act, wd_ref[sl, :], preferred_element_type=F32)
    x = x + ff
    if final:
        x = _rms(x, gfin_ref[...])
    o_ref[...] = x


def _post(x, oa, hm, oc, mk, mv, w_out, g_cross, w_cq, w_co, g_ffn, w_gate, w_up, w_down, g_final,
          tm, tiles_per_stream, final):
    t = x.shape[0]
    row = lambda n: pl.BlockSpec((tm, n), lambda i: (i, 0))
    mem = pl.BlockSpec((1,) + mk.shape[1:], lambda i: (i // tiles_per_stream, 0, 0))
    vec = _const_spec((1, D_MODEL))
    return pl.pallas_call(
        functools.partial(_post_kernel, final=final),
        grid=(t // tm,),
        in_specs=[row(D_MODEL), row(ATT_WIDTH), row(ML_WIDTH), row(CONV_CH), mem, mem,
                  _const_spec(w_out.shape), vec, _const_spec(w_cq.shape), _const_spec(w_co.shape),
                  vec, _const_spec(w_gate.shape), _const_spec(w_up.shape), _const_spec(w_down.shape), vec],
        out_specs=row(D_MODEL),
        out_shape=jax.ShapeDtypeStruct((t, D_MODEL), F32),
        compiler_params=pltpu.CompilerParams(dimension_semantics=("parallel",),
                                             vmem_limit_bytes=V7X_VMEM_LIMIT),
        name="post",
    )(x, oa, hm, oc, mk, mv, w_out, g_cross, w_cq, w_co, g_ffn, w_gate, w_up, w_down, g_final)


def _rel_bucket(rel):
    half = NUM_BUCKETS // 2
    max_exact = half // 2
    n = jnp.abs(rel)
    large = max_exact + (jnp.log(jnp.maximum(n, 1).astype(F32) / max_exact)
                         / math.log(MAX_DISTANCE / max_exact) * (half - max_exact)).astype(jnp.int32)
    large = jnp.minimum(large, half - 1)
    return jnp.where(rel > 0, half, 0) + jnp.where(n < max_exact, n, large)


def _bias_table(rel_bias, rel):
    return jnp.transpose(rel_bias[_rel_bucket(rel)], (2, 0, 1)).astype(F32)


def _prompt_bias_tiles(rel_bias, t):
    key = jnp.arange(2 * t, dtype=jnp.int32)[:, None]
    qry = (jnp.arange(2 * t, dtype=jnp.int32) % t)[None, :]
    far = rel_bias[_rel_bucket(jnp.int32(-MAX_DISTANCE))].astype(F32)[:, None, None]

    def tile(key_offset):
        rel = key + key_offset - qry
        visible = ((key + key_offset) // CHUNK) <= (qry // CHUNK)
        return jnp.where(visible, _bias_table(rel_bias, rel) - far, -jnp.inf)

    return jnp.stack([tile(0), tile(-t)], axis=1)


def _block_diag_state(c, n):
    b = c.shape[0]
    eye = jnp.eye(H_M, dtype=F32)
    cbd = jnp.einsum('bhdv,hg->bhdgv', c.astype(F32), eye).reshape(b, ML_WIDTH, ML_WIDTH)
    ncol = jnp.einsum('bhd,hg->bhdg', n.astype(F32), eye).reshape(b, ML_WIDTH, H_M)
    pad = jnp.zeros((b, ML_WIDTH, ML_STATE_W - ML_WIDTH - H_M), F32)
    return jnp.concatenate([cbd, ncol, pad], axis=2)


def _unpack_state(s, m):
    b = s.shape[0]
    blocks = s[:, :, :ML_WIDTH].reshape(b, H_M, D_M, H_M, D_M)
    c = jnp.stack([blocks[:, h, :, h, :] for h in range(H_M)], axis=1)
    ncols = s[:, :, ML_WIDTH:ML_WIDTH + H_M].reshape(b, H_M, D_M, H_M)
    n = jnp.stack([ncols[:, h, :, h] for h in range(H_M)], axis=1)
    return c, n, m[:, 0, :H_M]


def _layer(x, layer, wts, attn_fn, mem_k, mem_v, ml_state, conv_prev, g_final, final,
           batch, seq, tm_proj, L, tm_post):
    (g_mix, w_proj, w_gt, gate_b, conv_w, lam_p, g_att, g_ml, w_out,
     g_cross, w_cq, w_co, g_ffn, w_gate, w_up, w_down) = wts
    t = batch * seq
    lam_init = 0.8 - 0.6 * math.exp(-0.3 * layer)
    qa, kf, kb, vf, vb, qkvm, og, bch, gt = _in_proj(x, g_mix, w_proj, w_gt, tm_proj)
    oa = attn_fn(qa, kb, vb, lam_p, g_att, lam_init)
    gt = jnp.transpose(gt.reshape(8, t // L, L), (1, 0, 2))
    hm, oc, s1, m1, cv1 = _recurrent(gate_b, qkvm, og, gt, bch, conv_w, g_ml,
                                     ml_state[0], ml_state[1], conv_prev, batch, seq, L)
    x = _post(x, oa, hm, oc, mem_k, mem_v, w_out, g_cross, w_cq, w_co, g_ffn, w_gate, w_up, w_down,
              g_final, tm_post, seq // tm_post, final)
    c1, n1, mm1 = _unpack_state(s1, m1)
    return x, kf, vf, c1, n1, mm1, cv1


def kernel(x_prompt, x_sample, mem_prompt, cache_att_k, cache_att_v, cache_mem_k, cache_mem_v,
           state_mlstm_C, state_mlstm_n, state_mlstm_m, state_conv,
           norm_mix, w_in, mlstm_gate_bias, conv_w, lambda_params, norm_att_heads, norm_mlstm_heads,
           w_out, norm_cross, w_cq, w_ck, w_cv, w_co, norm_ffn, w_gate, w_up, w_down,
           rel_bias, norm_final):
    bp, sp, _ = x_prompt.shape
    bs, ss, _ = x_sample.shape
    depth = w_in.shape[0]
    past = cache_att_k.shape[2]
    n_mem = mem_prompt.shape[1]
    t_attn = 256
    l_prompt = 128

    xp = x_prompt.reshape(bp * sp, D_MODEL)
    xs = x_sample.reshape(bs * ss, D_MODEL)
    mem = mem_prompt.reshape(bp * n_mem, D_MODEL)
    g_final = norm_final.reshape(1, D_MODEL)

    bias_tiles = _prompt_bias_tiles(rel_bias, t_attn)
    rel_s = jnp.arange(past + ss, dtype=jnp.int32)[None, :] - (past + jnp.arange(ss, dtype=jnp.int32))[:, None]
    bias_s = _bias_table(rel_bias, rel_s)
    bias_s_past, bias_s_new = bias_s[:, :, :past], bias_s[:, :, past:]

    zero_state = (jnp.zeros((bp, ML_WIDTH, ML_STATE_W), F32), jnp.zeros((bp, 1, 128), F32))
    zero_conv = jnp.zeros((bp, CONV_K - 1, CONV_CH), F32)

    outs = {k: [] for k in ('pk', 'pv', 'pmk', 'pmv', 'pC', 'pn', 'pm', 'pcv', 'sk', 'sv', 'sC', 'sn', 'sm', 'scv')}
    gate_lo = 3 * ATT_WIDTH + 4 * ML_WIDTH
    for l in range(depth):
        w = w_in[l]
        wts = (norm_mix[l].reshape(1, D_MODEL),
               jnp.concatenate([w[:, :gate_lo], w[:, gate_lo + 2 * H_M:]], axis=1).astype(BF16),
               w[:, gate_lo:gate_lo + 2 * H_M].T.astype(BF16),
               mlstm_gate_bias[l].astype(F32), conv_w[l].astype(F32), lambda_params[l].astype(F32),
               norm_att_heads[l].reshape(1, ATT_WIDTH), norm_mlstm_heads[l].reshape(1, ML_WIDTH),
               w_out[l].astype(BF16), norm_cross[l].reshape(1, D_MODEL),
               w_cq[l].astype(BF16), w_co[l].astype(BF16), norm_ffn[l].reshape(1, D_MODEL),
               w_gate[l].astype(BF16), w_up[l].astype(BF16), w_down[l].astype(BF16))
        final = l == depth - 1

        w_kv = jnp.concatenate([w_ck[l], w_cv[l]], axis=1).astype(BF16)
        mkf, mvf, mkb, mvb = _mem_kv(mem, w_kv, 512)
        attn_p = lambda q, k, v, lam_p, g, li: _attn_prompt(q, k, v, bias_tiles, lam_p, g, bp, sp, t_attn, li)
        xp, kf, vf, c1, n1, m1, cv1 = _layer(
            xp, l, wts, attn_p, mkb.reshape(bp, n_mem, -1), mvb.reshape(bp, n_mem, -1),
            zero_state, zero_conv, g_final, final, bp, sp, 512, l_prompt, 512)
        outs['pk'].append(kf.reshape(bp, sp, H_A, 2 * D_A))
        outs['pv'].append(vf.reshape(bp, sp, H_A, DV_A))
        outs['pmk'].append(mkf.reshape(bp, n_mem, H_X, D_X))
        outs['pmv'].append(mvf.reshape(bp, n_mem, H_X, D_X))
        outs['pC'].append(c1); outs['pn'].append(n1); outs['pm'].append(m1); outs['pcv'].append(cv1)

        k_past = cache_att_k[l].reshape(bs, past, ATT_WIDTH)
        v_past = cache_att_v[l].reshape(bs, past, ATT_WIDTH)
        attn_s = lambda q, k, v, lam_p, g, li: _attn_sample(q, k, v, k_past, v_past, bias_s_past, bias_s_new,
                                                            lam_p, g, bs, ss, li)
        s0 = _block_diag_state(state_mlstm_C[l], state_mlstm_n[l])
        m0 = jnp.pad(state_mlstm_m[l].astype(F32), ((0, 0), (0, 128 - H_M))).reshape(bs, 1, 128)
        xs, kf, vf, c1, n1, m1, cv1 = _layer(
            xs, l, wts, attn_s,
            cache_mem_k[l].reshape(bs, n_mem, -1).astype(BF16), cache_mem_v[l].reshape(bs, n_mem, -1).astype(BF16),
            (s0, m0), state_conv[l].astype(F32), g_final, final, bs, ss, bs * ss, ss, ss)
        outs['sk'].append(kf.reshape(bs, ss, H_A, 2 * D_A))
        outs['sv'].append(vf.reshape(bs, ss, H_A, DV_A))
        outs['sC'].append(c1); outs['sn'].append(n1); outs['sm'].append(m1); outs['scv'].append(cv1)

    st = lambda k: jnp.stack(outs[k])
    return (xp.reshape(bp, sp, D_MODEL), xs.reshape(bs, ss, D_MODEL),
            st('pk'), st('pv'), st('pmk'), st('pmv'), st('pC'), st('pn'), st('pm'), st('pcv'),
            st('sk'), st('sv'), st('sC'), st('sn'), st('sm'), st('scv'))
```

```python
import functools
import math

import jax
import jax.numpy as jnp
from jax import lax
from jax.experimental import pallas as pl
from jax.experimental.pallas import tpu as pltpu

F32 = jnp.float32
BF16 = jnp.bfloat16

D_MODEL = 1024
CHUNK = 64
H_A = 4
D_A = 64
DV_A = 128
ATT_WIDTH = H_A * DV_A
H_M = 4
D_M = 64
ML_WIDTH = H_M * D_M
CONV_CH = 256
CONV_K = 3
H_X = 4
D_X = 256
D_FF = 2816
NUM_BUCKETS = 32
MAX_DISTANCE = 128
EPS = 1e-6

N_PROJ = 3 * ATT_WIDTH + 4 * ML_WIDTH + 3 * CONV_CH
ML_STATE_W = ML_WIDTH + 128

V7X_VMEM_LIMIT = 56 * 1024 * 1024

LOG2E = math.log2(math.e)

_NT = (((1,), (1,)), ((), ()))
_TN = (((0,), (0,)), ((), ()))


def _const_spec(shape):
    nd = len(shape)
    return pl.BlockSpec(shape, lambda *_: (0,) * nd, pipeline_mode=pl.Buffered(1))


def _rms(x, g):
    ms = jnp.mean(x * x, axis=-1, keepdims=True)
    return x * lax.rsqrt(ms + EPS) * g


def _in_proj_kernel(x_ref, g_ref, w_ref, wgt_ref,
                    qa_ref, kf_ref, kb_ref, vf_ref, vb_ref, qkvm_ref, om_ref, bch_ref, gt_ref):
    xn = _rms(x_ref[...], g_ref[...]).astype(BF16)

    def proj(lo, hi):
        return jnp.dot(xn, w_ref[:, lo:hi], preferred_element_type=F32)

    a = ATT_WIDTH
    qa_ref[...] = (proj(0, a) * (D_A ** -0.5 * LOG2E)).astype(BF16)
    k = proj(a, 2 * a)
    kf_ref[...] = k
    kb_ref[...] = k.astype(BF16)
    v = proj(2 * a, 3 * a)
    vf_ref[...] = v
    vb_ref[...] = v.astype(BF16)
    o = 3 * a
    m = ML_WIDTH
    qkvm_ref[:, 0:m] = proj(o, o + m).astype(BF16)
    qkvm_ref[:, m:2 * m] = (proj(o + m, o + 2 * m) * (D_M ** -0.5)).astype(BF16)
    qkvm_ref[:, 2 * m:3 * m] = proj(o + 2 * m, o + 3 * m).astype(BF16)
    om_ref[...] = proj(o + 3 * m, o + 4 * m)
    bch_ref[...] = proj(o + 4 * m, N_PROJ)
    gt_ref[...] = lax.dot_general(wgt_ref[...], xn, _NT, preferred_element_type=F32)


def _in_proj(x, g, w, wgt, tm):
    t = x.shape[0]
    a, m = ATT_WIDTH, ML_WIDTH
    row = lambda n: pl.BlockSpec((tm, n), lambda i: (i, 0))
    outs = [
        (jax.ShapeDtypeStruct((t, a), BF16), row(a)),
        (jax.ShapeDtypeStruct((t, a), F32), row(a)),
        (jax.ShapeDtypeStruct((t, a), BF16), row(a)),
        (jax.ShapeDtypeStruct((t, a), F32), row(a)),
        (jax.ShapeDtypeStruct((t, a), BF16), row(a)),
        (jax.ShapeDtypeStruct((t, 3 * m), BF16), row(3 * m)),
        (jax.ShapeDtypeStruct((t, m), F32), row(m)),
        (jax.ShapeDtypeStruct((t, 3 * CONV_CH), F32), row(3 * CONV_CH)),
        (jax.ShapeDtypeStruct((8, t), F32), pl.BlockSpec((8, tm), lambda i: (0, i))),
    ]
    return pl.pallas_call(
        _in_proj_kernel,
        grid=(t // tm,),
        in_specs=[row(D_MODEL), _const_spec((1, D_MODEL)), _const_spec((D_MODEL, N_PROJ)),
                  _const_spec((8, D_MODEL))],
        out_specs=[s for _, s in outs],
        out_shape=[s for s, _ in outs],
        compiler_params=pltpu.CompilerParams(dimension_semantics=("parallel",),
                                             vmem_limit_bytes=V7X_VMEM_LIMIT),
        name="in_proj",
    )(x, g, w, wgt)


def _lambda_value(lp, lam_init):
    a = jnp.sum(lp[0:1] * lp[1:2], axis=1, keepdims=True)
    b = jnp.sum(lp[2:3] * lp[3:4], axis=1, keepdims=True)
    return jnp.exp(a) - jnp.exp(b) + lam_init


def _stack_maps(q):
    lane = lax.broadcasted_iota(jnp.int32, q.shape, 1)
    zero = jnp.zeros_like(q)
    return jnp.concatenate([jnp.where(lane < D_A, q, zero), jnp.where(lane >= D_A, q, zero)], axis=0)


def _attn_finish(acc, lam, g, lam_init, t):
    o = acc[:, :DV_A] / acc[:, DV_A:]
    d = o[:t] - lam * o[t:]
    return (_rms(d, g) * (1.0 - lam_init)).astype(BF16)


def _with_ones(v):
    return jnp.concatenate([v, jnp.ones_like(v)], axis=1)


def _attn_prompt_kernel(lam_ref, g_ref, q_ref, k_ref, v_ref, bias_ref, o_ref,
                        m_scr, l_scr, acc_scr, sa_scr, sb_scr, *, t, lam_init):
    i = pl.program_id(2)
    qz = _stack_maps(q_ref[...])

    def scores(kb):
        r = pl.ds(pl.multiple_of(kb * t, t), t)
        return lax.dot_general(k_ref[r, :], qz, _NT, preferred_element_type=F32), v_ref[r, :]

    def pv(p, v):
        return lax.dot_general(v, p.astype(BF16), _TN, preferred_element_type=F32)

    def update(state, s, v, weight=None):
        m_prev, l, acc = state
        m_new = jnp.maximum(m_prev, jnp.max(s, axis=0, keepdims=True))
        alpha = jnp.exp2(m_prev - m_new)
        p = jnp.exp2(s - m_new)
        p_sum = jnp.sum(p, axis=0, keepdims=True)
        p_v = pv(p, v)
        if weight is not None:
            p_sum, p_v = weight * p_sum, weight * p_v
        return m_new, alpha * l + p_sum, alpha * acc + p_v

    def load():
        return m_scr[...], l_scr[...], acc_scr[...]

    def store(state):
        m_scr[...], l_scr[...], acc_scr[...] = state

    n_far = jnp.maximum(i - 1, 0)
    last = n_far - 1

    s_diag, v_diag = scores(i)
    s_prev, v_prev = scores(jnp.maximum(i - 1, 0))
    sa_scr[...] = scores(0)[0]
    s_diag = s_diag + bias_ref[0, t:, :]
    m0 = jnp.max(s_diag, axis=0, keepdims=True)
    p = jnp.exp2(s_diag - m0)
    state = (m0, jnp.sum(p, axis=0, keepdims=True), pv(p, v_diag))
    no_prev = jnp.where(i >= 1, 0.0, -jnp.inf)
    store(update(state, s_prev + (bias_ref[0, :t, :] + no_prev), v_prev))


    def far_pair(j, carry):
        kb1 = jnp.minimum(2 * j + 1, last)
        w1 = (2 * j + 1 <= last).astype(F32)
        s1, v1 = scores(kb1)
        sb_scr[...] = s1
        state = update(load(), sa_scr[...], v_ref[pl.ds(pl.multiple_of(2 * j * t, t), t), :])
        sa_scr[...] = scores(jnp.minimum(2 * j + 2, last))[0]
        store(update(state, sb_scr[...], v1, w1))
        return carry

    lax.fori_loop(0, (n_far + 1) // 2, far_pair, 0)

    lam = _lambda_value(lam_ref[...], lam_init)
    o = acc_scr[...] * (1.0 / l_scr[...])
    d = o[:, :t] - lam * o[:, t:]
    ms = jnp.mean(d * d, axis=0, keepdims=True)
    y = jnp.transpose(d * lax.rsqrt(ms + EPS))
    o_ref[...] = (y * g_ref[...] * (1.0 - lam_init)).astype(BF16)


def _attn_prompt(q, k, v, bias, lam_p, g_att, batch, seq, t, lam_init):
    nq = seq // t
    kv_spec = pl.BlockSpec((seq, DV_A), lambda b, h, i: (b, h))
    return pl.pallas_call(
        functools.partial(_attn_prompt_kernel, t=t, lam_init=lam_init),
        grid=(batch, H_A, nq),
        in_specs=[_const_spec((4, D_A)),
                  pl.BlockSpec((1, DV_A), lambda b, h, i: (0, h)),
                  pl.BlockSpec((t, DV_A), lambda b, h, i: (b * nq + i, h)),
                  kv_spec, kv_spec,
                  pl.BlockSpec((1, 2 * t, 2 * t), lambda b, h, i: (h, 0, 0))],
        out_specs=pl.BlockSpec((t, DV_A), lambda b, h, i: (b * nq + i, h)),
        out_shape=jax.ShapeDtypeStruct((batch * seq, ATT_WIDTH), BF16),
        scratch_shapes=[pltpu.VMEM((1, 2 * t), F32), pltpu.VMEM((1, 2 * t), F32),
                        pltpu.VMEM((DV_A, 2 * t), F32),
                        pltpu.VMEM((t, 2 * t), F32), pltpu.VMEM((t, 2 * t), F32)],
        compiler_params=pltpu.CompilerParams(dimension_semantics=("parallel", "parallel", "arbitrary"),
                                             vmem_limit_bytes=V7X_VMEM_LIMIT),
        name="attn_prompt",
    )(lam_p, g_att, q, k, v, bias)


def _attn_sample_kernel(lam_ref, g_ref, q_ref, kp_ref, vp_ref, kn_ref, vn_ref, bp_ref, bn_ref, o_ref,
                        *, lq, lam_init):
    qz = _stack_maps(q_ref[...])
    bp = bp_ref[0]
    bn = bn_ref[0]
    sp = lax.dot_general(qz, kp_ref[0].astype(BF16), _NT, preferred_element_type=F32)
    sp = sp + jnp.concatenate([bp, bp], axis=0)
    sn = lax.dot_general(qz, kn_ref[...], _NT, preferred_element_type=F32)
    sn = sn + jnp.concatenate([bn, bn], axis=0)
    m = jnp.maximum(jnp.max(sp, axis=1, keepdims=True), jnp.max(sn, axis=1, keepdims=True))
    pp = jnp.exp2(sp - m)
    pn = jnp.exp2(sn - m)
    acc = (jnp.dot(pp.astype(BF16), _with_ones(vp_ref[0].astype(BF16)), preferred_element_type=F32)
           + jnp.dot(pn.astype(BF16), _with_ones(vn_ref[...]), preferred_element_type=F32))
    lam = _lambda_value(lam_ref[...], lam_init)
    o_ref[...] = _attn_finish(acc, lam, g_ref[...], lam_init, lq)


def _attn_sample(q, k_new, v_new, k_past, v_past, bias_past, bias_new, lam_p, g_att, batch, lq, lam_init):
    past = k_past.shape[1]
    new_spec = pl.BlockSpec((lq, DV_A), lambda b, h: (b, h))
    past_spec = pl.BlockSpec((1, past, DV_A), lambda b, h: (b, 0, h))
    return pl.pallas_call(
        functools.partial(_attn_sample_kernel, lq=lq, lam_init=lam_init),
        grid=(batch, H_A),
        in_specs=[_const_spec((4, D_A)),
                  pl.BlockSpec((1, DV_A), lambda b, h: (0, h)),
                  new_spec, past_spec, past_spec, new_spec, new_spec,
                  pl.BlockSpec((1, lq, past), lambda b, h: (h, 0, 0)),
                  pl.BlockSpec((1, lq, lq), lambda b, h: (h, 0, 0))],
        out_specs=new_spec,
        out_shape=jax.ShapeDtypeStruct((batch * lq, ATT_WIDTH), BF16),
        compiler_params=pltpu.CompilerParams(dimension_semantics=("parallel", "parallel"),
                                             vmem_limit_bytes=V7X_VMEM_LIMIT),
        name="attn_sample",
    )(lam_p, g_att, q, k_past, v_past, k_new, v_new, bias_past, bias_new)


def _recurrent_kernel(gb_ref, q_ref, k_ref, v_ref, og_ref, gt_ref, b_ref, c_ref, hc_ref, cw_ref, gml_ref,
                      s0_ref, m0_ref, cv0_ref,
                      hm_ref, oc_ref, s_ref, m_ref, cv_ref, *, L):
    @pl.when(pl.program_id(1) == 0)
    def _():
        s_ref[...] = s0_ref[...]
        m_ref[...] = m0_ref[...]
        cv_ref[...] = cv0_ref[...]

    q = q_ref[...]
    k = k_ref[...]
    v = v_ref[...]
    state = s_ref[0]
    q_state = jnp.dot(q, state.astype(BF16), preferred_element_type=F32)
    q_c = q_state[:, :ML_WIDTH]

    lane_head = lax.broadcasted_iota(jnp.int32, (L, ML_WIDTH), 1) // D_M
    row = lax.broadcasted_iota(jnp.int32, (L, L), 0)
    col = lax.broadcasted_iota(jnp.int32, (L, L), 1)
    causal = col <= row
    eye = col == row
    st_lane = lax.broadcasted_iota(jnp.int32, (1, ML_STATE_W), 1)
    st_lane_head = jnp.where(st_lane < ML_WIDTH, st_lane // D_M, st_lane - ML_WIDTH)
    m_lane = lax.broadcasted_iota(jnp.int32, (1, 128), 1)

    gt = gt_ref[0]
    m_all = m_ref[0]
    num = jnp.zeros((L, ML_WIDTH), F32)
    w_state = jnp.zeros((L, ML_WIDTH), F32)
    decay = jnp.zeros((1, ML_STATE_W), F32)
    m_next = jnp.zeros((1, 128), F32)
    for h in range(H_M):
        ig = gt[h:h + 1, :] + gb_ref[0, h]
        fz = gt[H_M + h:H_M + h + 1, :] + gb_ref[1, h]
        lf = jnp.minimum(fz, 0.0) - jnp.log1p(jnp.exp(-jnp.abs(fz)))
        f_col = jnp.sum(jnp.where(causal, lf, 0.0), axis=1, keepdims=True)
        f_row = jnp.sum(jnp.where(eye, f_col, 0.0), axis=0, keepdims=True)
        d = jnp.where(causal, f_col + (ig - f_row), -jnp.inf)
        m_prev = m_all[:, h:h + 1]
        inter = f_col + m_prev
        mt = jnp.maximum(inter, jnp.max(d, axis=1, keepdims=True))
        w_intra = jnp.exp(d - mt)
        w_inter = jnp.exp(inter - mt)
        head = lane_head == h
        qk = lax.dot_general(jnp.where(head, q, jnp.zeros_like(q)), k, _NT, preferred_element_type=F32)
        s = qk * w_intra
        den = jnp.sum(s, axis=1, keepdims=True) + w_inter * q_state[:, ML_WIDTH + h:ML_WIDTH + h + 1]
        inv = 1.0 / jnp.maximum(jnp.abs(den), jnp.exp(-mt))
        sv = jnp.dot(s.astype(BF16), v, preferred_element_type=F32)
        num = jnp.where(head, (sv + w_inter * q_c) * inv, num)
        m_new = mt[L - 1:L, :]
        w_col = jnp.sum(jnp.where(eye, w_intra[L - 1:L, :], 0.0), axis=1, keepdims=True)
        w_state = jnp.where(head, w_col, w_state)
        dec = jnp.exp(f_col[L - 1:L, :] + m_prev - m_new)
        decay = jnp.where(st_lane_head == h, dec, decay)
        m_next = jnp.where(m_lane == h, m_new, m_next)

    kw = (k.astype(F32) * w_state).astype(BF16)
    v_aug = jnp.concatenate([v, jnp.ones((L, ML_STATE_W - ML_WIDTH), BF16)], axis=1)
    upd = lax.dot_general(kw, v_aug, _TN, preferred_element_type=F32)
    s_row_head = lax.broadcasted_iota(jnp.int32, (ML_WIDTH, ML_STATE_W), 0) // D_M
    s_col = lax.broadcasted_iota(jnp.int32, (ML_WIDTH, ML_STATE_W), 1)
    s_col_head = jnp.where(s_col < ML_WIDTH, s_col // D_M, s_col - ML_WIDTH)
    s_ref[0] = decay * state + jnp.where(s_row_head == s_col_head, upd, 0.0)
    m_ref[0] = m_next

    hg = num * jax.nn.sigmoid(og_ref[...])
    scale = jnp.zeros((L, ML_WIDTH), F32)
    for h in range(H_M):
        head = lane_head == h
        ss = jnp.sum(jnp.where(head, hg * hg, 0.0), axis=1, keepdims=True)
        scale = jnp.where(head, lax.rsqrt(ss * (1.0 / D_M) + EPS), scale)
    hm_ref[...] = (hg * scale * gml_ref[...]).astype(BF16)

    u = c_ref[...] * hc_ref[...]
    prev = cv_ref[0]
    ri = lax.broadcasted_iota(jnp.int32, u.shape, 0)
    u1 = jnp.where(ri == 0, prev[1:2], pltpu.roll(u, 1, 0))
    u2 = jnp.where(ri == 0, prev[0:1], jnp.where(ri == 1, prev[1:2], pltpu.roll(u, 2, 0)))
    w = cw_ref[...]
    oc_ref[...] = (b_ref[...] * (w[0:1] * u2 + w[1:2] * u1 + w[2:3] * u)).astype(BF16)
    cv_ref[0, 0:1, :] = u[L - 2:L - 1]
    cv_ref[0, 1:2, :] = u[L - 1:L]


def _recurrent(gate_b, qkvm, og, gt, bch, conv_w, g_ml, s0, m0, cv0, batch, seq, L):
    nc = seq // L
    t = batch * seq
    tok = lambda j: pl.BlockSpec((L, ML_WIDTH), lambda b, c: (b * nc + c, j))
    per_b = lambda shape: pl.BlockSpec((1,) + shape, lambda b, c: (b, 0, 0))
    return pl.pallas_call(
        functools.partial(_recurrent_kernel, L=L),
        grid=(batch, nc),
        in_specs=[pl.BlockSpec(memory_space=pltpu.SMEM),
                  tok(0), tok(1), tok(2), tok(0),
                  pl.BlockSpec((1, 8, L), lambda b, c: (b * nc + c, 0, 0)),
                  tok(0), tok(1), tok(2),
                  _const_spec((CONV_K, CONV_CH)), _const_spec((1, ML_WIDTH)),
                  per_b((ML_WIDTH, ML_STATE_W)), per_b((1, 128)), per_b((CONV_K - 1, CONV_CH))],
        out_specs=[tok(0), tok(0),
                   per_b((ML_WIDTH, ML_STATE_W)), per_b((1, 128)), per_b((CONV_K - 1, CONV_CH))],
        out_shape=[jax.ShapeDtypeStruct((t, ML_WIDTH), BF16),
                   jax.ShapeDtypeStruct((t, CONV_CH), BF16),
                   jax.ShapeDtypeStruct((batch, ML_WIDTH, ML_STATE_W), F32),
                   jax.ShapeDtypeStruct((batch, 1, 128), F32),
                   jax.ShapeDtypeStruct((batch, CONV_K - 1, CONV_CH), F32)],
        compiler_params=pltpu.CompilerParams(dimension_semantics=("parallel", "arbitrary"),
                                             vmem_limit_bytes=V7X_VMEM_LIMIT),
        name="recurrent",
    )(gate_b, qkvm, qkvm, qkvm, og, gt, bch, bch, bch, conv_w, g_ml, s0, m0, cv0)


def _mem_kv_kernel(x_ref, w_ref, kf_ref, vf_ref, kb_ref, vb_ref):
    x = x_ref[...].astype(BF16)
    n = H_X * D_X
    k = jnp.dot(x, w_ref[:, :n], preferred_element_type=F32)
    v = jnp.dot(x, w_ref[:, n:], preferred_element_type=F32)
    kf_ref[...] = k
    vf_ref[...] = v
    kb_ref[...] = k.astype(BF16)
    vb_ref[...] = v.astype(BF16)


def _mem_kv(mem, w_kv, tm):
    t = mem.shape[0]
    n = H_X * D_X
    row = pl.BlockSpec((tm, n), lambda i: (i, 0))
    return pl.pallas_call(
        _mem_kv_kernel,
        grid=(t // tm,),
        in_specs=[pl.BlockSpec((tm, D_MODEL), lambda i: (i, 0)), _const_spec((D_MODEL, 2 * n))],
        out_specs=[row, row, row, row],
        out_shape=[jax.ShapeDtypeStruct((t, n), F32), jax.ShapeDtypeStruct((t, n), F32),
                   jax.ShapeDtypeStruct((t, n), BF16), jax.ShapeDtypeStruct((t, n), BF16)],
        compiler_params=pltpu.CompilerParams(dimension_semantics=("parallel",),
                                             vmem_limit_bytes=V7X_VMEM_LIMIT),
        name="mem_kv",
    )(mem, w_kv)


FF_CHUNK = 256


def _post_kernel(x_ref, oa_ref, hm_ref, oc_ref, mk_ref, mv_ref, wout_ref, gc_ref, wcq_ref, wco_ref,
                 gf_ref, wg_ref, wu_ref, wd_ref, gfin_ref, o_ref, *, final):
    a, m = ATT_WIDTH, ML_WIDTH
    x = x_ref[...]
    x = x + (jnp.dot(oa_ref[...], wout_ref[0:a, :], preferred_element_type=F32)
             + jnp.dot(hm_ref[...], wout_ref[a:a + m, :], preferred_element_type=F32)
             + jnp.dot(oc_ref[...], wout_ref[a + m:, :], preferred_element_type=F32))

    xn = _rms(x, gc_ref[...]).astype(BF16)
    qc = (jnp.dot(xn, wcq_ref[...], preferred_element_type=F32) * (D_X ** -0.5)).astype(BF16)
    cross = jnp.zeros_like(x)
    for h in range(H_X):
        sl = slice(h * D_X, (h + 1) * D_X)
        s = lax.dot_general(qc[:, sl], mk_ref[0, :, sl], _NT, preferred_element_type=F32)
        p = jnp.exp(s - jnp.max(s, axis=1, keepdims=True))
        p = p * (1.0 / jnp.sum(p, axis=1, keepdims=True))
        o = jnp.dot(p.astype(BF16), mv_ref[0, :, sl], preferred_element_type=F32)
        cross = cross + jnp.dot(o.astype(BF16), wco_ref[sl, :], preferred_element_type=F32)
    x = x + cross

    xn = _rms(x, gf_ref[...]).astype(BF16)
    ff = jnp.zeros_like(x)
    for j in range(D_FF // FF_CHUNK):
        sl = slice(j * FF_CHUNK, (j + 1) * FF_CHUNK)
        g = jnp.dot(xn, wg_ref[:, sl], preferred_element_type=F32)
        u = jnp.dot(xn, wu_ref[:, sl], preferred_element_type=F32)
        act = (g * jax.nn.sigmoid(g) * u).astype(BF16)
        ff = ff + jnp.dot(act, wd_ref[sl, :], preferred_element_type=F32)
    x = x + ff
    if final:
        x = _rms(x, gfin_ref[...])
    o_ref[...] = x


def _post(x, oa, hm, oc, mk, mv, w_out, g_cross, w_cq, w_co, g_ffn, w_gate, w_up, w_down, g_final,
          tm, tiles_per_stream, final):
    t = x.shape[0]
    row = lambda n: pl.BlockSpec((tm, n), lambda i: (i, 0))
    mem = pl.BlockSpec((1,) + mk.shape[1:], lambda i: (i // tiles_per_stream, 0, 0))
    vec = _const_spec((1, D_MODEL))
    return pl.pallas_call(
        functools.partial(_post_kernel, final=final),
        grid=(t // tm,),
        in_specs=[row(D_MODEL), row(ATT_WIDTH), row(ML_WIDTH), row(CONV_CH), mem, mem,
                  _const_spec(w_out.shape), vec, _const_spec(w_cq.shape), _const_spec(w_co.shape),
                  vec, _const_spec(w_gate.shape), _const_spec(w_up.shape), _const_spec(w_down.shape), vec],
        out_specs=row(D_MODEL),
        out_shape=jax.ShapeDtypeStruct((t, D_MODEL), F32),
        compiler_params=pltpu.CompilerParams(dimension_semantics=("parallel",),
                                             vmem_limit_bytes=V7X_VMEM_LIMIT),
        name="post",
    )(x, oa, hm, oc, mk, mv, w_out, g_cross, w_cq, w_co, g_ffn, w_gate, w_up, w_down, g_final)


def _rel_bucket(rel):
    half = NUM_BUCKETS // 2
    max_exact = half // 2
    n = jnp.abs(rel)
    large = max_exact + (jnp.log(jnp.maximum(n, 1).astype(F32) / max_exact)
                         / math.log(MAX_DISTANCE / max_exact) * (half - max_exact)).astype(jnp.int32)
    large = jnp.minimum(large, half - 1)
    return jnp.where(rel > 0, half, 0) + jnp.where(n < max_exact, n, large)


def _bias_table(rel_bias, rel):
    bucket = _rel_bucket(rel)[None]
    table = rel_bias.astype(F32)
    out = jnp.zeros((table.shape[1],) + rel.shape, F32)
    for b in range(NUM_BUCKETS):
        out = jnp.where(bucket == b, table[b][:, None, None], out)
    return out


def _prompt_bias_tiles(rel_bias, t):
    key = jnp.arange(2 * t, dtype=jnp.int32)[:, None] - t
    qry = (jnp.arange(2 * t, dtype=jnp.int32) % t)[None, :]
    far = rel_bias[_rel_bucket(jnp.int32(-MAX_DISTANCE))].astype(F32)[:, None, None]
    visible = (key // CHUNK) <= (qry // CHUNK)
    return jnp.where(visible, (_bias_table(rel_bias, key - qry) - far) * LOG2E, -jnp.inf)


def _block_diag_state(c, n):
    b = c.shape[0]
    eye = jnp.eye(H_M, dtype=F32)
    cbd = jnp.einsum('bhdv,hg->bhdgv', c.astype(F32), eye).reshape(b, ML_WIDTH, ML_WIDTH)
    ncol = jnp.einsum('bhd,hg->bhdg', n.astype(F32), eye).reshape(b, ML_WIDTH, H_M)
    pad = jnp.zeros((b, ML_WIDTH, ML_STATE_W - ML_WIDTH - H_M), F32)
    return jnp.concatenate([cbd, ncol, pad], axis=2)


def _unpack_state(s, m):
    b = s.shape[0]
    blocks = s[:, :, :ML_WIDTH].reshape(b, H_M, D_M, H_M, D_M)
    c = jnp.stack([blocks[:, h, :, h, :] for h in range(H_M)], axis=1)
    ncols = s[:, :, ML_WIDTH:ML_WIDTH + H_M].reshape(b, H_M, D_M, H_M)
    n = jnp.stack([ncols[:, h, :, h] for h in range(H_M)], axis=1)
    return c, n, m[:, 0, :H_M]


def _layer(x, layer, wts, attn_fn, mem_k, mem_v, ml_state, conv_prev, g_final, final,
           batch, seq, tm_proj, L, tm_post):
    (g_mix, w_proj, w_gt, gate_b, conv_w, lam_p, g_att, g_ml, w_out,
     g_cross, w_cq, w_co, g_ffn, w_gate, w_up, w_down) = wts
    t = batch * seq
    lam_init = 0.8 - 0.6 * math.exp(-0.3 * layer)
    qa, kf, kb, vf, vb, qkvm, og, bch, gt = _in_proj(x, g_mix, w_proj, w_gt, tm_proj)
    oa = attn_fn(qa, kb, vb, lam_p, g_att, lam_init)
    gt = jnp.transpose(gt.reshape(8, t // L, L), (1, 0, 2))
    hm, oc, s1, m1, cv1 = _recurrent(gate_b, qkvm, og, gt, bch, conv_w, g_ml,
                                     ml_state[0], ml_state[1], conv_prev, batch, seq, L)
    x = _post(x, oa, hm, oc, mem_k, mem_v, w_out, g_cross, w_cq, w_co, g_ffn, w_gate, w_up, w_down,
              g_final, tm_post, seq // tm_post, final)
    c1, n1, mm1 = _unpack_state(s1, m1)
    return x, kf, vf, c1, n1, mm1, cv1


def kernel(x_prompt, x_sample, mem_prompt, cache_att_k, cache_att_v, cache_mem_k, cache_mem_v,
           state_mlstm_C, state_mlstm_n, state_mlstm_m, state_conv,
           norm_mix, w_in, mlstm_gate_bias, conv_w, lambda_params, norm_att_heads, norm_mlstm_heads,
           w_out, norm_cross, w_cq, w_ck, w_cv, w_co, norm_ffn, w_gate, w_up, w_down,
           rel_bias, norm_final):
    bp, sp, _ = x_prompt.shape
    bs, ss, _ = x_sample.shape
    depth = w_in.shape[0]
    past = cache_att_k.shape[2]
    n_mem = mem_prompt.shape[1]
    t_attn = 256
    l_prompt = 128

    xp = x_prompt.reshape(bp * sp, D_MODEL)
    xs = x_sample.reshape(bs * ss, D_MODEL)
    mem = mem_prompt.reshape(bp * n_mem, D_MODEL)
    g_final = norm_final.reshape(1, D_MODEL)

    bias_tiles = _prompt_bias_tiles(rel_bias, t_attn)
    rel_s = jnp.arange(past + ss, dtype=jnp.int32)[None, :] - (past + jnp.arange(ss, dtype=jnp.int32))[:, None]
    bias_s = _bias_table(rel_bias, rel_s) * LOG2E
    bias_s_past, bias_s_new = bias_s[:, :, :past], bias_s[:, :, past:]

    zero_state = (jnp.zeros((bp, ML_WIDTH, ML_STATE_W), F32), jnp.zeros((bp, 1, 128), F32))
    zero_conv = jnp.zeros((bp, CONV_K - 1, CONV_CH), F32)

    outs = {k: [] for k in ('pk', 'pv', 'pmk', 'pmv', 'pC', 'pn', 'pm', 'pcv', 'sk', 'sv', 'sC', 'sn', 'sm', 'scv')}
    gate_lo = 3 * ATT_WIDTH + 4 * ML_WIDTH
    for l in range(depth):
        w = w_in[l]
        wts = (norm_mix[l].reshape(1, D_MODEL),
               jnp.concatenate([w[:, :gate_lo], w[:, gate_lo + 2 * H_M:]], axis=1).astype(BF16),
               w[:, gate_lo:gate_lo + 2 * H_M].T.astype(BF16),
               mlstm_gate_bias[l].astype(F32), conv_w[l].astype(F32), lambda_params[l].astype(F32),
               norm_att_heads[l].reshape(1, ATT_WIDTH), norm_mlstm_heads[l].reshape(1, ML_WIDTH),
               w_out[l].astype(BF16), norm_cross[l].reshape(1, D_MODEL),
               w_cq[l].astype(BF16), w_co[l].astype(BF16), norm_ffn[l].reshape(1, D_MODEL),
               w_gate[l].astype(BF16), w_up[l].astype(BF16), w_down[l].astype(BF16))
        final = l == depth - 1

        w_kv = jnp.concatenate([w_ck[l], w_cv[l]], axis=1).astype(BF16)
        mkf, mvf, mkb, mvb = _mem_kv(mem, w_kv, 512)
        attn_p = lambda q, k, v, lam_p, g, li: _attn_prompt(q, k, v, bias_tiles, lam_p, g, bp, sp, t_attn, li)
        xp, kf, vf, c1, n1, m1, cv1 = _layer(
            xp, l, wts, attn_p, mkb.reshape(bp, n_mem, -1), mvb.reshape(bp, n_mem, -1),
            zero_state, zero_conv, g_final, final, bp, sp, 512, l_prompt, 512)
        outs['pk'].append(kf.reshape(bp, sp, H_A, 2 * D_A))
        outs['pv'].append(vf.reshape(bp, sp, H_A, DV_A))
        outs['pmk'].append(mkf.reshape(bp, n_mem, H_X, D_X))
        outs['pmv'].append(mvf.reshape(bp, n_mem, H_X, D_X))
        outs['pC'].append(c1); outs['pn'].append(n1); outs['pm'].append(m1); outs['pcv'].append(cv1)

        k_past = cache_att_k[l].reshape(bs, past, ATT_WIDTH)
        v_past = cache_att_v[l].reshape(bs, past, ATT_WIDTH)
        attn_s = lambda q, k, v, lam_p, g, li: _attn_sample(q, k, v, k_past, v_past, bias_s_past, bias_s_new,
                                                            lam_p, g, bs, ss, li)
        s0 = _block_diag_state(state_mlstm_C[l], state_mlstm_n[l])
        m0 = jnp.pad(state_mlstm_m[l].astype(F32), ((0, 0), (0, 128 - H_M))).reshape(bs, 1, 128)
        xs, kf, vf, c1, n1, m1, cv1 = _layer(
            xs, l, wts, attn_s,
            cache_mem_k[l].reshape(bs, n_mem, -1).astype(BF16), cache_mem_v[l].reshape(bs, n_mem, -1).astype(BF16),
            (s0, m0), state_conv[l].astype(F32), g_final, final, bs, ss, bs * ss, ss, ss)
        outs['sk'].append(kf.reshape(bs, ss, H_A, 2 * D_A))
        outs['sv'].append(vf.reshape(bs, ss, H_A, DV_A))
        outs['sC'].append(c1); outs['sn'].append(n1); outs['sm'].append(m1); outs['scv'].append(cv1)

    st = lambda k: jnp.stack(outs[k])
    return (xp.reshape(bp, sp, D_MODEL), xs.reshape(bs, ss, D_MODEL),
            st('pk'), st('pv'), st('pmk'), st('pmv'), st('pC'), st('pn'), st('pm'), st('pcv'),
            st('sk'), st('sv'), st('sC'), st('sn'), st('sm'), st('scv'))
```

```python
import functools
import math

import jax
import jax.numpy as jnp
from jax import lax
from jax.experimental import pallas as pl
from jax.experimental.pallas import tpu as pltpu

F32 = jnp.float32
BF16 = jnp.bfloat16

D_MODEL = 1024
CHUNK = 64
H_A = 4
D_A = 64
DV_A = 128
ATT_WIDTH = H_A * DV_A
H_M = 4
D_M = 64
ML_WIDTH = H_M * D_M
CONV_CH = 256
CONV_K = 3
H_X = 4
D_X = 256
D_FF = 2816
NUM_BUCKETS = 32
MAX_DISTANCE = 128
EPS = 1e-6

N_PROJ = 3 * ATT_WIDTH + 4 * ML_WIDTH + 3 * CONV_CH
ML_STATE_W = ML_WIDTH + 128

V7X_VMEM_LIMIT = 56 * 1024 * 1024

LOG2E = math.log2(math.e)

_NT = (((1,), (1,)), ((), ()))
_TN = (((0,), (0,)), ((), ()))


def _const_spec(shape):
    nd = len(shape)
    return pl.BlockSpec(shape, lambda *_: (0,) * nd, pipeline_mode=pl.Buffered(1))


def _rms(x, g):
    ms = jnp.mean(x * x, axis=-1, keepdims=True)
    return x * lax.rsqrt(ms + EPS) * g


def _in_proj_kernel(x_ref, g_ref, w_ref, wgt_ref,
                    qa_ref, kf_ref, kb_ref, vf_ref, vb_ref, qkvm_ref, om_ref, bch_ref, gt_ref):
    xn = _rms(x_ref[...], g_ref[...]).astype(BF16)

    def proj(lo, hi):
        return jnp.dot(xn, w_ref[:, lo:hi], preferred_element_type=F32)

    a = ATT_WIDTH
    qa_ref[...] = (proj(0, a) * (D_A ** -0.5 * LOG2E)).astype(BF16)
    k = proj(a, 2 * a)
    kf_ref[...] = k
    kb_ref[...] = k.astype(BF16)
    v = proj(2 * a, 3 * a)
    vf_ref[...] = v
    vb_ref[...] = v.astype(BF16)
    o = 3 * a
    m = ML_WIDTH
    qkvm_ref[:, 0:m] = proj(o, o + m).astype(BF16)
    qkvm_ref[:, m:2 * m] = (proj(o + m, o + 2 * m) * (D_M ** -0.5)).astype(BF16)
    qkvm_ref[:, 2 * m:3 * m] = proj(o + 2 * m, o + 3 * m).astype(BF16)
    om_ref[...] = proj(o + 3 * m, o + 4 * m)
    bch_ref[...] = proj(o + 4 * m, N_PROJ)
    gt_ref[...] = lax.dot_general(wgt_ref[...], xn, _NT, preferred_element_type=F32)


def _in_proj(x, g, w, wgt, tm):
    t = x.shape[0]
    a, m = ATT_WIDTH, ML_WIDTH
    row = lambda n: pl.BlockSpec((tm, n), lambda i: (i, 0))
    outs = [
        (jax.ShapeDtypeStruct((t, a), BF16), row(a)),
        (jax.ShapeDtypeStruct((t, a), F32), row(a)),
        (jax.ShapeDtypeStruct((t, a), BF16), row(a)),
        (jax.ShapeDtypeStruct((t, a), F32), row(a)),
        (jax.ShapeDtypeStruct((t, a), BF16), row(a)),
        (jax.ShapeDtypeStruct((t, 3 * m), BF16), row(3 * m)),
        (jax.ShapeDtypeStruct((t, m), F32), row(m)),
        (jax.ShapeDtypeStruct((t, 3 * CONV_CH), F32), row(3 * CONV_CH)),
        (jax.ShapeDtypeStruct((8, t), F32), pl.BlockSpec((8, tm), lambda i: (0, i))),
    ]
    return pl.pallas_call(
        _in_proj_kernel,
        grid=(t // tm,),
        in_specs=[row(D_MODEL), _const_spec((1, D_MODEL)), _const_spec((D_MODEL, N_PROJ)),
                  _const_spec((8, D_MODEL))],
        out_specs=[s for _, s in outs],
        out_shape=[s for s, _ in outs],
        compiler_params=pltpu.CompilerParams(dimension_semantics=("parallel",),
                                             vmem_limit_bytes=V7X_VMEM_LIMIT),
        name="in_proj",
    )(x, g, w, wgt)


def _lambda_value(lp, lam_init):
    a = jnp.sum(lp[0:1] * lp[1:2], axis=1, keepdims=True)
    b = jnp.sum(lp[2:3] * lp[3:4], axis=1, keepdims=True)
    return jnp.exp(a) - jnp.exp(b) + lam_init


def _stack_maps(q):
    lane = lax.broadcasted_iota(jnp.int32, q.shape, 1)
    zero = jnp.zeros_like(q)
    return jnp.concatenate([jnp.where(lane < D_A, q, zero), jnp.where(lane >= D_A, q, zero)], axis=0)


def _attn_finish(acc, lam, g, lam_init, t):
    o = acc[:, :DV_A] / acc[:, DV_A:]
    d = o[:t] - lam * o[t:]
    return (_rms(d, g) * (1.0 - lam_init)).astype(BF16)


def _with_ones(v):
    return jnp.concatenate([v, jnp.ones_like(v)], axis=1)


def _attn_prompt_kernel(lam_ref, g_ref, q_ref, k_ref, v_ref, bias_ref, o_ref,
                        m_scr, l_scr, acc_scr, sa_scr, sb_scr, p_scr, *, t, lam_init):
    i = pl.program_id(2)
    qz = _stack_maps(q_ref[...])

    def rows(kb):
        return pl.ds(pl.multiple_of(kb * t, t), t)

    def qk(kb):
        return lax.dot_general(k_ref[rows(kb), :], qz, _NT, preferred_element_type=F32)

    def pv(p, v):
        return lax.dot_general(v, p, _TN, preferred_element_type=F32)

    def softmax_step(m_prev, s):
        m_new = jnp.maximum(m_prev, jnp.max(s, axis=0, keepdims=True))
        p = jnp.exp2(s - m_new)
        return m_new, jnp.exp2(m_prev - m_new), p, jnp.sum(p, axis=0, keepdims=True)

    n_far = jnp.maximum(i - 1, 0)
    last = n_far - 1
    kb_prev = jnp.maximum(i - 1, 0)

    s_diag = qk(i)
    s_prev = qk(kb_prev)
    sa_scr[...] = qk(0)
    s_diag = s_diag + bias_ref[0, t:, :]
    m = jnp.max(s_diag, axis=0, keepdims=True)
    p = jnp.exp2(s_diag - m)
    l = jnp.sum(p, axis=0, keepdims=True)
    acc = pv(p.astype(BF16), v_ref[rows(i), :])
    no_prev = jnp.where(i >= 1, 0.0, -jnp.inf)
    m, alpha, p, p_sum = softmax_step(m, s_prev + (bias_ref[0, :t, :] + no_prev))
    m_scr[...] = m
    l_scr[...] = alpha * l + p_sum
    acc_scr[...] = alpha * acc
    p_scr[...] = p.astype(BF16)

    def far_pair(j, carry):
        kb1 = jnp.minimum(2 * j + 1, last)
        w1 = (2 * j + 1 <= last).astype(F32)
        kb_pending = jnp.where(j == 0, kb_prev, 2 * j - 1)
        sb_scr[...] = qk(kb1)
        owed = pv(p_scr[...], v_ref[rows(kb_pending), :])
        m, alpha, p, p_sum = softmax_step(m_scr[...], sa_scr[...])
        l = alpha * l_scr[...] + p_sum
        acc = alpha * (acc_scr[...] + owed)
        sa_scr[...] = qk(jnp.minimum(2 * j + 2, last))
        owed = pv(p.astype(BF16), v_ref[rows(2 * j), :])
        m, alpha, p, p_sum = softmax_step(m, sb_scr[...])
        m_scr[...] = m
        l_scr[...] = alpha * l + w1 * p_sum
        acc_scr[...] = alpha * (acc + owed)
        p_scr[...] = p.astype(BF16)
        return carry

    trips = (n_far + 1) // 2
    lax.fori_loop(0, trips, far_pair, 0)

    kb_pending = jnp.where(trips == 0, kb_prev, jnp.minimum(2 * trips - 1, last))
    w_pending = jnp.where((trips == 0) | (n_far % 2 == 0), 1.0, 0.0)
    acc = acc_scr[...] + w_pending * pv(p_scr[...], v_ref[rows(kb_pending), :])

    lam = _lambda_value(lam_ref[...], lam_init)
    o = acc * (1.0 / l_scr[...])
    d = o[:, :t] - lam * o[:, t:]
    ms = jnp.mean(d * d, axis=0, keepdims=True)
    y = jnp.transpose(d * lax.rsqrt(ms + EPS))
    o_ref[...] = (y * g_ref[...] * (1.0 - lam_init)).astype(BF16)


def _attn_prompt(q, k, v, bias, lam_p, g_att, batch, seq, t, lam_init):
    nq = seq // t
    kv_spec = pl.BlockSpec((seq, DV_A), lambda b, h, i: (b, h))
    return pl.pallas_call(
        functools.partial(_attn_prompt_kernel, t=t, lam_init=lam_init),
        grid=(batch, H_A, nq),
        in_specs=[_const_spec((4, D_A)),
                  pl.BlockSpec((1, DV_A), lambda b, h, i: (0, h)),
                  pl.BlockSpec((t, DV_A), lambda b, h, i: (b * nq + i, h)),
                  kv_spec, kv_spec,
                  pl.BlockSpec((1, 2 * t, 2 * t), lambda b, h, i: (h, 0, 0))],
        out_specs=pl.BlockSpec((t, DV_A), lambda b, h, i: (b * nq + i, h)),
        out_shape=jax.ShapeDtypeStruct((batch * seq, ATT_WIDTH), BF16),
        scratch_shapes=[pltpu.VMEM((1, 2 * t), F32), pltpu.VMEM((1, 2 * t), F32),
                        pltpu.VMEM((DV_A, 2 * t), F32),
                        pltpu.VMEM((t, 2 * t), F32), pltpu.VMEM((t, 2 * t), F32),
                        pltpu.VMEM((t, 2 * t), BF16)],
        compiler_params=pltpu.CompilerParams(dimension_semantics=("parallel", "parallel", "arbitrary"),
                                             vmem_limit_bytes=V7X_VMEM_LIMIT),
        name="attn_prompt",
    )(lam_p, g_att, q, k, v, bias)


def _attn_sample_kernel(lam_ref, g_ref, q_ref, kp_ref, vp_ref, kn_ref, vn_ref, bp_ref, bn_ref, o_ref,
                        *, lq, lam_init):
    qz = _stack_maps(q_ref[...])
    bp = bp_ref[0]
    bn = bn_ref[0]
    sp = lax.dot_general(qz, kp_ref[0].astype(BF16), _NT, preferred_element_type=F32)
    sp = sp + jnp.concatenate([bp, bp], axis=0)
    sn = lax.dot_general(qz, kn_ref[...], _NT, preferred_element_type=F32)
    sn = sn + jnp.concatenate([bn, bn], axis=0)
    m = jnp.maximum(jnp.max(sp, axis=1, keepdims=True), jnp.max(sn, axis=1, keepdims=True))
    pp = jnp.exp2(sp - m)
    pn = jnp.exp2(sn - m)
    acc = (jnp.dot(pp.astype(BF16), _with_ones(vp_ref[0].astype(BF16)), preferred_element_type=F32)
           + jnp.dot(pn.astype(BF16), _with_ones(vn_ref[...]), preferred_element_type=F32))
    lam = _lambda_value(lam_ref[...], lam_init)
    o_ref[...] = _attn_finish(acc, lam, g_ref[...], lam_init, lq)


def _attn_sample(q, k_new, v_new, k_past, v_past, bias_past, bias_new, lam_p, g_att, batch, lq, lam_init):
    past = k_past.shape[1]
    new_spec = pl.BlockSpec((lq, DV_A), lambda b, h: (b, h))
    past_spec = pl.BlockSpec((1, past, DV_A), lambda b, h: (b, 0, h))
    return pl.pallas_call(
        functools.partial(_attn_sample_kernel, lq=lq, lam_init=lam_init),
        grid=(batch, H_A),
        in_specs=[_const_spec((4, D_A)),
                  pl.BlockSpec((1, DV_A), lambda b, h: (0, h)),
                  new_spec, past_spec, past_spec, new_spec, new_spec,
                  pl.BlockSpec((1, lq, past), lambda b, h: (h, 0, 0)),
                  pl.BlockSpec((1, lq, lq), lambda b, h: (h, 0, 0))],
        out_specs=new_spec,
        out_shape=jax.ShapeDtypeStruct((batch * lq, ATT_WIDTH), BF16),
        compiler_params=pltpu.CompilerParams(dimension_semantics=("parallel", "parallel"),
                                             vmem_limit_bytes=V7X_VMEM_LIMIT),
        name="attn_sample",
    )(lam_p, g_att, q, k_past, v_past, k_new, v_new, bias_past, bias_new)


def _recurrent_kernel(gb_ref, q_ref, k_ref, v_ref, og_ref, gt_ref, b_ref, c_ref, hc_ref, cw_ref, gml_ref,
                      s0_ref, m0_ref, cv0_ref,
                      hm_ref, oc_ref, s_ref, m_ref, cv_ref, *, L):
    @pl.when(pl.program_id(1) == 0)
    def _():
        s_ref[...] = s0_ref[...]
        m_ref[...] = m0_ref[...]
        cv_ref[...] = cv0_ref[...]

    q = q_ref[...]
    k = k_ref[...]
    v = v_ref[...]
    state = s_ref[0]
    q_state = jnp.dot(q, state.astype(BF16), preferred_element_type=F32)
    q_c = q_state[:, :ML_WIDTH]

    lane_head = lax.broadcasted_iota(jnp.int32, (L, ML_WIDTH), 1) // D_M
    row = lax.broadcasted_iota(jnp.int32, (L, L), 0)
    col = lax.broadcasted_iota(jnp.int32, (L, L), 1)
    causal = col <= row
    eye = col == row
    st_lane = lax.broadcasted_iota(jnp.int32, (1, ML_STATE_W), 1)
    st_lane_head = jnp.where(st_lane < ML_WIDTH, st_lane // D_M, st_lane - ML_WIDTH)
    m_lane = lax.broadcasted_iota(jnp.int32, (1, 128), 1)

    gt = gt_ref[0]
    m_all = m_ref[0]
    num = jnp.zeros((L, ML_WIDTH), F32)
    w_state = jnp.zeros((L, ML_WIDTH), F32)
    decay = jnp.zeros((1, ML_STATE_W), F32)
    m_next = jnp.zeros((1, 128), F32)
    for h in range(H_M):
        ig = gt[h:h + 1, :] + gb_ref[0, h]
        fz = gt[H_M + h:H_M + h + 1, :] + gb_ref[1, h]
        lf = jnp.minimum(fz, 0.0) - jnp.log1p(jnp.exp(-jnp.abs(fz)))
        f_col = jnp.sum(jnp.where(causal, lf, 0.0), axis=1, keepdims=True)
        f_row = jnp.sum(jnp.where(eye, f_col, 0.0), axis=0, keepdims=True)
        d = jnp.where(causal, f_col + (ig - f_row), -jnp.inf)
        m_prev = m_all[:, h:h + 1]
        inter = f_col + m_prev
        mt = jnp.maximum(inter, jnp.max(d, axis=1, keepdims=True))
        w_intra = jnp.exp(d - mt)
        w_inter = jnp.exp(inter - mt)
        head = lane_head == h
        qk = lax.dot_general(jnp.where(head, q, jnp.zeros_like(q)), k, _NT, preferred_element_type=F32)
        s = qk * w_intra
        den = jnp.sum(s, axis=1, keepdims=True) + w_inter * q_state[:, ML_WIDTH + h:ML_WIDTH + h + 1]
        inv = 1.0 / jnp.maximum(jnp.abs(den), jnp.exp(-mt))
        sv = jnp.dot(s.astype(BF16), v, preferred_element_type=F32)
        num = jnp.where(head, (sv + w_inter * q_c) * inv, num)
        m_new = mt[L - 1:L, :]
        w_col = jnp.sum(jnp.where(eye, w_intra[L - 1:L, :], 0.0), axis=1, keepdims=True)
        w_state = jnp.where(head, w_col, w_state)
        dec = jnp.exp(f_col[L - 1:L, :] + m_prev - m_new)
        decay = jnp.where(st_lane_head == h, dec, decay)
        m_next = jnp.where(m_lane == h, m_new, m_next)

    kw = (k.astype(F32) * w_state).astype(BF16)
    v_aug = jnp.concatenate([v, jnp.ones((L, ML_STATE_W - ML_WIDTH), BF16)], axis=1)
    upd = lax.dot_general(kw, v_aug, _TN, preferred_element_type=F32)
    s_row_head = lax.broadcasted_iota(jnp.int32, (ML_WIDTH, ML_STATE_W), 0) // D_M
    s_col = lax.broadcasted_iota(jnp.int32, (ML_WIDTH, ML_STATE_W), 1)
    s_col_head = jnp.where(s_col < ML_WIDTH, s_col // D_M, s_col - ML_WIDTH)
    s_ref[0] = decay * state + jnp.where(s_row_head == s_col_head, upd, 0.0)
    m_ref[0] = m_next

    hg = num * jax.nn.sigmoid(og_ref[...])
    scale = jnp.zeros((L, ML_WIDTH), F32)
    for h in range(H_M):
        head = lane_head == h
        ss = jnp.sum(jnp.where(head, hg * hg, 0.0), axis=1, keepdims=True)
        scale = jnp.where(head, lax.rsqrt(ss * (1.0 / D_M) + EPS), scale)
    hm_ref[...] = (hg * scale * gml_ref[...]).astype(BF16)

    u = c_ref[...] * hc_ref[...]
    prev = cv_ref[0]
    ri = lax.broadcasted_iota(jnp.int32, u.shape, 0)
    u1 = jnp.where(ri == 0, prev[1:2], pltpu.roll(u, 1, 0))
    u2 = jnp.where(ri == 0, prev[0:1], jnp.where(ri == 1, prev[1:2], pltpu.roll(u, 2, 0)))
    w = cw_ref[...]
    oc_ref[...] = (b_ref[...] * (w[0:1] * u2 + w[1:2] * u1 + w[2:3] * u)).astype(BF16)
    cv_ref[0, 0:1, :] = u[L - 2:L - 1]
    cv_ref[0, 1:2, :] = u[L - 1:L]


def _recurrent(gate_b, qkvm, og, gt, bch, conv_w, g_ml, s0, m0, cv0, batch, seq, L):
    nc = seq // L
    t = batch * seq
    tok = lambda j: pl.BlockSpec((L, ML_WIDTH), lambda b, c: (b * nc + c, j))
    per_b = lambda shape: pl.BlockSpec((1,) + shape, lambda b, c: (b, 0, 0))
    return pl.pallas_call(
        functools.partial(_recurrent_kernel, L=L),
        grid=(batch, nc),
        in_specs=[pl.BlockSpec(memory_space=pltpu.SMEM),
                  tok(0), tok(1), tok(2), tok(0),
                  pl.BlockSpec((1, 8, L), lambda b, c: (b * nc + c, 0, 0)),
                  tok(0), tok(1), tok(2),
                  _const_spec((CONV_K, CONV_CH)), _const_spec((1, ML_WIDTH)),
                  per_b((ML_WIDTH, ML_STATE_W)), per_b((1, 128)), per_b((CONV_K - 1, CONV_CH))],
        out_specs=[tok(0), tok(0),
                   per_b((ML_WIDTH, ML_STATE_W)), per_b((1, 128)), per_b((CONV_K - 1, CONV_CH))],
        out_shape=[jax.ShapeDtypeStruct((t, ML_WIDTH), BF16),
                   jax.ShapeDtypeStruct((t, CONV_CH), BF16),
                   jax.ShapeDtypeStruct((batch, ML_WIDTH, ML_STATE_W), F32),
                   jax.ShapeDtypeStruct((batch, 1, 128), F32),
                   jax.ShapeDtypeStruct((batch, CONV_K - 1, CONV_CH), F32)],
        compiler_params=pltpu.CompilerParams(dimension_semantics=("parallel", "arbitrary"),
                                             vmem_limit_bytes=V7X_VMEM_LIMIT),
        name="recurrent",
    )(gate_b, qkvm, qkvm, qkvm, og, gt, bch, bch, bch, conv_w, g_ml, s0, m0, cv0)


def _mem_kv_kernel(x_ref, w_ref, kf_ref, vf_ref, kb_ref, vb_ref):
    x = x_ref[...].astype(BF16)
    n = H_X * D_X
    k = jnp.dot(x, w_ref[:, :n], preferred_element_type=F32)
    v = jnp.dot(x, w_ref[:, n:], preferred_element_type=F32)
    kf_ref[...] = k
    vf_ref[...] = v
    kb_ref[...] = k.astype(BF16)
    vb_ref[...] = v.astype(BF16)


def _mem_kv(mem, w_kv, tm):
    t = mem.shape[0]
    n = H_X * D_X
    row = pl.BlockSpec((tm, n), lambda i: (i, 0))
    return pl.pallas_call(
        _mem_kv_kernel,
        grid=(t // tm,),
        in_specs=[pl.BlockSpec((tm, D_MODEL), lambda i: (i, 0)), _const_spec((D_MODEL, 2 * n))],
        out_specs=[row, row, row, row],
        out_shape=[jax.ShapeDtypeStruct((t, n), F32), jax.ShapeDtypeStruct((t, n), F32),
                   jax.ShapeDtypeStruct((t, n), BF16), jax.ShapeDtypeStruct((t, n), BF16)],
        compiler_params=pltpu.CompilerParams(dimension_semantics=("parallel",),
                                             vmem_limit_bytes=V7X_VMEM_LIMIT),
        name="mem_kv",
    )(mem, w_kv)


FF_CHUNK = 256


def _post_kernel(x_ref, oa_ref, hm_ref, oc_ref, mk_ref, mv_ref, wout_ref, gc_ref, wcq_ref, wco_ref,
                 gf_ref, wg_ref, wu_ref, wd_ref, gfin_ref, o_ref, *, final):
    a, m = ATT_WIDTH, ML_WIDTH
    x = x_ref[...]
    x = x + (jnp.dot(oa_ref[...], wout_ref[0:a, :], preferred_element_type=F32)
             + jnp.dot(hm_ref[...], wout_ref[a:a + m, :], preferred_element_type=F32)
             + jnp.dot(oc_ref[...], wout_ref[a + m:, :], preferred_element_type=F32))

    xn = _rms(x, gc_ref[...]).astype(BF16)
    qc = (jnp.dot(xn, wcq_ref[...], preferred_element_type=F32) * (D_X ** -0.5)).astype(BF16)
    cross = jnp.zeros_like(x)
    for h in range(H_X):
        sl = slice(h * D_X, (h + 1) * D_X)
        s = lax.dot_general(qc[:, sl], mk_ref[0, :, sl], _NT, preferred_element_type=F32)
        p = jnp.exp(s - jnp.max(s, axis=1, keepdims=True))
        p = p * (1.0 / jnp.sum(p, axis=1, keepdims=True))
        o = jnp.dot(p.astype(BF16), mv_ref[0, :, sl], preferred_element_type=F32)
        cross = cross + jnp.dot(o.astype(BF16), wco_ref[sl, :], preferred_element_type=F32)
    x = x + cross

    xn = _rms(x, gf_ref[...]).astype(BF16)
    ff = jnp.zeros_like(x)
    for j in range(D_FF // FF_CHUNK):
        sl = slice(j * FF_CHUNK, (j + 1) * FF_CHUNK)
        g = jnp.dot(xn, wg_ref[:, sl], preferred_element_type=F32)
        u = jnp.dot(xn, wu_ref[:, sl], preferred_element_type=F32)
        act = (g * jax.nn.sigmoid(g) * u).astype(BF16)
        ff = ff + jnp.dot(act, wd_ref[sl, :], preferred_element_type=F32)
    x = x + ff
    if final:
        x = _rms(x, gfin_ref[...])
    o_ref[...] = x


def _post(x, oa, hm, oc, mk, mv, w_out, g_cross, w_cq, w_co, g_ffn, w_gate, w_up, w_down, g_final,
          tm, tiles_per_stream, final):
    t = x.shape[0]
    row = lambda n: pl.BlockSpec((tm, n), lambda i: (i, 0))
    mem = pl.BlockSpec((1,) + mk.shape[1:], lambda i: (i // tiles_per_stream, 0, 0))
    vec = _const_spec((1, D_MODEL))
    return pl.pallas_call(
        functools.partial(_post_kernel, final=final),
        grid=(t // tm,),
        in_specs=[row(D_MODEL), row(ATT_WIDTH), row(ML_WIDTH), row(CONV_CH), mem, mem,
                  _const_spec(w_out.shape), vec, _const_spec(w_cq.shape), _const_spec(w_co.shape),
                  vec, _const_spec(w_gate.shape), _const_spec(w_up.shape), _const_spec(w_down.shape), vec],
        out_specs=row(D_MODEL),
        out_shape=jax.ShapeDtypeStruct((t, D_MODEL), F32),
        compiler_params=pltpu.CompilerParams(dimension_semantics=("parallel",),
                                             vmem_limit_bytes=V7X_VMEM_LIMIT),
        name="post",
    )(x, oa, hm, oc, mk, mv, w_out, g_cross, w_cq, w_co, g_ffn, w_gate, w_up, w_down, g_final)


def _rel_bucket(rel):
    half = NUM_BUCKETS // 2
    max_exact = half // 2
    n = jnp.abs(rel)
    large = max_exact + (jnp.log(jnp.maximum(n, 1).astype(F32) / max_exact)
                         / math.log(MAX_DISTANCE / max_exact) * (half - max_exact)).astype(jnp.int32)
    large = jnp.minimum(large, half - 1)
    return jnp.where(rel > 0, half, 0) + jnp.where(n < max_exact, n, large)


def _bias_table(rel_bias, rel):
    bucket = _rel_bucket(rel)[None]
    table = rel_bias.astype(F32)
    out = jnp.zeros((table.shape[1],) + rel.shape, F32)
    for b in range(NUM_BUCKETS):
        out = jnp.where(bucket == b, table[b][:, None, None], out)
    return out


def _prompt_bias_tiles(rel_bias, t):
    key = jnp.arange(2 * t, dtype=jnp.int32)[:, None] - t
    qry = (jnp.arange(2 * t, dtype=jnp.int32) % t)[None, :]
    far = rel_bias[_rel_bucket(jnp.int32(-MAX_DISTANCE))].astype(F32)[:, None, None]
    visible = (key // CHUNK) <= (qry // CHUNK)
    return jnp.where(visible, (_bias_table(rel_bias, key - qry) - far) * LOG2E, -jnp.inf)


def _block_diag_state(c, n):
    b = c.shape[0]
    eye = jnp.eye(H_M, dtype=F32)
    cbd = jnp.einsum('bhdv,hg->bhdgv', c.astype(F32), eye).reshape(b, ML_WIDTH, ML_WIDTH)
    ncol = jnp.einsum('bhd,hg->bhdg', n.astype(F32), eye).reshape(b, ML_WIDTH, H_M)
    pad = jnp.zeros((b, ML_WIDTH, ML_STATE_W - ML_WIDTH - H_M), F32)
    return jnp.concatenate([cbd, ncol, pad], axis=2)


def _unpack_state(s, m):
    b = s.shape[0]
    blocks = s[:, :, :ML_WIDTH].reshape(b, H_M, D_M, H_M, D_M)
    c = jnp.stack([blocks[:, h, :, h, :] for h in range(H_M)], axis=1)
    ncols = s[:, :, ML_WIDTH:ML_WIDTH + H_M].reshape(b, H_M, D_M, H_M)
    n = jnp.stack([ncols[:, h, :, h] for h in range(H_M)], axis=1)
    return c, n, m[:, 0, :H_M]


def _layer(x, layer, wts, attn_fn, mem_k, mem_v, ml_state, conv_prev, g_final, final,
           batch, seq, tm_proj, L, tm_post):
    (g_mix, w_proj, w_gt, gate_b, conv_w, lam_p, g_att, g_ml, w_out,
     g_cross, w_cq, w_co, g_ffn, w_gate, w_up, w_down) = wts
    t = batch * seq
    lam_init = 0.8 - 0.6 * math.exp(-0.3 * layer)
    qa, kf, kb, vf, vb, qkvm, og, bch, gt = _in_proj(x, g_mix, w_proj, w_gt, tm_proj)
    oa = attn_fn(qa, kb, vb, lam_p, g_att, lam_init)
    gt = jnp.transpose(gt.reshape(8, t // L, L), (1, 0, 2))
    hm, oc, s1, m1, cv1 = _recurrent(gate_b, qkvm, og, gt, bch, conv_w, g_ml,
                                     ml_state[0], ml_state[1], conv_prev, batch, seq, L)
    x = _post(x, oa, hm, oc, mem_k, mem_v, w_out, g_cross, w_cq, w_co, g_ffn, w_gate, w_up, w_down,
              g_final, tm_post, seq // tm_post, final)
    c1, n1, mm1 = _unpack_state(s1, m1)
    return x, kf, vf, c1, n1, mm1, cv1


def kernel(x_prompt, x_sample, mem_prompt, cache_att_k, cache_att_v, cache_mem_k, cache_mem_v,
           state_mlstm_C, state_mlstm_n, state_mlstm_m, state_conv,
           norm_mix, w_in, mlstm_gate_bias, conv_w, lambda_params, norm_att_heads, norm_mlstm_heads,
           w_out, norm_cross, w_cq, w_ck, w_cv, w_co, norm_ffn, w_gate, w_up, w_down,
           rel_bias, norm_final):
    bp, sp, _ = x_prompt.shape
    bs, ss, _ = x_sample.shape
    depth = w_in.shape[0]
    past = cache_att_k.shape[2]
    n_mem = mem_prompt.shape[1]
    t_attn = 256
    l_prompt = 128

    xp = x_prompt.reshape(bp * sp, D_MODEL)
    xs = x_sample.reshape(bs * ss, D_MODEL)
    mem = mem_prompt.reshape(bp * n_mem, D_MODEL)
    g_final = norm_final.reshape(1, D_MODEL)

    bias_tiles = _prompt_bias_tiles(rel_bias, t_attn)
    rel_s = jnp.arange(past + ss, dtype=jnp.int32)[None, :] - (past + jnp.arange(ss, dtype=jnp.int32))[:, None]
    bias_s = _bias_table(rel_bias, rel_s) * LOG2E
    bias_s_past, bias_s_new = bias_s[:, :, :past], bias_s[:, :, past:]

    zero_state = (jnp.zeros((bp, ML_WIDTH, ML_STATE_W), F32), jnp.zeros((bp, 1, 128), F32))
    zero_conv = jnp.zeros((bp, CONV_K - 1, CONV_CH), F32)

    outs = {k: [] for k in ('pk', 'pv', 'pmk', 'pmv', 'pC', 'pn', 'pm', 'pcv', 'sk', 'sv', 'sC', 'sn', 'sm', 'scv')}
    gate_lo = 3 * ATT_WIDTH + 4 * ML_WIDTH
    for l in range(depth):
        w = w_in[l]
        wts = (norm_mix[l].reshape(1, D_MODEL),
               jnp.concatenate([w[:, :gate_lo], w[:, gate_lo + 2 * H_M:]], axis=1).astype(BF16),
               w[:, gate_lo:gate_lo + 2 * H_M].T.astype(BF16),
               mlstm_gate_bias[l].astype(F32), conv_w[l].astype(F32), lambda_params[l].astype(F32),
               norm_att_heads[l].reshape(1, ATT_WIDTH), norm_mlstm_heads[l].reshape(1, ML_WIDTH),
               w_out[l].astype(BF16), norm_cross[l].reshape(1, D_MODEL),
               w_cq[l].astype(BF16), w_co[l].astype(BF16), norm_ffn[l].reshape(1, D_MODEL),
               w_gate[l].astype(BF16), w_up[l].astype(BF16), w_down[l].astype(BF16))
        final = l == depth - 1

        w_kv = jnp.concatenate([w_ck[l], w_cv[l]], axis=1).astype(BF16)
        mkf, mvf, mkb, mvb = _mem_kv(mem, w_kv, 512)
        attn_p = lambda q, k, v, lam_p, g, li: _attn_prompt(q, k, v, bias_tiles, lam_p, g, bp, sp, t_attn, li)
        xp, kf, vf, c1, n1, m1, cv1 = _layer(
            xp, l, wts, attn_p, mkb.reshape(bp, n_mem, -1), mvb.reshape(bp, n_mem, -1),
            zero_state, zero_conv, g_final, final, bp, sp, 512, l_prompt, 512)
        outs['pk'].append(kf.reshape(bp, sp, H_A, 2 * D_A))
        outs['pv'].append(vf.reshape(bp, sp, H_A, DV_A))
        outs['pmk'].append(mkf.reshape(bp, n_mem, H_X, D_X))
        outs['pmv'].append(mvf.reshape(bp, n_mem, H_X, D_X))
        outs['pC'].append(c1); outs['pn'].append(n1); outs['pm'].append(m1); outs['pcv'].append(cv1)

        k_past = cache_att_k[l].reshape(bs, past, ATT_WIDTH)
        v_past = cache_att_v[l].reshape(bs, past, ATT_WIDTH)
        attn_s = lambda q, k, v, lam_p, g, li: _attn_sample(q, k, v, k_past, v_past, bias_s_past, bias_s_new,
                                                            lam_p, g, bs, ss, li)
        s0 = _block_diag_state(state_mlstm_C[l], state_mlstm_n[l])
        m0 = jnp.pad(state_mlstm_m[l].astype(F32), ((0, 0), (0, 128 - H_M))).reshape(bs, 1, 128)
        xs, kf, vf, c1, n1, m1, cv1 = _layer(
            xs, l, wts, attn_s,
            cache_mem_k[l].reshape(bs, n_mem, -1).astype(BF16), cache_mem_v[l].reshape(bs, n_mem, -1).astype(BF16),
            (s0, m0), state_conv[l].astype(F32), g_final, final, bs, ss, bs * ss, ss, ss)
        outs['sk'].append(kf.reshape(bs, ss, H_A, 2 * D_A))
        outs['sv'].append(vf.reshape(bs, ss, H_A, DV_A))
        outs['sC'].append(c1); outs['sn'].append(n1); outs['sm'].append(m1); outs['scv'].append(cv1)

    st = lambda k: jnp.stack(outs[k])
    return (xp.reshape(bp, sp, D_MODEL), xs.reshape(bs, ss, D_MODEL),
            st('pk'), st('pv'), st('pmk'), st('pmv'), st('pC'), st('pn'), st('pm'), st('pcv'),
            st('sk'), st('sv'), st('sC'), st('sn'), st('sm'), st('scv'))
```

```python
import functools
import math

import jax
import jax.numpy as jnp
from jax import lax
from jax.experimental import pallas as pl
from jax.experimental.pallas import tpu as pltpu

F32 = jnp.float32
BF16 = jnp.bfloat16

D_MODEL = 1024
CHUNK = 64
H_A = 4
D_A = 64
DV_A = 128
ATT_WIDTH = H_A * DV_A
H_M = 4
D_M = 64
ML_WIDTH = H_M * D_M
CONV_CH = 256
CONV_K = 3
H_X = 4
D_X = 256
D_FF = 2816
NUM_BUCKETS = 32
MAX_DISTANCE = 128
EPS = 1e-6

N_PROJ = 3 * ATT_WIDTH + 4 * ML_WIDTH + 3 * CONV_CH
ML_STATE_W = ML_WIDTH + 128

V7X_VMEM_LIMIT = 56 * 1024 * 1024

LOG2E = math.log2(math.e)
RECURRENT_STREAMS = 1
ATTN_HEADS_PER_STEP = 4

_NT = (((1,), (1,)), ((), ()))
_TN = (((0,), (0,)), ((), ()))


def _const_spec(shape):
    nd = len(shape)
    return pl.BlockSpec(shape, lambda *_: (0,) * nd, pipeline_mode=pl.Buffered(1))


def _rms(x, g):
    ms = jnp.mean(x * x, axis=-1, keepdims=True)
    return x * lax.rsqrt(ms + EPS) * g


def _in_proj_kernel(x_ref, g_ref, w_ref, wgt_ref,
                    qa_ref, kf_ref, kb_ref, vf_ref, vb_ref, qkvm_ref, om_ref, bch_ref, gt_ref):
    xn = _rms(x_ref[...], g_ref[...]).astype(BF16)

    def proj(lo, hi):
        return jnp.dot(xn, w_ref[:, lo:hi], preferred_element_type=F32)

    a = ATT_WIDTH
    qa_ref[...] = (proj(0, a) * (D_A ** -0.5 * LOG2E)).astype(BF16)
    k = proj(a, 2 * a)
    kf_ref[...] = k
    kb_ref[...] = k.astype(BF16)
    v = proj(2 * a, 3 * a)
    vf_ref[...] = v
    vb_ref[...] = v.astype(BF16)
    o = 3 * a
    m = ML_WIDTH
    qkvm_ref[:, 0:m] = proj(o, o + m).astype(BF16)
    qkvm_ref[:, m:2 * m] = (proj(o + m, o + 2 * m) * (D_M ** -0.5)).astype(BF16)
    qkvm_ref[:, 2 * m:3 * m] = proj(o + 2 * m, o + 3 * m).astype(BF16)
    om_ref[...] = proj(o + 3 * m, o + 4 * m)
    bch_ref[...] = proj(o + 4 * m, N_PROJ)
    gt_ref[...] = lax.dot_general(wgt_ref[...], xn, _NT, preferred_element_type=F32)


def _in_proj(x, g, w, wgt, tm):
    t = x.shape[0]
    a, m = ATT_WIDTH, ML_WIDTH
    row = lambda n: pl.BlockSpec((tm, n), lambda i: (i, 0))
    outs = [
        (jax.ShapeDtypeStruct((t, a), BF16), row(a)),
        (jax.ShapeDtypeStruct((t, a), F32), row(a)),
        (jax.ShapeDtypeStruct((t, a), BF16), row(a)),
        (jax.ShapeDtypeStruct((t, a), F32), row(a)),
        (jax.ShapeDtypeStruct((t, a), BF16), row(a)),
        (jax.ShapeDtypeStruct((t, 3 * m), BF16), row(3 * m)),
        (jax.ShapeDtypeStruct((t, m), F32), row(m)),
        (jax.ShapeDtypeStruct((t, 3 * CONV_CH), F32), row(3 * CONV_CH)),
        (jax.ShapeDtypeStruct((8, t), F32), pl.BlockSpec((8, tm), lambda i: (0, i))),
    ]
    return pl.pallas_call(
        _in_proj_kernel,
        grid=(t // tm,),
        in_specs=[row(D_MODEL), _const_spec((1, D_MODEL)), _const_spec((D_MODEL, N_PROJ)),
                  _const_spec((8, D_MODEL))],
        out_specs=[s for _, s in outs],
        out_shape=[s for s, _ in outs],
        compiler_params=pltpu.CompilerParams(dimension_semantics=("parallel",),
                                             vmem_limit_bytes=V7X_VMEM_LIMIT),
        name="in_proj",
    )(x, g, w, wgt)


def _lambda_value(lp, lam_init):
    a = jnp.sum(lp[0:1] * lp[1:2], axis=1, keepdims=True)
    b = jnp.sum(lp[2:3] * lp[3:4], axis=1, keepdims=True)
    return jnp.exp(a) - jnp.exp(b) + lam_init


def _stack_maps(q):
    lane = lax.broadcasted_iota(jnp.int32, q.shape, 1)
    zero = jnp.zeros_like(q)
    return jnp.concatenate([jnp.where(lane < D_A, q, zero), jnp.where(lane >= D_A, q, zero)], axis=0)


def _attn_finish(acc, lam, g, lam_init, t):
    o = acc[:, :DV_A] / acc[:, DV_A:]
    d = o[:t] - lam * o[t:]
    return (_rms(d, g) * (1.0 - lam_init)).astype(BF16)


def _with_ones(v):
    return jnp.concatenate([v, jnp.ones_like(v)], axis=1)


def _attn_prompt_kernel(lam_ref, g_ref, q_ref, k_ref, v_ref, bias_ref, o_ref,
                        m_scr, l_scr, acc_scr, sa_scr, sb_scr, p_scr, *, t, hp, lam_init):
    i = pl.program_id(2)
    heads = [slice(hh * DV_A, (hh + 1) * DV_A) for hh in range(hp)]
    cols = [slice(hh * 2 * t, (hh + 1) * 2 * t) for hh in range(hp)]
    qz = [_stack_maps(q_ref[:, hd]) for hd in heads]

    def rows(kb):
        return pl.ds(pl.multiple_of(kb * t, t), t)

    def qk(kb):
        r = rows(kb)
        return jnp.concatenate([lax.dot_general(k_ref[r, hd], z, _NT, preferred_element_type=F32)
                                for hd, z in zip(heads, qz)], axis=1)

    def pv(p, kb):
        r = rows(kb)
        return jnp.concatenate([lax.dot_general(v_ref[r, hd], p[:, c], _TN, preferred_element_type=F32)
                                for hd, c in zip(heads, cols)], axis=1)

    def bias(lo):
        return jnp.concatenate([bias_ref[hh, lo:lo + t, :] for hh in range(hp)], axis=1)

    def softmax_step(m_prev, s):
        m_new = jnp.maximum(m_prev, jnp.max(s, axis=0, keepdims=True))
        p = jnp.exp2(s - m_new)
        return m_new, jnp.exp2(m_prev - m_new), p, jnp.sum(p, axis=0, keepdims=True)

    n_far = jnp.maximum(i - 1, 0)
    last = n_far - 1
    kb_prev = jnp.maximum(i - 1, 0)

    s_diag = qk(i)
    s_prev = qk(kb_prev)
    sa_scr[...] = qk(0)
    s_diag = s_diag + bias(t)
    m = jnp.max(s_diag, axis=0, keepdims=True)
    p = jnp.exp2(s_diag - m)
    l = jnp.sum(p, axis=0, keepdims=True)
    acc = pv(p.astype(BF16), i)
    no_prev = jnp.where(i >= 1, 0.0, -jnp.inf)
    m, alpha, p, p_sum = softmax_step(m, s_prev + (bias(0) + no_prev))
    m_scr[...] = m
    l_scr[...] = alpha * l + p_sum
    acc_scr[...] = alpha * acc
    p_scr[...] = p.astype(BF16)

    def far_pair(j, carry):
        kb1 = jnp.minimum(2 * j + 1, last)
        w1 = (2 * j + 1 <= last).astype(F32)
        kb_pending = jnp.where(j == 0, kb_prev, 2 * j - 1)
        sb_scr[...] = qk(kb1)
        owed = pv(p_scr[...], kb_pending)
        m, alpha, p, p_sum = softmax_step(m_scr[...], sa_scr[...])
        l = alpha * l_scr[...] + p_sum
        acc = alpha * (acc_scr[...] + owed)
        sa_scr[...] = qk(jnp.minimum(2 * j + 2, last))
        owed = pv(p.astype(BF16), 2 * j)
        m, alpha, p, p_sum = softmax_step(m, sb_scr[...])
        m_scr[...] = m
        l_scr[...] = alpha * l + w1 * p_sum
        acc_scr[...] = alpha * (acc + owed)
        p_scr[...] = p.astype(BF16)
        return carry

    trips = (n_far + 1) // 2
    lax.fori_loop(0, trips, far_pair, 0)

    kb_pending = jnp.where(trips == 0, kb_prev, jnp.minimum(2 * trips - 1, last))
    w_pending = jnp.where((trips == 0) | (n_far % 2 == 0), 1.0, 0.0)
    acc = acc_scr[...] + w_pending * pv(p_scr[...], kb_pending)

    lam = _lambda_value(lam_ref[...], lam_init)
    o = acc * (1.0 / l_scr[...])
    for hd, c in zip(heads, cols):
        d = o[:, c][:, :t] - lam * o[:, c][:, t:]
        ms = jnp.mean(d * d, axis=0, keepdims=True)
        y = jnp.transpose(d * lax.rsqrt(ms + EPS))
        o_ref[:, hd] = (y * g_ref[:, hd] * (1.0 - lam_init)).astype(BF16)


def _attn_prompt(q, k, v, bias, lam_p, g_att, batch, seq, t, lam_init):
    nq = seq // t
    hp = ATTN_HEADS_PER_STEP
    w = hp * 2 * t
    kv_spec = pl.BlockSpec((seq, hp * DV_A), lambda b, h, i: (b, h))
    return pl.pallas_call(
        functools.partial(_attn_prompt_kernel, t=t, hp=hp, lam_init=lam_init),
        grid=(batch, H_A // hp, nq),
        in_specs=[_const_spec((4, D_A)),
                  pl.BlockSpec((1, hp * DV_A), lambda b, h, i: (0, h)),
                  pl.BlockSpec((t, hp * DV_A), lambda b, h, i: (b * nq + i, h)),
                  kv_spec, kv_spec,
                  pl.BlockSpec((hp, 2 * t, 2 * t), lambda b, h, i: (h, 0, 0))],
        out_specs=pl.BlockSpec((t, hp * DV_A), lambda b, h, i: (b * nq + i, h)),
        out_shape=jax.ShapeDtypeStruct((batch * seq, ATT_WIDTH), BF16),
        scratch_shapes=[pltpu.VMEM((1, w), F32), pltpu.VMEM((1, w), F32),
                        pltpu.VMEM((DV_A, w), F32),
                        pltpu.VMEM((t, w), F32), pltpu.VMEM((t, w), F32),
                        pltpu.VMEM((t, w), BF16)],
        compiler_params=pltpu.CompilerParams(dimension_semantics=("parallel", "parallel", "arbitrary"),
                                             vmem_limit_bytes=V7X_VMEM_LIMIT),
        name="attn_prompt",
    )(lam_p, g_att, q, k, v, bias)


def _attn_sample_kernel(lam_ref, g_ref, q_ref, kp_ref, vp_ref, kn_ref, vn_ref, bp_ref, bn_ref, o_ref,
                        *, lq, lam_init):
    qz = _stack_maps(q_ref[...])
    bp = bp_ref[0]
    bn = bn_ref[0]
    sp = lax.dot_general(qz, kp_ref[0].astype(BF16), _NT, preferred_element_type=F32)
    sp = sp + jnp.concatenate([bp, bp], axis=0)
    sn = lax.dot_general(qz, kn_ref[...], _NT, preferred_element_type=F32)
    sn = sn + jnp.concatenate([bn, bn], axis=0)
    m = jnp.maximum(jnp.max(sp, axis=1, keepdims=True), jnp.max(sn, axis=1, keepdims=True))
    pp = jnp.exp2(sp - m)
    pn = jnp.exp2(sn - m)
    acc = (jnp.dot(pp.astype(BF16), _with_ones(vp_ref[0].astype(BF16)), preferred_element_type=F32)
           + jnp.dot(pn.astype(BF16), _with_ones(vn_ref[...]), preferred_element_type=F32))
    lam = _lambda_value(lam_ref[...], lam_init)
    o_ref[...] = _attn_finish(acc, lam, g_ref[...], lam_init, lq)


def _attn_sample(q, k_new, v_new, k_past, v_past, bias_past, bias_new, lam_p, g_att, batch, lq, lam_init):
    past = k_past.shape[1]
    new_spec = pl.BlockSpec((lq, DV_A), lambda b, h: (b, h))
    past_spec = pl.BlockSpec((1, past, DV_A), lambda b, h: (b, 0, h))
    return pl.pallas_call(
        functools.partial(_attn_sample_kernel, lq=lq, lam_init=lam_init),
        grid=(batch, H_A),
        in_specs=[_const_spec((4, D_A)),
                  pl.BlockSpec((1, DV_A), lambda b, h: (0, h)),
                  new_spec, past_spec, past_spec, new_spec, new_spec,
                  pl.BlockSpec((1, lq, past), lambda b, h: (h, 0, 0)),
                  pl.BlockSpec((1, lq, lq), lambda b, h: (h, 0, 0))],
        out_specs=new_spec,
        out_shape=jax.ShapeDtypeStruct((batch * lq, ATT_WIDTH), BF16),
        compiler_params=pltpu.CompilerParams(dimension_semantics=("parallel", "parallel"),
                                             vmem_limit_bytes=V7X_VMEM_LIMIT),
        name="attn_sample",
    )(lam_p, g_att, q, k_past, v_past, k_new, v_new, bias_past, bias_new)


def _recurrent_kernel(gb_ref, q_ref, k_ref, v_ref, og_ref, gt_ref, b_ref, c_ref, hc_ref, cw_ref, gml_ref,
                      s0_ref, m0_ref, cv0_ref,
                      hm_ref, oc_ref, s_ref, m_ref, cv_ref, *, L, G):
    @pl.when(pl.program_id(1) == 0)
    def _():
        s_ref[...] = s0_ref[...]
        m_ref[...] = m0_ref[...]
        cv_ref[...] = cv0_ref[...]

    for g in range(G):
        _recurrent_stream(g, gb_ref, q_ref, k_ref, v_ref, og_ref, gt_ref, b_ref, c_ref, hc_ref, cw_ref,
                          gml_ref, hm_ref, oc_ref, s_ref, m_ref, cv_ref, L)


def _recurrent_stream(g, gb_ref, q_ref, k_ref, v_ref, og_ref, gt_ref, b_ref, c_ref, hc_ref, cw_ref, gml_ref,
                      hm_ref, oc_ref, s_ref, m_ref, cv_ref, L):
    q = q_ref[g]
    k = k_ref[g]
    v = v_ref[g]
    state = s_ref[g]
    q_state = jnp.dot(q, state.astype(BF16), preferred_element_type=F32)
    q_c = q_state[:, :ML_WIDTH]

    lane_head = lax.broadcasted_iota(jnp.int32, (L, ML_WIDTH), 1) // D_M
    row = lax.broadcasted_iota(jnp.int32, (L, L), 0)
    col = lax.broadcasted_iota(jnp.int32, (L, L), 1)
    causal = col <= row
    eye = col == row
    st_lane = lax.broadcasted_iota(jnp.int32, (1, ML_STATE_W), 1)
    st_lane_head = jnp.where(st_lane < ML_WIDTH, st_lane // D_M, st_lane - ML_WIDTH)
    m_lane = lax.broadcasted_iota(jnp.int32, (1, 128), 1)

    gt = gt_ref[g, 0]
    m_all = m_ref[g]
    num = jnp.zeros((L, ML_WIDTH), F32)
    w_state = jnp.zeros((L, ML_WIDTH), F32)
    decay = jnp.zeros((1, ML_STATE_W), F32)
    m_next = jnp.zeros((1, 128), F32)
    for h in range(H_M):
        ig = gt[h:h + 1, :] + gb_ref[0, h]
        fz = gt[H_M + h:H_M + h + 1, :] + gb_ref[1, h]
        lf = jnp.minimum(fz, 0.0) - jnp.log1p(jnp.exp(-jnp.abs(fz)))
        f_col = jnp.sum(jnp.where(causal, lf, 0.0), axis=1, keepdims=True)
        f_row = jnp.sum(jnp.where(eye, f_col, 0.0), axis=0, keepdims=True)
        d = jnp.where(causal, f_col + (ig - f_row), -jnp.inf)
        m_prev = m_all[:, h:h + 1]
        inter = f_col + m_prev
        mt = jnp.maximum(inter, jnp.max(d, axis=1, keepdims=True))
        w_intra = jnp.exp(d - mt)
        w_inter = jnp.exp(inter - mt)
        head = lane_head == h
        qk = lax.dot_general(jnp.where(head, q, jnp.zeros_like(q)), k, _NT, preferred_element_type=F32)
        s = qk * w_intra
        den = jnp.sum(s, axis=1, keepdims=True) + w_inter * q_state[:, ML_WIDTH + h:ML_WIDTH + h + 1]
        inv = 1.0 / jnp.maximum(jnp.abs(den), jnp.exp(-mt))
        sv = jnp.dot(s.astype(BF16), v, preferred_element_type=F32)
        num = jnp.where(head, (sv + w_inter * q_c) * inv, num)
        m_new = mt[L - 1:L, :]
        w_col = jnp.sum(jnp.where(eye, w_intra[L - 1:L, :], 0.0), axis=1, keepdims=True)
        w_state = jnp.where(head, w_col, w_state)
        dec = jnp.exp(f_col[L - 1:L, :] + m_prev - m_new)
        decay = jnp.where(st_lane_head == h, dec, decay)
        m_next = jnp.where(m_lane == h, m_new, m_next)

    kw = (k.astype(F32) * w_state).astype(BF16)
    v_aug = jnp.concatenate([v, jnp.ones((L, ML_STATE_W - ML_WIDTH), BF16)], axis=1)
    upd = lax.dot_general(kw, v_aug, _TN, preferred_element_type=F32)
    s_row_head = lax.broadcasted_iota(jnp.int32, (ML_WIDTH, ML_STATE_W), 0) // D_M
    s_col = lax.broadcasted_iota(jnp.int32, (ML_WIDTH, ML_STATE_W), 1)
    s_col_head = jnp.where(s_col < ML_WIDTH, s_col // D_M, s_col - ML_WIDTH)
    s_ref[g] = decay * state + jnp.where(s_row_head == s_col_head, upd, 0.0)
    m_ref[g] = m_next

    hg = num * jax.nn.sigmoid(og_ref[g])
    scale = jnp.zeros((L, ML_WIDTH), F32)
    for h in range(H_M):
        head = lane_head == h
        ss = jnp.sum(jnp.where(head, hg * hg, 0.0), axis=1, keepdims=True)
        scale = jnp.where(head, lax.rsqrt(ss * (1.0 / D_M) + EPS), scale)
    hm_ref[g] = (hg * scale * gml_ref[...]).astype(BF16)

    u = c_ref[g] * hc_ref[g]
    prev = cv_ref[g]
    ri = lax.broadcasted_iota(jnp.int32, u.shape, 0)
    u1 = jnp.where(ri == 0, prev[1:2], pltpu.roll(u, 1, 0))
    u2 = jnp.where(ri == 0, prev[0:1], jnp.where(ri == 1, prev[1:2], pltpu.roll(u, 2, 0)))
    w = cw_ref[...]
    oc_ref[g] = (b_ref[g] * (w[0:1] * u2 + w[1:2] * u1 + w[2:3] * u)).astype(BF16)
    cv_ref[g, 0:1, :] = u[L - 2:L - 1]
    cv_ref[g, 1:2, :] = u[L - 1:L]


def _recurrent(gate_b, qkvm, og, gt, bch, conv_w, g_ml, s0, m0, cv0, batch, seq, L, G):
    nc = seq // L
    tok = lambda j: pl.BlockSpec((G, L, ML_WIDTH), lambda b, c: (b, c, j))
    per_b = lambda shape: pl.BlockSpec((G,) + shape, lambda b, c: (b, 0, 0))
    return pl.pallas_call(
        functools.partial(_recurrent_kernel, L=L, G=G),
        grid=(batch // G, nc),
        in_specs=[pl.BlockSpec(memory_space=pltpu.SMEM),
                  tok(0), tok(1), tok(2), tok(0),
                  pl.BlockSpec((G, 1, 8, L), lambda b, c: (b, c, 0, 0)),
                  tok(0), tok(1), tok(2),
                  _const_spec((CONV_K, CONV_CH)), _const_spec((1, ML_WIDTH)),
                  per_b((ML_WIDTH, ML_STATE_W)), per_b((1, 128)), per_b((CONV_K - 1, CONV_CH))],
        out_specs=[tok(0), tok(0),
                   per_b((ML_WIDTH, ML_STATE_W)), per_b((1, 128)), per_b((CONV_K - 1, CONV_CH))],
        out_shape=[jax.ShapeDtypeStruct((batch, seq, ML_WIDTH), BF16),
                   jax.ShapeDtypeStruct((batch, seq, CONV_CH), BF16),
                   jax.ShapeDtypeStruct((batch, ML_WIDTH, ML_STATE_W), F32),
                   jax.ShapeDtypeStruct((batch, 1, 128), F32),
                   jax.ShapeDtypeStruct((batch, CONV_K - 1, CONV_CH), F32)],
        compiler_params=pltpu.CompilerParams(dimension_semantics=("parallel", "arbitrary"),
                                             vmem_limit_bytes=V7X_VMEM_LIMIT),
        name="recurrent",
    )(gate_b, qkvm, qkvm, qkvm, og, gt, bch, bch, bch, conv_w, g_ml, s0, m0, cv0)


def _mem_kv_kernel(x_ref, w_ref, kf_ref, vf_ref, kb_ref, vb_ref):
    x = x_ref[...].astype(BF16)
    n = H_X * D_X
    k = jnp.dot(x, w_ref[:, :n], preferred_element_type=F32)
    v = jnp.dot(x, w_ref[:, n:], preferred_element_type=F32)
    kf_ref[...] = k
    vf_ref[...] = v
    kb_ref[...] = k.astype(BF16)
    vb_ref[...] = v.astype(BF16)


def _mem_kv(mem, w_kv, tm):
    t = mem.shape[0]
    n = H_X * D_X
    row = pl.BlockSpec((tm, n), lambda i: (i, 0))
    return pl.pallas_call(
        _mem_kv_kernel,
        grid=(t // tm,),
        in_specs=[pl.BlockSpec((tm, D_MODEL), lambda i: (i, 0)), _const_spec((D_MODEL, 2 * n))],
        out_specs=[row, row, row, row],
        out_shape=[jax.ShapeDtypeStruct((t, n), F32), jax.ShapeDtypeStruct((t, n), F32),
                   jax.ShapeDtypeStruct((t, n), BF16), jax.ShapeDtypeStruct((t, n), BF16)],
        compiler_params=pltpu.CompilerParams(dimension_semantics=("parallel",),
                                             vmem_limit_bytes=V7X_VMEM_LIMIT),
        name="mem_kv",
    )(mem, w_kv)


FF_CHUNK = 256


def _post_kernel(x_ref, oa_ref, hm_ref, oc_ref, mk_ref, mv_ref, wout_ref, gc_ref, wcq_ref, wco_ref,
                 gf_ref, wg_ref, wu_ref, wd_ref, gfin_ref, o_ref, *, final):
    a, m = ATT_WIDTH, ML_WIDTH
    x = x_ref[...]
    x = x + (jnp.dot(oa_ref[...], wout_ref[0:a, :], preferred_element_type=F32)
             + jnp.dot(hm_ref[...], wout_ref[a:a + m, :], preferred_element_type=F32)
             + jnp.dot(oc_ref[...], wout_ref[a + m:, :], preferred_element_type=F32))

    xn = _rms(x, gc_ref[...]).astype(BF16)
    qc = (jnp.dot(xn, wcq_ref[...], preferred_element_type=F32) * (D_X ** -0.5)).astype(BF16)
    cross = jnp.zeros_like(x)
    for h in range(H_X):
        sl = slice(h * D_X, (h + 1) * D_X)
        s = lax.dot_general(qc[:, sl], mk_ref[0, :, sl], _NT, preferred_element_type=F32)
        p = jnp.exp(s - jnp.max(s, axis=1, keepdims=True))
        p = p * (1.0 / jnp.sum(p, axis=1, keepdims=True))
        o = jnp.dot(p.astype(BF16), mv_ref[0, :, sl], preferred_element_type=F32)
        cross = cross + jnp.dot(o.astype(BF16), wco_ref[sl, :], preferred_element_type=F32)
    x = x + cross

    xn = _rms(x, gf_ref[...]).astype(BF16)
    ff = jnp.zeros_like(x)
    for j in range(D_FF // FF_CHUNK):
        sl = slice(j * FF_CHUNK, (j + 1) * FF_CHUNK)
        g = jnp.dot(xn, wg_ref[:, sl], preferred_element_type=F32)
        u = jnp.dot(xn, wu_ref[:, sl], preferred_element_type=F32)
        act = (g * jax.nn.sigmoid(g) * u).astype(BF16)
        ff = ff + jnp.dot(act, wd_ref[sl, :], preferred_element_type=F32)
    x = x + ff
    if final:
        x = _rms(x, gfin_ref[...])
    o_ref[...] = x


def _post(x, oa, hm, oc, mk, mv, w_out, g_cross, w_cq, w_co, g_ffn, w_gate, w_up, w_down, g_final,
          tm, tiles_per_stream, final):
    t = x.shape[0]
    row = lambda n: pl.BlockSpec((tm, n), lambda i: (i, 0))
    mem = pl.BlockSpec((1,) + mk.shape[1:], lambda i: (i // tiles_per_stream, 0, 0))
    vec = _const_spec((1, D_MODEL))
    return pl.pallas_call(
        functools.partial(_post_kernel, final=final),
        grid=(t // tm,),
        in_specs=[row(D_MODEL), row(ATT_WIDTH), row(ML_WIDTH), row(CONV_CH), mem, mem,
                  _const_spec(w_out.shape), vec, _const_spec(w_cq.shape), _const_spec(w_co.shape),
                  vec, _const_spec(w_gate.shape), _const_spec(w_up.shape), _const_spec(w_down.shape), vec],
        out_specs=row(D_MODEL),
        out_shape=jax.ShapeDtypeStruct((t, D_MODEL), F32),
        compiler_params=pltpu.CompilerParams(dimension_semantics=("parallel",),
                                             vmem_limit_bytes=V7X_VMEM_LIMIT),
        name="post",
    )(x, oa, hm, oc, mk, mv, w_out, g_cross, w_cq, w_co, g_ffn, w_gate, w_up, w_down, g_final)


def _rel_bucket(rel):
    half = NUM_BUCKETS // 2
    max_exact = half // 2
    n = jnp.abs(rel)
    large = max_exact + (jnp.log(jnp.maximum(n, 1).astype(F32) / max_exact)
                         / math.log(MAX_DISTANCE / max_exact) * (half - max_exact)).astype(jnp.int32)
    large = jnp.minimum(large, half - 1)
    return jnp.where(rel > 0, half, 0) + jnp.where(n < max_exact, n, large)


def _bias_table(rel_bias, rel):
    bucket = _rel_bucket(rel)[None]
    table = rel_bias.astype(F32)
    out = jnp.zeros((table.shape[1],) + rel.shape, F32)
    for b in range(NUM_BUCKETS):
        out = jnp.where(bucket == b, table[b][:, None, None], out)
    return out


def _prompt_bias_tiles(rel_bias, t):
    key = jnp.arange(2 * t, dtype=jnp.int32)[:, None] - t
    qry = (jnp.arange(2 * t, dtype=jnp.int32) % t)[None, :]
    far = rel_bias[_rel_bucket(jnp.int32(-MAX_DISTANCE))].astype(F32)[:, None, None]
    visible = (key // CHUNK) <= (qry // CHUNK)
    return jnp.where(visible, (_bias_table(rel_bias, key - qry) - far) * LOG2E, -jnp.inf)


def _block_diag_state(c, n):
    b = c.shape[0]
    eye = jnp.eye(H_M, dtype=F32)
    cbd = jnp.einsum('bhdv,hg->bhdgv', c.astype(F32), eye).reshape(b, ML_WIDTH, ML_WIDTH)
    ncol = jnp.einsum('bhd,hg->bhdg', n.astype(F32), eye).reshape(b, ML_WIDTH, H_M)
    pad = jnp.zeros((b, ML_WIDTH, ML_STATE_W - ML_WIDTH - H_M), F32)
    return jnp.concatenate([cbd, ncol, pad], axis=2)


def _unpack_state(s, m):
    b = s.shape[0]
    blocks = s[:, :, :ML_WIDTH].reshape(b, H_M, D_M, H_M, D_M)
    c = jnp.stack([blocks[:, h, :, h, :] for h in range(H_M)], axis=1)
    ncols = s[:, :, ML_WIDTH:ML_WIDTH + H_M].reshape(b, H_M, D_M, H_M)
    n = jnp.stack([ncols[:, h, :, h] for h in range(H_M)], axis=1)
    return c, n, m[:, 0, :H_M]


def _layer(x, layer, wts, attn_fn, mem_k, mem_v, ml_state, conv_prev, g_final, final,
           batch, seq, tm_proj, L, tm_post):
    (g_mix, w_proj, w_gt, gate_b, conv_w, lam_p, g_att, g_ml, w_out,
     g_cross, w_cq, w_co, g_ffn, w_gate, w_up, w_down) = wts
    t = batch * seq
    lam_init = 0.8 - 0.6 * math.exp(-0.3 * layer)
    qa, kf, kb, vf, vb, qkvm, og, bch, gt = _in_proj(x, g_mix, w_proj, w_gt, tm_proj)
    oa = attn_fn(qa, kb, vb, lam_p, g_att, lam_init)
    gt = jnp.transpose(gt.reshape(8, batch, seq // L, L), (1, 2, 0, 3))
    tok3 = lambda a: a.reshape(batch, seq, a.shape[-1])
    hm, oc, s1, m1, cv1 = _recurrent(gate_b, tok3(qkvm), tok3(og), gt, tok3(bch), conv_w, g_ml,
                                     ml_state[0], ml_state[1], conv_prev, batch, seq, L, RECURRENT_STREAMS)
    hm, oc = hm.reshape(t, ML_WIDTH), oc.reshape(t, CONV_CH)
    x = _post(x, oa, hm, oc, mem_k, mem_v, w_out, g_cross, w_cq, w_co, g_ffn, w_gate, w_up, w_down,
              g_final, tm_post, seq // tm_post, final)
    c1, n1, mm1 = _unpack_state(s1, m1)
    return x, kf, vf, c1, n1, mm1, cv1


def kernel(x_prompt, x_sample, mem_prompt, cache_att_k, cache_att_v, cache_mem_k, cache_mem_v,
           state_mlstm_C, state_mlstm_n, state_mlstm_m, state_conv,
           norm_mix, w_in, mlstm_gate_bias, conv_w, lambda_params, norm_att_heads, norm_mlstm_heads,
           w_out, norm_cross, w_cq, w_ck, w_cv, w_co, norm_ffn, w_gate, w_up, w_down,
           rel_bias, norm_final):
    bp, sp, _ = x_prompt.shape
    bs, ss, _ = x_sample.shape
    depth = w_in.shape[0]
    past = cache_att_k.shape[2]
    n_mem = mem_prompt.shape[1]
    t_attn = 256
    l_prompt = 128

    xp = x_prompt.reshape(bp * sp, D_MODEL)
    xs = x_sample.reshape(bs * ss, D_MODEL)
    mem = mem_prompt.reshape(bp * n_mem, D_MODEL)
    g_final = norm_final.reshape(1, D_MODEL)

    bias_tiles = _prompt_bias_tiles(rel_bias, t_attn)
    rel_s = jnp.arange(past + ss, dtype=jnp.int32)[None, :] - (past + jnp.arange(ss, dtype=jnp.int32))[:, None]
    bias_s = _bias_table(rel_bias, rel_s) * LOG2E
    bias_s_past, bias_s_new = bias_s[:, :, :past], bias_s[:, :, past:]

    zero_state = (jnp.zeros((bp, ML_WIDTH, ML_STATE_W), F32), jnp.zeros((bp, 1, 128), F32))
    zero_conv = jnp.zeros((bp, CONV_K - 1, CONV_CH), F32)

    outs = {k: [] for k in ('pk', 'pv', 'pmk', 'pmv', 'pC', 'pn', 'pm', 'pcv', 'sk', 'sv', 'sC', 'sn', 'sm', 'scv')}
    gate_lo = 3 * ATT_WIDTH + 4 * ML_WIDTH
    for l in range(depth):
        w = w_in[l]
        wts = (norm_mix[l].reshape(1, D_MODEL),
               jnp.concatenate([w[:, :gate_lo], w[:, gate_lo + 2 * H_M:]], axis=1).astype(BF16),
               w[:, gate_lo:gate_lo + 2 * H_M].T.astype(BF16),
               mlstm_gate_bias[l].astype(F32), conv_w[l].astype(F32), lambda_params[l].astype(F32),
               norm_att_heads[l].reshape(1, ATT_WIDTH), norm_mlstm_heads[l].reshape(1, ML_WIDTH),
               w_out[l].astype(BF16), norm_cross[l].reshape(1, D_MODEL),
               w_cq[l].astype(BF16), w_co[l].astype(BF16), norm_ffn[l].reshape(1, D_MODEL),
               w_gate[l].astype(BF16), w_up[l].astype(BF16), w_down[l].astype(BF16))
        final = l == depth - 1

        w_kv = jnp.concatenate([w_ck[l], w_cv[l]], axis=1).astype(BF16)
        mkf, mvf, mkb, mvb = _mem_kv(mem, w_kv, 512)
        attn_p = lambda q, k, v, lam_p, g, li: _attn_prompt(q, k, v, bias_tiles, lam_p, g, bp, sp, t_attn, li)
        xp, kf, vf, c1, n1, m1, cv1 = _layer(
            xp, l, wts, attn_p, mkb.reshape(bp, n_mem, -1), mvb.reshape(bp, n_mem, -1),
            zero_state, zero_conv, g_final, final, bp, sp, 512, l_prompt, 512)
        outs['pk'].append(kf.reshape(bp, sp, H_A, 2 * D_A))
        outs['pv'].append(vf.reshape(bp, sp, H_A, DV_A))
        outs['pmk'].append(mkf.reshape(bp, n_mem, H_X, D_X))
        outs['pmv'].append(mvf.reshape(bp, n_mem, H_X, D_X))
        outs['pC'].append(c1); outs['pn'].append(n1); outs['pm'].append(m1); outs['pcv'].append(cv1)

        k_past = cache_att_k[l].reshape(bs, past, ATT_WIDTH)
        v_past = cache_att_v[l].reshape(bs, past, ATT_WIDTH)
        attn_s = lambda q, k, v, lam_p, g, li: _attn_sample(q, k, v, k_past, v_past, bias_s_past, bias_s_new,
                                                            lam_p, g, bs, ss, li)
        s0 = _block_diag_state(state_mlstm_C[l], state_mlstm_n[l])
        m0 = jnp.pad(state_mlstm_m[l].astype(F32), ((0, 0), (0, 128 - H_M))).reshape(bs, 1, 128)
        xs, kf, vf, c1, n1, m1, cv1 = _layer(
            xs, l, wts, attn_s,
            cache_mem_k[l].reshape(bs, n_mem, -1).astype(BF16), cache_mem_v[l].reshape(bs, n_mem, -1).astype(BF16),
            (s0, m0), state_conv[l].astype(F32), g_final, final, bs, ss, bs * ss, ss, ss)
        outs['sk'].append(kf.reshape(bs, ss, H_A, 2 * D_A))
        outs['sv'].append(vf.reshape(bs, ss, H_A, DV_A))
        outs['sC'].append(c1); outs['sn'].append(n1); outs['sm'].append(m1); outs['scv'].append(cv1)

    st = lambda k: jnp.stack(outs[k])
    return (xp.reshape(bp, sp, D_MODEL), xs.reshape(bs, ss, D_MODEL),
            st('pk'), st('pv'), st('pmk'), st('pmv'), st('pC'), st('pn'), st('pm'), st('pcv'),
            st('sk'), st('sv'), st('sC'), st('sn'), st('sm'), st('scv'))
```

```python
import functools
import math

import jax
import jax.numpy as jnp
from jax import lax
from jax.experimental import pallas as pl
from jax.experimental.pallas import tpu as pltpu

F32 = jnp.float32
BF16 = jnp.bfloat16

D_MODEL = 1024
CHUNK = 64
H_A = 4
D_A = 64
DV_A = 128
ATT_WIDTH = H_A * DV_A
H_M = 4
D_M = 64
ML_WIDTH = H_M * D_M
CONV_CH = 256
CONV_K = 3
H_X = 4
D_X = 256
D_FF = 2816
NUM_BUCKETS = 32
MAX_DISTANCE = 128
EPS = 1e-6

N_PROJ = 3 * ATT_WIDTH + 4 * ML_WIDTH + 3 * CONV_CH
ML_STATE_W = ML_WIDTH + 128

V7X_VMEM_LIMIT = 56 * 1024 * 1024

LOG2E = math.log2(math.e)
RECURRENT_STREAMS = 8
ATTN_HEADS_PER_STEP = 4

_NT = (((1,), (1,)), ((), ()))
_TN = (((0,), (0,)), ((), ()))


def _const_spec(shape):
    nd = len(shape)
    return pl.BlockSpec(shape, lambda *_: (0,) * nd, pipeline_mode=pl.Buffered(1))


def _rms(x, g):
    ms = jnp.mean(x * x, axis=-1, keepdims=True)
    return x * lax.rsqrt(ms + EPS) * g


def _in_proj_kernel(x_ref, g_ref, w_ref, wgt_ref,
                    qa_ref, kf_ref, kb_ref, vf_ref, vb_ref, qkvm_ref, om_ref, bch_ref, gt_ref):
    xn = _rms(x_ref[...], g_ref[...]).astype(BF16)

    def proj(lo, hi):
        return jnp.dot(xn, w_ref[:, lo:hi], preferred_element_type=F32)

    a = ATT_WIDTH
    qa_ref[...] = (proj(0, a) * (D_A ** -0.5 * LOG2E)).astype(BF16)
    k = proj(a, 2 * a)
    kf_ref[...] = k
    kb_ref[...] = k.astype(BF16)
    v = proj(2 * a, 3 * a)
    vf_ref[...] = v
    vb_ref[...] = v.astype(BF16)
    o = 3 * a
    m = ML_WIDTH
    qkvm_ref[:, 0:m] = proj(o, o + m).astype(BF16)
    qkvm_ref[:, m:2 * m] = (proj(o + m, o + 2 * m) * (D_M ** -0.5)).astype(BF16)
    qkvm_ref[:, 2 * m:3 * m] = proj(o + 2 * m, o + 3 * m).astype(BF16)
    om_ref[...] = proj(o + 3 * m, o + 4 * m)
    bch_ref[...] = proj(o + 4 * m, N_PROJ)
    gt_ref[...] = lax.dot_general(wgt_ref[...], xn, _NT, preferred_element_type=F32)


def _in_proj(x, g, w, wgt, tm):
    t = x.shape[0]
    a, m = ATT_WIDTH, ML_WIDTH
    row = lambda n: pl.BlockSpec((tm, n), lambda i: (i, 0))
    outs = [
        (jax.ShapeDtypeStruct((t, a), BF16), row(a)),
        (jax.ShapeDtypeStruct((t, a), F32), row(a)),
        (jax.ShapeDtypeStruct((t, a), BF16), row(a)),
        (jax.ShapeDtypeStruct((t, a), F32), row(a)),
        (jax.ShapeDtypeStruct((t, a), BF16), row(a)),
        (jax.ShapeDtypeStruct((t, 3 * m), BF16), row(3 * m)),
        (jax.ShapeDtypeStruct((t, m), F32), row(m)),
        (jax.ShapeDtypeStruct((t, 3 * CONV_CH), F32), row(3 * CONV_CH)),
        (jax.ShapeDtypeStruct((8, t), F32), pl.BlockSpec((8, tm), lambda i: (0, i))),
    ]
    return pl.pallas_call(
        _in_proj_kernel,
        grid=(t // tm,),
        in_specs=[row(D_MODEL), _const_spec((1, D_MODEL)), _const_spec((D_MODEL, N_PROJ)),
                  _const_spec((8, D_MODEL))],
        out_specs=[s for _, s in outs],
        out_shape=[s for s, _ in outs],
        compiler_params=pltpu.CompilerParams(dimension_semantics=("parallel",),
                                             vmem_limit_bytes=V7X_VMEM_LIMIT),
        name="in_proj",
    )(x, g, w, wgt)


def _lambda_value(lp, lam_init):
    a = jnp.sum(lp[0:1] * lp[1:2], axis=1, keepdims=True)
    b = jnp.sum(lp[2:3] * lp[3:4], axis=1, keepdims=True)
    return jnp.exp(a) - jnp.exp(b) + lam_init


def _stack_maps(q):
    lane = lax.broadcasted_iota(jnp.int32, q.shape, 1)
    zero = jnp.zeros_like(q)
    return jnp.concatenate([jnp.where(lane < D_A, q, zero), jnp.where(lane >= D_A, q, zero)], axis=0)


def _attn_finish(acc, lam, g, lam_init, t):
    o = acc[:, :DV_A] / acc[:, DV_A:]
    d = o[:t] - lam * o[t:]
    return (_rms(d, g) * (1.0 - lam_init)).astype(BF16)


def _with_ones(v):
    return jnp.concatenate([v, jnp.ones_like(v)], axis=1)


def _attn_prompt_kernel(lam_ref, g_ref, q_ref, k_ref, v_ref, bias_ref, o_ref,
                        m_scr, l_scr, acc_scr, sa_scr, sb_scr, p_scr, *, t, hp, lam_init):
    i = pl.program_id(2)
    heads = [slice(hh * DV_A, (hh + 1) * DV_A) for hh in range(hp)]
    cols = [slice(hh * 2 * t, (hh + 1) * 2 * t) for hh in range(hp)]
    qz = [_stack_maps(q_ref[:, hd]) for hd in heads]

    def rows(kb):
        return pl.ds(pl.multiple_of(kb * t, t), t)

    def qk(kb):
        r = rows(kb)
        return jnp.concatenate([lax.dot_general(k_ref[r, hd], z, _NT, preferred_element_type=F32)
                                for hd, z in zip(heads, qz)], axis=1)

    def pv(p, kb):
        r = rows(kb)
        return jnp.concatenate([lax.dot_general(v_ref[r, hd], p[:, c], _TN, preferred_element_type=F32)
                                for hd, c in zip(heads, cols)], axis=1)

    def bias(lo):
        return jnp.concatenate([bias_ref[hh, lo:lo + t, :] for hh in range(hp)], axis=1)

    def softmax_step(m_prev, s):
        m_new = jnp.maximum(m_prev, jnp.max(s, axis=0, keepdims=True))
        p = jnp.exp2(s - m_new)
        return m_new, jnp.exp2(m_prev - m_new), p, jnp.sum(p, axis=0, keepdims=True)

    n_far = jnp.maximum(i - 1, 0)
    last = n_far - 1
    kb_prev = jnp.maximum(i - 1, 0)

    s_diag = qk(i)
    s_prev = qk(kb_prev)
    sa_scr[...] = qk(0)
    s_diag = s_diag + bias(t)
    m = jnp.max(s_diag, axis=0, keepdims=True)
    p = jnp.exp2(s_diag - m)
    l = jnp.sum(p, axis=0, keepdims=True)
    acc = pv(p.astype(BF16), i)
    no_prev = jnp.where(i >= 1, 0.0, -jnp.inf)
    m, alpha, p, p_sum = softmax_step(m, s_prev + (bias(0) + no_prev))
    m_scr[...] = m
    l_scr[...] = alpha * l + p_sum
    acc_scr[...] = alpha * acc
    p_scr[...] = p.astype(BF16)

    def far_pair(j, carry):
        kb1 = jnp.minimum(2 * j + 1, last)
        w1 = (2 * j + 1 <= last).astype(F32)
        kb_pending = jnp.where(j == 0, kb_prev, 2 * j - 1)
        sb_scr[...] = qk(kb1)
        owed = pv(p_scr[...], kb_pending)
        m, alpha, p, p_sum = softmax_step(m_scr[...], sa_scr[...])
        l = alpha * l_scr[...] + p_sum
        acc = alpha * (acc_scr[...] + owed)
        sa_scr[...] = qk(jnp.minimum(2 * j + 2, last))
        owed = pv(p.astype(BF16), 2 * j)
        m, alpha, p, p_sum = softmax_step(m, sb_scr[...])
        m_scr[...] = m
        l_scr[...] = alpha * l + w1 * p_sum
        acc_scr[...] = alpha * (acc + owed)
        p_scr[...] = p.astype(BF16)
        return carry

    trips = (n_far + 1) // 2
    lax.fori_loop(0, trips, far_pair, 0)

    kb_pending = jnp.where(trips == 0, kb_prev, jnp.minimum(2 * trips - 1, last))
    w_pending = jnp.where((trips == 0) | (n_far % 2 == 0), 1.0, 0.0)
    acc = acc_scr[...] + w_pending * pv(p_scr[...], kb_pending)

    lam = _lambda_value(lam_ref[...], lam_init)
    o = acc * (1.0 / l_scr[...])
    for hd, c in zip(heads, cols):
        d = o[:, c][:, :t] - lam * o[:, c][:, t:]
        ms = jnp.mean(d * d, axis=0, keepdims=True)
        y = jnp.transpose(d * lax.rsqrt(ms + EPS))
        o_ref[:, hd] = (y * g_ref[:, hd] * (1.0 - lam_init)).astype(BF16)


def _attn_prompt(q, k, v, bias, lam_p, g_att, batch, seq, t, lam_init):
    nq = seq // t
    hp = ATTN_HEADS_PER_STEP
    w = hp * 2 * t
    kv_spec = pl.BlockSpec((seq, hp * DV_A), lambda b, h, i: (b, h))
    return pl.pallas_call(
        functools.partial(_attn_prompt_kernel, t=t, hp=hp, lam_init=lam_init),
        grid=(batch, H_A // hp, nq),
        in_specs=[_const_spec((4, D_A)),
                  pl.BlockSpec((1, hp * DV_A), lambda b, h, i: (0, h)),
                  pl.BlockSpec((t, hp * DV_A), lambda b, h, i: (b * nq + i, h)),
                  kv_spec, kv_spec,
                  pl.BlockSpec((hp, 2 * t, 2 * t), lambda b, h, i: (h, 0, 0))],
        out_specs=pl.BlockSpec((t, hp * DV_A), lambda b, h, i: (b * nq + i, h)),
        out_shape=jax.ShapeDtypeStruct((batch * seq, ATT_WIDTH), BF16),
        scratch_shapes=[pltpu.VMEM((1, w), F32), pltpu.VMEM((1, w), F32),
                        pltpu.VMEM((DV_A, w), F32),
                        pltpu.VMEM((t, w), F32), pltpu.VMEM((t, w), F32),
                        pltpu.VMEM((t, w), BF16)],
        compiler_params=pltpu.CompilerParams(dimension_semantics=("parallel", "parallel", "arbitrary"),
                                             vmem_limit_bytes=V7X_VMEM_LIMIT),
        name="attn_prompt",
    )(lam_p, g_att, q, k, v, bias)


def _attn_sample_kernel(lam_ref, g_ref, q_ref, kp_ref, vp_ref, kn_ref, vn_ref, bp_ref, bn_ref, o_ref,
                        *, lq, lam_init):
    qz = _stack_maps(q_ref[...])
    bp = bp_ref[0]
    bn = bn_ref[0]
    sp = lax.dot_general(qz, kp_ref[0].astype(BF16), _NT, preferred_element_type=F32)
    sp = sp + jnp.concatenate([bp, bp], axis=0)
    sn = lax.dot_general(qz, kn_ref[...], _NT, preferred_element_type=F32)
    sn = sn + jnp.concatenate([bn, bn], axis=0)
    m = jnp.maximum(jnp.max(sp, axis=1, keepdims=True), jnp.max(sn, axis=1, keepdims=True))
    pp = jnp.exp2(sp - m)
    pn = jnp.exp2(sn - m)
    acc = (jnp.dot(pp.astype(BF16), _with_ones(vp_ref[0].astype(BF16)), preferred_element_type=F32)
           + jnp.dot(pn.astype(BF16), _with_ones(vn_ref[...]), preferred_element_type=F32))
    lam = _lambda_value(lam_ref[...], lam_init)
    o_ref[...] = _attn_finish(acc, lam, g_ref[...], lam_init, lq)


def _attn_sample(q, k_new, v_new, k_past, v_past, bias_past, bias_new, lam_p, g_att, batch, lq, lam_init):
    past = k_past.shape[1]
    new_spec = pl.BlockSpec((lq, DV_A), lambda b, h: (b, h))
    past_spec = pl.BlockSpec((1, past, DV_A), lambda b, h: (b, 0, h))
    return pl.pallas_call(
        functools.partial(_attn_sample_kernel, lq=lq, lam_init=lam_init),
        grid=(batch, H_A),
        in_specs=[_const_spec((4, D_A)),
                  pl.BlockSpec((1, DV_A), lambda b, h: (0, h)),
                  new_spec, past_spec, past_spec, new_spec, new_spec,
                  pl.BlockSpec((1, lq, past), lambda b, h: (h, 0, 0)),
                  pl.BlockSpec((1, lq, lq), lambda b, h: (h, 0, 0))],
        out_specs=new_spec,
        out_shape=jax.ShapeDtypeStruct((batch * lq, ATT_WIDTH), BF16),
        compiler_params=pltpu.CompilerParams(dimension_semantics=("parallel", "parallel"),
                                             vmem_limit_bytes=V7X_VMEM_LIMIT),
        name="attn_sample",
    )(lam_p, g_att, q, k_past, v_past, k_new, v_new, bias_past, bias_new)


def _per_stream(fn, *arrays):
    return jnp.stack([fn(*(a[g] for a in arrays)) for g in range(arrays[0].shape[0])])


def _recurrent_kernel(gb_ref, q_ref, k_ref, v_ref, og_ref, gt_ref, b_ref, c_ref, hc_ref, cw_ref, gml_ref,
                      s0_ref, m0_ref, cv0_ref,
                      hm_ref, oc_ref, s_ref, m_ref, cv_ref, *, L):
    @pl.when(pl.program_id(1) == 0)
    def _():
        s_ref[...] = s0_ref[...]
        m_ref[...] = m0_ref[...]
        cv_ref[...] = cv0_ref[...]

    q = q_ref[...]
    k = k_ref[...]
    v = v_ref[...]
    G = q.shape[0]
    state = s_ref[...]
    q_state = _per_stream(lambda a, b: jnp.dot(a, b, preferred_element_type=F32),
                          q, state.astype(BF16))
    q_c = q_state[:, :, :ML_WIDTH]

    lane_head = lax.broadcasted_iota(jnp.int32, (1, L, ML_WIDTH), 2) // D_M
    row = lax.broadcasted_iota(jnp.int32, (1, L, L), 1)
    col = lax.broadcasted_iota(jnp.int32, (1, L, L), 2)
    causal = col <= row
    eye = col == row
    st_lane = lax.broadcasted_iota(jnp.int32, (1, 1, ML_STATE_W), 2)
    st_lane_head = jnp.where(st_lane < ML_WIDTH, st_lane // D_M, st_lane - ML_WIDTH)
    m_lane = lax.broadcasted_iota(jnp.int32, (1, 1, 128), 2)

    gt = gt_ref[:, 0]
    m_all = m_ref[...]
    num = jnp.zeros((G, L, ML_WIDTH), F32)
    w_state = jnp.zeros((G, L, ML_WIDTH), F32)
    decay = jnp.zeros((G, 1, ML_STATE_W), F32)
    m_next = jnp.zeros((G, 1, 128), F32)
    for h in range(H_M):
        ig = gt[:, h:h + 1, :] + gb_ref[0, h]
        fz = gt[:, H_M + h:H_M + h + 1, :] + gb_ref[1, h]
        lf = jnp.minimum(fz, 0.0) - jnp.log1p(jnp.exp(-jnp.abs(fz)))
        f_col = jnp.sum(jnp.where(causal, lf, 0.0), axis=2, keepdims=True)
        f_row = jnp.sum(jnp.where(eye, f_col, 0.0), axis=1, keepdims=True)
        d = jnp.where(causal, f_col + (ig - f_row), -jnp.inf)
        m_prev = m_all[:, :, h:h + 1]
        inter = f_col + m_prev
        mt = jnp.maximum(inter, jnp.max(d, axis=2, keepdims=True))
        w_intra = jnp.exp(d - mt)
        w_inter = jnp.exp(inter - mt)
        head = lane_head == h
        qk = _per_stream(lambda a, b: lax.dot_general(a, b, _NT, preferred_element_type=F32),
                         jnp.where(head, q, jnp.zeros_like(q)), k)
        s = qk * w_intra
        den = jnp.sum(s, axis=2, keepdims=True) + w_inter * q_state[:, :, ML_WIDTH + h:ML_WIDTH + h + 1]
        inv = 1.0 / jnp.maximum(jnp.abs(den), jnp.exp(-mt))
        sv = _per_stream(lambda a, b: jnp.dot(a, b, preferred_element_type=F32), s.astype(BF16), v)
        num = jnp.where(head, (sv + w_inter * q_c) * inv, num)
        m_new = mt[:, L - 1:L, :]
        w_col = jnp.sum(jnp.where(eye, w_intra[:, L - 1:L, :], 0.0), axis=2, keepdims=True)
        w_state = jnp.where(head, w_col, w_state)
        dec = jnp.exp(f_col[:, L - 1:L, :] + m_prev - m_new)
        decay = jnp.where(st_lane_head == h, dec, decay)
        m_next = jnp.where(m_lane == h, m_new, m_next)

    kw = (k.astype(F32) * w_state).astype(BF16)
    v_aug = jnp.concatenate([v, jnp.ones((G, L, ML_STATE_W - ML_WIDTH), BF16)], axis=2)
    upd = _per_stream(lambda a, b: lax.dot_general(a, b, _TN, preferred_element_type=F32), kw, v_aug)
    s_row_head = lax.broadcasted_iota(jnp.int32, (1, ML_WIDTH, ML_STATE_W), 1) // D_M
    s_col = lax.broadcasted_iota(jnp.int32, (1, ML_WIDTH, ML_STATE_W), 2)
    s_col_head = jnp.where(s_col < ML_WIDTH, s_col // D_M, s_col - ML_WIDTH)
    s_ref[...] = decay * state + jnp.where(s_row_head == s_col_head, upd, 0.0)
    m_ref[...] = m_next

    hg = num * jax.nn.sigmoid(og_ref[...])
    scale = jnp.zeros((G, L, ML_WIDTH), F32)
    for h in range(H_M):
        head = lane_head == h
        ss = jnp.sum(jnp.where(head, hg * hg, 0.0), axis=2, keepdims=True)
        scale = jnp.where(head, lax.rsqrt(ss * (1.0 / D_M) + EPS), scale)
    hm_ref[...] = (hg * scale * gml_ref[...]).astype(BF16)

    u = c_ref[...] * hc_ref[...]
    prev = cv_ref[...]
    ri = lax.broadcasted_iota(jnp.int32, (1, L, CONV_CH), 1)
    u1 = jnp.where(ri == 0, prev[:, 1:2], _per_stream(lambda a: pltpu.roll(a, 1, 0), u))
    u2 = jnp.where(ri == 0, prev[:, 0:1],
                   jnp.where(ri == 1, prev[:, 1:2], _per_stream(lambda a: pltpu.roll(a, 2, 0), u)))
    w = cw_ref[...]
    oc_ref[...] = (b_ref[...] * (w[0:1] * u2 + w[1:2] * u1 + w[2:3] * u)).astype(BF16)
    cv_ref[:, 0:1, :] = u[:, L - 2:L - 1]
    cv_ref[:, 1:2, :] = u[:, L - 1:L]


def _recurrent(gate_b, qkvm, og, gt, bch, conv_w, g_ml, s0, m0, cv0, batch, seq, L, G):
    assert batch % G == 0 and seq % L == 0
    nc = seq // L
    tok = lambda j: pl.BlockSpec((G, L, ML_WIDTH), lambda b, c: (b, c, j))
    per_b = lambda shape: pl.BlockSpec((G,) + shape, lambda b, c: (b, 0, 0))
    return pl.pallas_call(
        functools.partial(_recurrent_kernel, L=L),
        grid=(batch // G, nc),
        in_specs=[pl.BlockSpec(memory_space=pltpu.SMEM),
                  tok(0), tok(1), tok(2), tok(0),
                  pl.BlockSpec((G, 1, 8, L), lambda b, c: (b, c, 0, 0)),
                  tok(0), tok(1), tok(2),
                  _const_spec((CONV_K, CONV_CH)), _const_spec((1, ML_WIDTH)),
                  per_b((ML_WIDTH, ML_STATE_W)), per_b((1, 128)), per_b((CONV_K - 1, CONV_CH))],
        out_specs=[tok(0), tok(0),
                   per_b((ML_WIDTH, ML_STATE_W)), per_b((1, 128)), per_b((CONV_K - 1, CONV_CH))],
        out_shape=[jax.ShapeDtypeStruct((batch, seq, ML_WIDTH), BF16),
                   jax.ShapeDtypeStruct((batch, seq, CONV_CH), BF16),
                   jax.ShapeDtypeStruct((batch, ML_WIDTH, ML_STATE_W), F32),
                   jax.ShapeDtypeStruct((batch, 1, 128), F32),
                   jax.ShapeDtypeStruct((batch, CONV_K - 1, CONV_CH), F32)],
        compiler_params=pltpu.CompilerParams(dimension_semantics=("parallel", "arbitrary"),
                                             vmem_limit_bytes=V7X_VMEM_LIMIT),
        name="recurrent",
    )(gate_b, qkvm, qkvm, qkvm, og, gt, bch, bch, bch, conv_w, g_ml, s0, m0, cv0)


def _mem_kv_kernel(x_ref, w_ref, kf_ref, vf_ref, kb_ref, vb_ref):
    x = x_ref[...].astype(BF16)
    n = H_X * D_X
    k = jnp.dot(x, w_ref[:, :n], preferred_element_type=F32)
    v = jnp.dot(x, w_ref[:, n:], preferred_element_type=F32)
    kf_ref[...] = k
    vf_ref[...] = v
    kb_ref[...] = k.astype(BF16)
    vb_ref[...] = v.astype(BF16)


def _mem_kv(mem, w_kv, tm):
    t = mem.shape[0]
    n = H_X * D_X
    row = pl.BlockSpec((tm, n), lambda i: (i, 0))
    return pl.pallas_call(
        _mem_kv_kernel,
        grid=(t // tm,),
        in_specs=[pl.BlockSpec((tm, D_MODEL), lambda i: (i, 0)), _const_spec((D_MODEL, 2 * n))],
        out_specs=[row, row, row, row],
        out_shape=[jax.ShapeDtypeStruct((t, n), F32), jax.ShapeDtypeStruct((t, n), F32),
                   jax.ShapeDtypeStruct((t, n), BF16), jax.ShapeDtypeStruct((t, n), BF16)],
        compiler_params=pltpu.CompilerParams(dimension_semantics=("parallel",),
                                             vmem_limit_bytes=V7X_VMEM_LIMIT),
        name="mem_kv",
    )(mem, w_kv)


FF_CHUNK = 256


def _post_kernel(x_ref, oa_ref, hm_ref, oc_ref, mk_ref, mv_ref, wout_ref, gc_ref, wcq_ref, wco_ref,
                 gf_ref, wg_ref, wu_ref, wd_ref, gfin_ref, o_ref, *, final):
    a, m = ATT_WIDTH, ML_WIDTH
    x = x_ref[...]
    x = x + (jnp.dot(oa_ref[...], wout_ref[0:a, :], preferred_element_type=F32)
             + jnp.dot(hm_ref[...], wout_ref[a:a + m, :], preferred_element_type=F32)
             + jnp.dot(oc_ref[...], wout_ref[a + m:, :], preferred_element_type=F32))

    xn = _rms(x, gc_ref[...]).astype(BF16)
    qc = (jnp.dot(xn, wcq_ref[...], preferred_element_type=F32) * (D_X ** -0.5)).astype(BF16)
    cross = jnp.zeros_like(x)
    for h in range(H_X):
        sl = slice(h * D_X, (h + 1) * D_X)
        s = lax.dot_general(qc[:, sl], mk_ref[0, :, sl], _NT, preferred_element_type=F32)
        p = jnp.exp(s - jnp.max(s, axis=1, keepdims=True))
        p = p * (1.0 / jnp.sum(p, axis=1, keepdims=True))
        o = jnp.dot(p.astype(BF16), mv_ref[0, :, sl], preferred_element_type=F32)
        cross = cross + jnp.dot(o.astype(BF16), wco_ref[sl, :], preferred_element_type=F32)
    x = x + cross

    xn = _rms(x, gf_ref[...]).astype(BF16)
    ff = jnp.zeros_like(x)
    for j in range(D_FF // FF_CHUNK):
        sl = slice(j * FF_CHUNK, (j + 1) * FF_CHUNK)
        g = jnp.dot(xn, wg_ref[:, sl], preferred_element_type=F32)
        u = jnp.dot(xn, wu_ref[:, sl], preferred_element_type=F32)
        act = (g * jax.nn.sigmoid(g) * u).astype(BF16)
        ff = ff + jnp.dot(act, wd_ref[sl, :], preferred_element_type=F32)
    x = x + ff
    if final:
        x = _rms(x, gfin_ref[...])
    o_ref[...] = x


def _post(x, oa, hm, oc, mk, mv, w_out, g_cross, w_cq, w_co, g_ffn, w_gate, w_up, w_down, g_final,
          tm, tiles_per_stream, final):
    t = x.shape[0]
    row = lambda n: pl.BlockSpec((tm, n), lambda i: (i, 0))
    mem = pl.BlockSpec((1,) + mk.shape[1:], lambda i: (i // tiles_per_stream, 0, 0))
    vec = _const_spec((1, D_MODEL))
    return pl.pallas_call(
        functools.partial(_post_kernel, final=final),
        grid=(t // tm,),
        in_specs=[row(D_MODEL), row(ATT_WIDTH), row(ML_WIDTH), row(CONV_CH), mem, mem,
                  _const_spec(w_out.shape), vec, _const_spec(w_cq.shape), _const_spec(w_co.shape),
                  vec, _const_spec(w_gate.shape), _const_spec(w_up.shape), _const_spec(w_down.shape), vec],
        out_specs=row(D_MODEL),
        out_shape=jax.ShapeDtypeStruct((t, D_MODEL), F32),
        compiler_params=pltpu.CompilerParams(dimension_semantics=("parallel",),
                                             vmem_limit_bytes=V7X_VMEM_LIMIT),
        name="post",
    )(x, oa, hm, oc, mk, mv, w_out, g_cross, w_cq, w_co, g_ffn, w_gate, w_up, w_down, g_final)


def _rel_bucket(rel):
    half = NUM_BUCKETS // 2
    max_exact = half // 2
    n = jnp.abs(rel)
    large = max_exact + (jnp.log(jnp.maximum(n, 1).astype(F32) / max_exact)
                         / math.log(MAX_DISTANCE / max_exact) * (half - max_exact)).astype(jnp.int32)
    large = jnp.minimum(large, half - 1)
    return jnp.where(rel > 0, half, 0) + jnp.where(n < max_exact, n, large)


def _bias_table(rel_bias, rel):
    bucket = _rel_bucket(rel)[None]
    table = rel_bias.astype(F32)
    out = jnp.zeros((table.shape[1],) + rel.shape, F32)
    for b in range(NUM_BUCKETS):
        out = jnp.where(bucket == b, table[b][:, None, None], out)
    return out


def _prompt_bias_tiles(rel_bias, t):
    key = jnp.arange(2 * t, dtype=jnp.int32)[:, None] - t
    qry = (jnp.arange(2 * t, dtype=jnp.int32) % t)[None, :]
    far = rel_bias[_rel_bucket(jnp.int32(-MAX_DISTANCE))].astype(F32)[:, None, None]
    visible = (key // CHUNK) <= (qry // CHUNK)
    return jnp.where(visible, (_bias_table(rel_bias, key - qry) - far) * LOG2E, -jnp.inf)


def _block_diag_state(c, n):
    b = c.shape[0]
    eye = jnp.eye(H_M, dtype=F32)
    cbd = jnp.einsum('bhdv,hg->bhdgv', c.astype(F32), eye).reshape(b, ML_WIDTH, ML_WIDTH)
    ncol = jnp.einsum('bhd,hg->bhdg', n.astype(F32), eye).reshape(b, ML_WIDTH, H_M)
    pad = jnp.zeros((b, ML_WIDTH, ML_STATE_W - ML_WIDTH - H_M), F32)
    return jnp.concatenate([cbd, ncol, pad], axis=2)


def _unpack_state(s, m):
    b = s.shape[0]
    blocks = s[:, :, :ML_WIDTH].reshape(b, H_M, D_M, H_M, D_M)
    c = jnp.stack([blocks[:, h, :, h, :] for h in range(H_M)], axis=1)
    ncols = s[:, :, ML_WIDTH:ML_WIDTH + H_M].reshape(b, H_M, D_M, H_M)
    n = jnp.stack([ncols[:, h, :, h] for h in range(H_M)], axis=1)
    return c, n, m[:, 0, :H_M]


def _layer(x, layer, wts, attn_fn, mem_k, mem_v, ml_state, conv_prev, g_final, final,
           batch, seq, tm_proj, L, tm_post):
    (g_mix, w_proj, w_gt, gate_b, conv_w, lam_p, g_att, g_ml, w_out,
     g_cross, w_cq, w_co, g_ffn, w_gate, w_up, w_down) = wts
    t = batch * seq
    lam_init = 0.8 - 0.6 * math.exp(-0.3 * layer)
    qa, kf, kb, vf, vb, qkvm, og, bch, gt = _in_proj(x, g_mix, w_proj, w_gt, tm_proj)
    oa = attn_fn(qa, kb, vb, lam_p, g_att, lam_init)
    gt = jnp.transpose(gt.reshape(8, batch, seq // L, L), (1, 2, 0, 3))
    tok3 = lambda a: a.reshape(batch, seq, a.shape[-1])
    hm, oc, s1, m1, cv1 = _recurrent(gate_b, tok3(qkvm), tok3(og), gt, tok3(bch), conv_w, g_ml,
                                     ml_state[0], ml_state[1], conv_prev, batch, seq, L, RECURRENT_STREAMS)
    hm, oc = hm.reshape(t, ML_WIDTH), oc.reshape(t, CONV_CH)
    x = _post(x, oa, hm, oc, mem_k, mem_v, w_out, g_cross, w_cq, w_co, g_ffn, w_gate, w_up, w_down,
              g_final, tm_post, seq // tm_post, final)
    c1, n1, mm1 = _unpack_state(s1, m1)
    return x, kf, vf, c1, n1, mm1, cv1


def kernel(x_prompt, x_sample, mem_prompt, cache_att_k, cache_att_v, cache_mem_k, cache_mem_v,
           state_mlstm_C, state_mlstm_n, state_mlstm_m, state_conv,
           norm_mix, w_in, mlstm_gate_bias, conv_w, lambda_params, norm_att_heads, norm_mlstm_heads,
           w_out, norm_cross, w_cq, w_ck, w_cv, w_co, norm_ffn, w_gate, w_up, w_down,
           rel_bias, norm_final):
    bp, sp, _ = x_prompt.shape
    bs, ss, _ = x_sample.shape
    depth = w_in.shape[0]
    past = cache_att_k.shape[2]
    n_mem = mem_prompt.shape[1]
    t_attn = 256
    l_prompt = 128

    xp = x_prompt.reshape(bp * sp, D_MODEL)
    xs = x_sample.reshape(bs * ss, D_MODEL)
    mem = mem_prompt.reshape(bp * n_mem, D_MODEL)
    g_final = norm_final.reshape(1, D_MODEL)

    bias_tiles = _prompt_bias_tiles(rel_bias, t_attn)
    rel_s = jnp.arange(past + ss, dtype=jnp.int32)[None, :] - (past + jnp.arange(ss, dtype=jnp.int32))[:, None]
    bias_s = _bias_table(rel_bias, rel_s) * LOG2E
    bias_s_past, bias_s_new = bias_s[:, :, :past], bias_s[:, :, past:]

    zero_state = (jnp.zeros((bp, ML_WIDTH, ML_STATE_W), F32), jnp.zeros((bp, 1, 128), F32))
    zero_conv = jnp.zeros((bp, CONV_K - 1, CONV_CH), F32)

    outs = {k: [] for k in ('pk', 'pv', 'pmk', 'pmv', 'pC', 'pn', 'pm', 'pcv', 'sk', 'sv', 'sC', 'sn', 'sm', 'scv')}
    gate_lo = 3 * ATT_WIDTH + 4 * ML_WIDTH
    for l in range(depth):
        w = w_in[l]
        wts = (norm_mix[l].reshape(1, D_MODEL),
               jnp.concatenate([w[:, :gate_lo], w[:, gate_lo + 2 * H_M:]], axis=1).astype(BF16),
               w[:, gate_lo:gate_lo + 2 * H_M].T.astype(BF16),
               mlstm_gate_bias[l].astype(F32), conv_w[l].astype(F32), lambda_params[l].astype(F32),
               norm_att_heads[l].reshape(1, ATT_WIDTH), norm_mlstm_heads[l].reshape(1, ML_WIDTH),
               w_out[l].astype(BF16), norm_cross[l].reshape(1, D_MODEL),
               w_cq[l].astype(BF16), w_co[l].astype(BF16), norm_ffn[l].reshape(1, D_MODEL),
               w_gate[l].astype(BF16), w_up[l].astype(BF16), w_down[l].astype(BF16))
        final = l == depth - 1

        w_kv = jnp.concatenate([w_ck[l], w_cv[l]], axis=1).astype(BF16)
        mkf, mvf, mkb, mvb = _mem_kv(mem, w_kv, 512)
        attn_p = lambda q, k, v, lam_p, g, li: _attn_prompt(q, k, v, bias_tiles, lam_p, g, bp, sp, t_attn, li)
        xp, kf, vf, c1, n1, m1, cv1 = _layer(
            xp, l, wts, attn_p, mkb.reshape(bp, n_mem, -1), mvb.reshape(bp, n_mem, -1),
            zero_state, zero_conv, g_final, final, bp, sp, 512, l_prompt, 512)
        outs['pk'].append(kf.reshape(bp, sp, H_A, 2 * D_A))
        outs['pv'].append(vf.reshape(bp, sp, H_A, DV_A))
        outs['pmk'].append(mkf.reshape(bp, n_mem, H_X, D_X))
        outs['pmv'].append(mvf.reshape(bp, n_mem, H_X, D_X))
        outs['pC'].append(c1); outs['pn'].append(n1); outs['pm'].append(m1); outs['pcv'].append(cv1)

        k_past = cache_att_k[l].reshape(bs, past, ATT_WIDTH)
        v_past = cache_att_v[l].reshape(bs, past, ATT_WIDTH)
        attn_s = lambda q, k, v, lam_p, g, li: _attn_sample(q, k, v, k_past, v_past, bias_s_past, bias_s_new,
                                                            lam_p, g, bs, ss, li)
        s0 = _block_diag_state(state_mlstm_C[l], state_mlstm_n[l])
        m0 = jnp.pad(state_mlstm_m[l].astype(F32), ((0, 0), (0, 128 - H_M))).reshape(bs, 1, 128)
        xs, kf, vf, c1, n1, m1, cv1 = _layer(
            xs, l, wts, attn_s,
            cache_mem_k[l].reshape(bs, n_mem, -1).astype(BF16), cache_mem_v[l].reshape(bs, n_mem, -1).astype(BF16),
            (s0, m0), state_conv[l].astype(F32), g_final, final, bs, ss, bs * ss, ss, ss)
        outs['sk'].append(kf.reshape(bs, ss, H_A, 2 * D_A))
        outs['sv'].append(vf.reshape(bs, ss, H_A, DV_A))
        outs['sC'].append(c1); outs['sn'].append(n1); outs['sm'].append(m1); outs['scv'].append(cv1)

    st = lambda k: jnp.stack(outs[k])
    return (xp.reshape(bp, sp, D_MODEL), xs.reshape(bs, ss, D_MODEL),
            st('pk'), st('pv'), st('pmk'), st('pmv'), st('pC'), st('pn'), st('pm'), st('pcv'),
            st('sk'), st('sv'), st('sC'), st('sn'), st('sm'), st('scv'))
```

```python
import functools
import math

import jax
import jax.numpy as jnp
from jax import lax
from jax.experimental import pallas as pl
from jax.experimental.pallas import tpu as pltpu

F32 = jnp.float32
BF16 = jnp.bfloat16

D_MODEL = 1024
CHUNK = 64
H_A = 4
D_A = 64
DV_A = 128
ATT_WIDTH = H_A * DV_A
H_M = 4
D_M = 64
ML_WIDTH = H_M * D_M
CONV_CH = 256
CONV_K = 3
H_X = 4
D_X = 256
D_FF = 2816
NUM_BUCKETS = 32
MAX_DISTANCE = 128
EPS = 1e-6

N_PROJ = 3 * ATT_WIDTH + 4 * ML_WIDTH + 3 * CONV_CH
ML_STATE_W = ML_WIDTH + 128

V7X_VMEM_LIMIT = 56 * 1024 * 1024

LOG2E = math.log2(math.e)
RECURRENT_STREAMS = 8
ATTN_HEADS_PER_STEP = 4

_NT = (((1,), (1,)), ((), ()))
_TN = (((0,), (0,)), ((), ()))


def _const_spec(shape):
    nd = len(shape)
    return pl.BlockSpec(shape, lambda *_: (0,) * nd, pipeline_mode=pl.Buffered(1))


def _rms(x, g):
    ms = jnp.mean(x * x, axis=-1, keepdims=True)
    return x * lax.rsqrt(ms + EPS) * g


def _in_proj_kernel(x_ref, g_ref, w_ref, wgt_ref,
                    qa_ref, kf_ref, kb_ref, vf_ref, vb_ref, qkvm_ref, om_ref, bch_ref, gt_ref):
    xn = _rms(x_ref[...], g_ref[...]).astype(BF16)

    def proj(lo, hi):
        return jnp.dot(xn, w_ref[:, lo:hi], preferred_element_type=F32)

    a = ATT_WIDTH
    qa_ref[...] = (proj(0, a) * (D_A ** -0.5 * LOG2E)).astype(BF16)
    k = proj(a, 2 * a)
    kf_ref[...] = k
    kb_ref[...] = k.astype(BF16)
    v = proj(2 * a, 3 * a)
    vf_ref[...] = v
    vb_ref[...] = v.astype(BF16)
    o = 3 * a
    m = ML_WIDTH
    qkvm_ref[:, 0:m] = proj(o, o + m).astype(BF16)
    qkvm_ref[:, m:2 * m] = (proj(o + m, o + 2 * m) * (D_M ** -0.5)).astype(BF16)
    qkvm_ref[:, 2 * m:3 * m] = proj(o + 2 * m, o + 3 * m).astype(BF16)
    om_ref[...] = proj(o + 3 * m, o + 4 * m)
    bch_ref[...] = proj(o + 4 * m, N_PROJ)
    gt_ref[...] = lax.dot_general(wgt_ref[...], xn, _NT, preferred_element_type=F32)


def _in_proj(x, g, w, wgt, tm):
    t = x.shape[0]
    a, m = ATT_WIDTH, ML_WIDTH
    row = lambda n: pl.BlockSpec((tm, n), lambda i: (i, 0))
    outs = [
        (jax.ShapeDtypeStruct((t, a), BF16), row(a)),
        (jax.ShapeDtypeStruct((t, a), F32), row(a)),
        (jax.ShapeDtypeStruct((t, a), BF16), row(a)),
        (jax.ShapeDtypeStruct((t, a), F32), row(a)),
        (jax.ShapeDtypeStruct((t, a), BF16), row(a)),
        (jax.ShapeDtypeStruct((t, 3 * m), BF16), row(3 * m)),
        (jax.ShapeDtypeStruct((t, m), F32), row(m)),
        (jax.ShapeDtypeStruct((t, 3 * CONV_CH), F32), row(3 * CONV_CH)),
        (jax.ShapeDtypeStruct((8, t), F32), pl.BlockSpec((8, tm), lambda i: (0, i))),
    ]
    return pl.pallas_call(
        _in_proj_kernel,
        grid=(t // tm,),
        in_specs=[row(D_MODEL), _const_spec((1, D_MODEL)), _const_spec((D_MODEL, N_PROJ)),
                  _const_spec((8, D_MODEL))],
        out_specs=[s for _, s in outs],
        out_shape=[s for s, _ in outs],
        compiler_params=pltpu.CompilerParams(dimension_semantics=("parallel",),
                                             vmem_limit_bytes=V7X_VMEM_LIMIT),
        name="in_proj",
    )(x, g, w, wgt)


def _lambda_value(lp, lam_init):
    a = jnp.sum(lp[0:1] * lp[1:2], axis=1, keepdims=True)
    b = jnp.sum(lp[2:3] * lp[3:4], axis=1, keepdims=True)
    return jnp.exp(a) - jnp.exp(b) + lam_init


def _stack_maps(q):
    lane = lax.broadcasted_iota(jnp.int32, q.shape, 1)
    zero = jnp.zeros_like(q)
    return jnp.concatenate([jnp.where(lane < D_A, q, zero), jnp.where(lane >= D_A, q, zero)], axis=0)


def _attn_finish(acc, lam, g, lam_init, t):
    o = acc[:, :DV_A] / acc[:, DV_A:]
    d = o[:t] - lam * o[t:]
    return (_rms(d, g) * (1.0 - lam_init)).astype(BF16)


def _with_ones(v):
    return jnp.concatenate([v, jnp.ones_like(v)], axis=1)


def _attn_prompt_kernel(lam_ref, g_ref, q_ref, k_ref, v_ref, bias_ref, o_ref,
                        m_scr, l_scr, acc_scr, sa_scr, sb_scr, p_scr, *, t, hp, lam_init):
    i = pl.program_id(2)
    heads = [slice(hh * DV_A, (hh + 1) * DV_A) for hh in range(hp)]
    cols = [slice(hh * 2 * t, (hh + 1) * 2 * t) for hh in range(hp)]
    qz = [_stack_maps(q_ref[:, hd]) for hd in heads]

    def rows(kb):
        return pl.ds(pl.multiple_of(kb * t, t), t)

    def qk(kb):
        r = rows(kb)
        return jnp.concatenate([lax.dot_general(k_ref[r, hd], z, _NT, preferred_element_type=F32)
                                for hd, z in zip(heads, qz)], axis=1)

    def pv(p, kb):
        r = rows(kb)
        return jnp.concatenate([lax.dot_general(v_ref[r, hd], p[:, c], _TN, preferred_element_type=F32)
                                for hd, c in zip(heads, cols)], axis=1)

    def bias(lo):
        return jnp.concatenate([bias_ref[hh, 0, lo:lo + t, :] for hh in range(hp)], axis=1)

    def softmax_step(m_prev, s):
        m_new = jnp.maximum(m_prev, jnp.max(s, axis=0, keepdims=True))
        p = jnp.exp2(s - m_new)
        return m_new, jnp.exp2(m_prev - m_new), p, jnp.sum(p, axis=0, keepdims=True)

    n_far = jnp.maximum(i - 1, 0)
    last = n_far - 1
    kb_prev = jnp.maximum(i - 1, 0)

    s_diag = qk(i)
    s_prev = qk(kb_prev)
    sa_scr[...] = qk(0)
    s_diag = s_diag + bias(t)
    m = jnp.max(s_diag, axis=0, keepdims=True)
    p = jnp.exp2(s_diag - m)
    l = jnp.sum(p, axis=0, keepdims=True)
    acc = pv(p.astype(BF16), i)
    w_prev = (i >= 1).astype(F32)
    m, alpha, p, p_sum = softmax_step(m, s_prev + bias(0))
    m_scr[...] = m
    l_scr[...] = alpha * l + w_prev * p_sum
    acc_scr[...] = alpha * acc
    p_scr[...] = p.astype(BF16)

    def far_pair(j, carry):
        kb1 = jnp.minimum(2 * j + 1, last)
        w1 = (2 * j + 1 <= last).astype(F32)
        kb_pending = jnp.where(j == 0, kb_prev, 2 * j - 1)
        sb_scr[...] = qk(kb1)
        owed = pv(p_scr[...], kb_pending)
        m, alpha, p, p_sum = softmax_step(m_scr[...], sa_scr[...])
        l = alpha * l_scr[...] + p_sum
        acc = alpha * (acc_scr[...] + owed)
        sa_scr[...] = qk(jnp.minimum(2 * j + 2, last))
        owed = pv(p.astype(BF16), 2 * j)
        m, alpha, p, p_sum = softmax_step(m, sb_scr[...])
        m_scr[...] = m
        l_scr[...] = alpha * l + w1 * p_sum
        acc_scr[...] = alpha * (acc + owed)
        p_scr[...] = p.astype(BF16)
        return carry

    trips = (n_far + 1) // 2
    lax.fori_loop(0, trips, far_pair, 0)

    kb_pending = jnp.where(trips == 0, kb_prev, jnp.minimum(2 * trips - 1, last))
    w_pending = jnp.where(trips == 0, w_prev, (n_far % 2 == 0).astype(F32))
    acc = acc_scr[...] + w_pending * pv(p_scr[...], kb_pending)

    lam = _lambda_value(lam_ref[...], lam_init)
    o = acc * (1.0 / l_scr[...])
    for hd, c in zip(heads, cols):
        d = o[:, c][:, :t] - lam * o[:, c][:, t:]
        ms = jnp.mean(d * d, axis=0, keepdims=True)
        y = jnp.transpose(d * lax.rsqrt(ms + EPS))
        o_ref[:, hd] = (y * g_ref[:, hd] * (1.0 - lam_init)).astype(BF16)


def _attn_prompt(q, k, v, bias, lam_p, g_att, batch, seq, t, lam_init):
    nq = seq // t
    hp = ATTN_HEADS_PER_STEP
    w = hp * 2 * t
    kv_spec = pl.BlockSpec((seq, hp * DV_A), lambda b, h, i: (b, h))
    return pl.pallas_call(
        functools.partial(_attn_prompt_kernel, t=t, hp=hp, lam_init=lam_init),
        grid=(batch, H_A // hp, nq),
        in_specs=[_const_spec((4, D_A)),
                  pl.BlockSpec((1, hp * DV_A), lambda b, h, i: (0, h)),
                  pl.BlockSpec((t, hp * DV_A), lambda b, h, i: (b * nq + i, h)),
                  kv_spec, kv_spec,
                  pl.BlockSpec((hp, 1, 2 * t, 2 * t), lambda b, h, i: (h, jnp.minimum(i, 1), 0, 0))],
        out_specs=pl.BlockSpec((t, hp * DV_A), lambda b, h, i: (b * nq + i, h)),
        out_shape=jax.ShapeDtypeStruct((batch * seq, ATT_WIDTH), BF16),
        scratch_shapes=[pltpu.VMEM((1, w), F32), pltpu.VMEM((1, w), F32),
                        pltpu.VMEM((DV_A, w), F32),
                        pltpu.VMEM((t, w), F32), pltpu.VMEM((t, w), F32),
                        pltpu.VMEM((t, w), BF16)],
        compiler_params=pltpu.CompilerParams(dimension_semantics=("parallel", "parallel", "arbitrary"),
                                             vmem_limit_bytes=V7X_VMEM_LIMIT),
        name="attn_prompt",
    )(lam_p, g_att, q, k, v, bias)


def _attn_sample_kernel(lam_ref, g_ref, q_ref, kp_ref, vp_ref, kn_ref, vn_ref, bp_ref, bn_ref, o_ref,
                        *, lq, lam_init):
    qz = _stack_maps(q_ref[...])
    bp = bp_ref[0]
    bn = bn_ref[0]
    sp = lax.dot_general(qz, kp_ref[0].astype(BF16), _NT, preferred_element_type=F32)
    sp = sp + jnp.concatenate([bp, bp], axis=0)
    sn = lax.dot_general(qz, kn_ref[...], _NT, preferred_element_type=F32)
    sn = sn + jnp.concatenate([bn, bn], axis=0)
    m = jnp.maximum(jnp.max(sp, axis=1, keepdims=True), jnp.max(sn, axis=1, keepdims=True))
    pp = jnp.exp2(sp - m)
    pn = jnp.exp2(sn - m)
    acc = (jnp.dot(pp.astype(BF16), _with_ones(vp_ref[0].astype(BF16)), preferred_element_type=F32)
           + jnp.dot(pn.astype(BF16), _with_ones(vn_ref[...]), preferred_element_type=F32))
    lam = _lambda_value(lam_ref[...], lam_init)
    o_ref[...] = _attn_finish(acc, lam, g_ref[...], lam_init, lq)


def _attn_sample(q, k_new, v_new, k_past, v_past, bias_past, bias_new, lam_p, g_att, batch, lq, lam_init):
    past = k_past.shape[1]
    new_spec = pl.BlockSpec((lq, DV_A), lambda b, h: (b, h))
    past_spec = pl.BlockSpec((1, past, DV_A), lambda b, h: (b, 0, h))
    return pl.pallas_call(
        functools.partial(_attn_sample_kernel, lq=lq, lam_init=lam_init),
        grid=(batch, H_A),
        in_specs=[_const_spec((4, D_A)),
                  pl.BlockSpec((1, DV_A), lambda b, h: (0, h)),
                  new_spec, past_spec, past_spec, new_spec, new_spec,
                  pl.BlockSpec((1, lq, past), lambda b, h: (h, 0, 0)),
                  pl.BlockSpec((1, lq, lq), lambda b, h: (h, 0, 0))],
        out_specs=new_spec,
        out_shape=jax.ShapeDtypeStruct((batch * lq, ATT_WIDTH), BF16),
        compiler_params=pltpu.CompilerParams(dimension_semantics=("parallel", "parallel"),
                                             vmem_limit_bytes=V7X_VMEM_LIMIT),
        name="attn_sample",
    )(lam_p, g_att, q, k_past, v_past, k_new, v_new, bias_past, bias_new)


def _per_stream(fn, *arrays):
    return jnp.stack([fn(*(a[g] for a in arrays)) for g in range(arrays[0].shape[0])])


def _recurrent_kernel(gb_ref, q_ref, k_ref, v_ref, og_ref, gt_ref, b_ref, c_ref, hc_ref, cw_ref, gml_ref,
                      s0_ref, m0_ref, cv0_ref,
                      hm_ref, oc_ref, s_ref, m_ref, cv_ref, *, L):
    @pl.when(pl.program_id(1) == 0)
    def _():
        s_ref[...] = s0_ref[...]
        m_ref[...] = m0_ref[...]
        cv_ref[...] = cv0_ref[...]

    q = q_ref[...]
    k = k_ref[...]
    v = v_ref[...]
    G = q.shape[0]
    state = s_ref[...]
    q_state = _per_stream(lambda a, b: jnp.dot(a, b, preferred_element_type=F32),
                          q, state.astype(BF16))
    q_c = q_state[:, :, :ML_WIDTH]

    lane_head = lax.broadcasted_iota(jnp.int32, (1, L, ML_WIDTH), 2) // D_M
    row = lax.broadcasted_iota(jnp.int32, (1, L, L), 1)
    col = lax.broadcasted_iota(jnp.int32, (1, L, L), 2)
    causal = col <= row
    eye = col == row
    st_lane = lax.broadcasted_iota(jnp.int32, (1, 1, ML_STATE_W), 2)
    st_lane_head = jnp.where(st_lane < ML_WIDTH, st_lane // D_M, st_lane - ML_WIDTH)
    m_lane = lax.broadcasted_iota(jnp.int32, (1, 1, 128), 2)

    gt = gt_ref[:, 0]
    m_all = m_ref[...]
    num = jnp.zeros((G, L, ML_WIDTH), F32)
    w_state = jnp.zeros((G, L, ML_WIDTH), F32)
    decay = jnp.zeros((G, 1, ML_STATE_W), F32)
    m_next = jnp.zeros((G, 1, 128), F32)
    for h in range(H_M):
        ig = gt[:, h:h + 1, :] + gb_ref[0, h]
        fz = gt[:, H_M + h:H_M + h + 1, :] + gb_ref[1, h]
        lf = jnp.minimum(fz, 0.0) - jnp.log1p(jnp.exp(-jnp.abs(fz)))
        f_col = jnp.sum(jnp.where(causal, lf, 0.0), axis=2, keepdims=True)
        f_row = jnp.sum(jnp.where(eye, f_col, 0.0), axis=1, keepdims=True)
        d = jnp.where(causal, f_col + (ig - f_row), -jnp.inf)
        m_prev = m_all[:, :, h:h + 1]
        inter = f_col + m_prev
        mt = jnp.maximum(inter, jnp.max(d, axis=2, keepdims=True))
        w_intra = jnp.exp(d - mt)
        w_inter = jnp.exp(inter - mt)
        head = lane_head == h
        qk = _per_stream(lambda a, b: lax.dot_general(a, b, _NT, preferred_element_type=F32),
                         jnp.where(head, q, jnp.zeros_like(q)), k)
        s = qk * w_intra
        den = jnp.sum(s, axis=2, keepdims=True) + w_inter * q_state[:, :, ML_WIDTH + h:ML_WIDTH + h + 1]
        inv = 1.0 / jnp.maximum(jnp.abs(den), jnp.exp(-mt))
        sv = _per_stream(lambda a, b: jnp.dot(a, b, preferred_element_type=F32), s.astype(BF16), v)
        num = jnp.where(head, (sv + w_inter * q_c) * inv, num)
        m_new = mt[:, L - 1:L, :]
        w_col = jnp.sum(jnp.where(eye, w_intra[:, L - 1:L, :], 0.0), axis=2, keepdims=True)
        w_state = jnp.where(head, w_col, w_state)
        dec = jnp.exp(f_col[:, L - 1:L, :] + m_prev - m_new)
        decay = jnp.where(st_lane_head == h, dec, decay)
        m_next = jnp.where(m_lane == h, m_new, m_next)

    kw = (k.astype(F32) * w_state).astype(BF16)
    v_aug = jnp.concatenate([v, jnp.ones((G, L, ML_STATE_W - ML_WIDTH), BF16)], axis=2)
    upd = _per_stream(lambda a, b: lax.dot_general(a, b, _TN, preferred_element_type=F32), kw, v_aug)
    s_row_head = lax.broadcasted_iota(jnp.int32, (1, ML_WIDTH, ML_STATE_W), 1) // D_M
    s_col = lax.broadcasted_iota(jnp.int32, (1, ML_WIDTH, ML_STATE_W), 2)
    s_col_head = jnp.where(s_col < ML_WIDTH, s_col // D_M, s_col - ML_WIDTH)
    s_ref[...] = decay * state + jnp.where(s_row_head == s_col_head, upd, 0.0)
    m_ref[...] = m_next

    hg = num * jax.nn.sigmoid(og_ref[...])
    scale = jnp.zeros((G, L, ML_WIDTH), F32)
    for h in range(H_M):
        head = lane_head == h
        ss = jnp.sum(jnp.where(head, hg * hg, 0.0), axis=2, keepdims=True)
        scale = jnp.where(head, lax.rsqrt(ss * (1.0 / D_M) + EPS), scale)
    hm_ref[...] = (hg * scale * gml_ref[...]).astype(BF16)

    u = c_ref[...] * hc_ref[...]
    prev = cv_ref[...]
    ri = lax.broadcasted_iota(jnp.int32, (1, L, CONV_CH), 1)
    u1 = jnp.where(ri == 0, prev[:, 1:2], _per_stream(lambda a: pltpu.roll(a, 1, 0), u))
    u2 = jnp.where(ri == 0, prev[:, 0:1],
                   jnp.where(ri == 1, prev[:, 1:2], _per_stream(lambda a: pltpu.roll(a, 2, 0), u)))
    w = cw_ref[...]
    oc_ref[...] = (b_ref[...] * (w[0:1] * u2 + w[1:2] * u1 + w[2:3] * u)).astype(BF16)
    cv_ref[:, 0:1, :] = u[:, L - 2:L - 1]
    cv_ref[:, 1:2, :] = u[:, L - 1:L]


def _recurrent(gate_b, qkvm, og, gt, bch, conv_w, g_ml, s0, m0, cv0, batch, seq, L, G):
    assert batch % G == 0 and seq % L == 0
    nc = seq // L
    tok = lambda j: pl.BlockSpec((G, L, ML_WIDTH), lambda b, c: (b, c, j))
    per_b = lambda shape: pl.BlockSpec((G,) + shape, lambda b, c: (b, 0, 0))
    return pl.pallas_call(
        functools.partial(_recurrent_kernel, L=L),
        grid=(batch // G, nc),
        in_specs=[pl.BlockSpec(memory_space=pltpu.SMEM),
                  tok(0), tok(1), tok(2), tok(0),
                  pl.BlockSpec((G, 1, 8, L), lambda b, c: (b, c, 0, 0)),
                  tok(0), tok(1), tok(2),
                  _const_spec((CONV_K, CONV_CH)), _const_spec((1, ML_WIDTH)),
                  per_b((ML_WIDTH, ML_STATE_W)), per_b((1, 128)), per_b((CONV_K - 1, CONV_CH))],
        out_specs=[tok(0), tok(0),
                   per_b((ML_WIDTH, ML_STATE_W)), per_b((1, 128)), per_b((CONV_K - 1, CONV_CH))],
        out_shape=[jax.ShapeDtypeStruct((batch, seq, ML_WIDTH), BF16),
                   jax.ShapeDtypeStruct((batch, seq, CONV_CH), BF16),
                   jax.ShapeDtypeStruct((batch, ML_WIDTH, ML_STATE_W), F32),
                   jax.ShapeDtypeStruct((batch, 1, 128), F32),
                   jax.ShapeDtypeStruct((batch, CONV_K - 1, CONV_CH), F32)],
        compiler_params=pltpu.CompilerParams(dimension_semantics=("parallel", "arbitrary"),
                                             vmem_limit_bytes=V7X_VMEM_LIMIT),
        name="recurrent",
    )(gate_b, qkvm, qkvm, qkvm, og, gt, bch, bch, bch, conv_w, g_ml, s0, m0, cv0)


def _mem_kv_kernel(x_ref, w_ref, kf_ref, vf_ref, kb_ref, vb_ref):
    x = x_ref[...].astype(BF16)
    n = H_X * D_X
    k = jnp.dot(x, w_ref[:, :n], preferred_element_type=F32)
    v = jnp.dot(x, w_ref[:, n:], preferred_element_type=F32)
    kf_ref[...] = k
    vf_ref[...] = v
    kb_ref[...] = k.astype(BF16)
    vb_ref[...] = v.astype(BF16)


def _mem_kv(mem, w_kv, tm):
    t = mem.shape[0]
    n = H_X * D_X
    row = pl.BlockSpec((tm, n), lambda i: (i, 0))
    return pl.pallas_call(
        _mem_kv_kernel,
        grid=(t // tm,),
        in_specs=[pl.BlockSpec((tm, D_MODEL), lambda i: (i, 0)), _const_spec((D_MODEL, 2 * n))],
        out_specs=[row, row, row, row],
        out_shape=[jax.ShapeDtypeStruct((t, n), F32), jax.ShapeDtypeStruct((t, n), F32),
                   jax.ShapeDtypeStruct((t, n), BF16), jax.ShapeDtypeStruct((t, n), BF16)],
        compiler_params=pltpu.CompilerParams(dimension_semantics=("parallel",),
                                             vmem_limit_bytes=V7X_VMEM_LIMIT),
        name="mem_kv",
    )(mem, w_kv)


FF_CHUNK = 256


def _post_kernel(x_ref, oa_ref, hm_ref, oc_ref, mk_ref, mv_ref, wout_ref, gc_ref, wcq_ref, wco_ref,
                 gf_ref, wg_ref, wu_ref, wd_ref, gfin_ref, o_ref, *, final):
    a, m = ATT_WIDTH, ML_WIDTH
    x = x_ref[...]
    x = x + (jnp.dot(oa_ref[...], wout_ref[0:a, :], preferred_element_type=F32)
             + jnp.dot(hm_ref[...], wout_ref[a:a + m, :], preferred_element_type=F32)
             + jnp.dot(oc_ref[...], wout_ref[a + m:, :], preferred_element_type=F32))

    xn = _rms(x, gc_ref[...]).astype(BF16)
    qc = (jnp.dot(xn, wcq_ref[...], preferred_element_type=F32) * (D_X ** -0.5)).astype(BF16)
    cross = jnp.zeros_like(x)
    n_streams = mk_ref.shape[0]
    rows = x.shape[0] // n_streams
    for h in range(H_X):
        sl = slice(h * D_X, (h + 1) * D_X)
        outs = []
        for b in range(n_streams):
            r = slice(b * rows, (b + 1) * rows)
            s = lax.dot_general(qc[r, sl], mk_ref[b, :, sl], _NT, preferred_element_type=F32)
            p = jnp.exp(s - jnp.max(s, axis=1, keepdims=True))
            p = p * (1.0 / jnp.sum(p, axis=1, keepdims=True))
            outs.append(jnp.dot(p.astype(BF16), mv_ref[b, :, sl], preferred_element_type=F32).astype(BF16))
        o = outs[0] if n_streams == 1 else jnp.concatenate(outs, axis=0)
        cross = cross + jnp.dot(o, wco_ref[sl, :], preferred_element_type=F32)
    x = x + cross

    xn = _rms(x, gf_ref[...]).astype(BF16)
    ff = jnp.zeros_like(x)
    for j in range(D_FF // FF_CHUNK):
        sl = slice(j * FF_CHUNK, (j + 1) * FF_CHUNK)
        g = jnp.dot(xn, wg_ref[:, sl], preferred_element_type=F32)
        u = jnp.dot(xn, wu_ref[:, sl], preferred_element_type=F32)
        act = (g * jax.nn.sigmoid(g) * u).astype(BF16)
        ff = ff + jnp.dot(act, wd_ref[sl, :], preferred_element_type=F32)
    x = x + ff
    if final:
        x = _rms(x, gfin_ref[...])
    o_ref[...] = x


def _post(x, oa, hm, oc, mk, mv, w_out, g_cross, w_cq, w_co, g_ffn, w_gate, w_up, w_down, g_final,
          tm, seq, final):
    t = x.shape[0]
    row = lambda n: pl.BlockSpec((tm, n), lambda i: (i, 0))
    if tm <= seq:
        tiles_per_stream = seq // tm
        mem = pl.BlockSpec((1,) + mk.shape[1:], lambda i: (i // tiles_per_stream, 0, 0))
    else:
        mem = pl.BlockSpec((tm // seq,) + mk.shape[1:], lambda i: (i, 0, 0))
    vec = _const_spec((1, D_MODEL))
    return pl.pallas_call(
        functools.partial(_post_kernel, final=final),
        grid=(t // tm,),
        in_specs=[row(D_MODEL), row(ATT_WIDTH), row(ML_WIDTH), row(CONV_CH), mem, mem,
                  _const_spec(w_out.shape), vec, _const_spec(w_cq.shape), _const_spec(w_co.shape),
                  vec, _const_spec(w_gate.shape), _const_spec(w_up.shape), _const_spec(w_down.shape), vec],
        out_specs=row(D_MODEL),
        out_shape=jax.ShapeDtypeStruct((t, D_MODEL), F32),
        compiler_params=pltpu.CompilerParams(dimension_semantics=("parallel",),
                                             vmem_limit_bytes=V7X_VMEM_LIMIT),
        name="post",
    )(x, oa, hm, oc, mk, mv, w_out, g_cross, w_cq, w_co, g_ffn, w_gate, w_up, w_down, g_final)


def _rel_bucket(rel):
    half = NUM_BUCKETS // 2
    max_exact = half // 2
    n = jnp.abs(rel)
    large = max_exact + (jnp.log(jnp.maximum(n, 1).astype(F32) / max_exact)
                         / math.log(MAX_DISTANCE / max_exact) * (half - max_exact)).astype(jnp.int32)
    large = jnp.minimum(large, half - 1)
    return jnp.where(rel > 0, half, 0) + jnp.where(n < max_exact, n, large)


def _bias_table(rel_bias, rel):
    bucket = _rel_bucket(rel)[None]
    table = rel_bias.astype(F32)
    out = jnp.zeros((table.shape[1],) + rel.shape, F32)
    for b in range(NUM_BUCKETS):
        out = jnp.where(bucket == b, table[b][:, None, None], out)
    return out


def _prompt_bias_tiles(rel_bias, t):
    key = jnp.arange(t, dtype=jnp.int32)[:, None]
    qry = jnp.arange(t, dtype=jnp.int32)[None, :]
    far = rel_bias[_rel_bucket(jnp.int32(-MAX_DISTANCE))].astype(F32)[:, None, None]
    prev = (_bias_table(rel_bias, key - t - qry) - far) * LOG2E
    diag = jnp.where((key // CHUNK) <= (qry // CHUNK), (_bias_table(rel_bias, key - qry) - far) * LOG2E, -jnp.inf)
    both_maps = lambda a: jnp.concatenate([a, a], axis=2)
    tiles = jnp.stack([jnp.concatenate([diag, diag], axis=1), jnp.concatenate([prev, diag], axis=1)], axis=1)
    return lax.optimization_barrier(both_maps(tiles.reshape(-1, 2 * t, t)).reshape(-1, 2, 2 * t, 2 * t))


def _block_diag_state(c, n):
    b = c.shape[0]
    eye = jnp.eye(H_M, dtype=F32)
    cbd = jnp.einsum('bhdv,hg->bhdgv', c.astype(F32), eye).reshape(b, ML_WIDTH, ML_WIDTH)
    ncol = jnp.einsum('bhd,hg->bhdg', n.astype(F32), eye).reshape(b, ML_WIDTH, H_M)
    pad = jnp.zeros((b, ML_WIDTH, ML_STATE_W - ML_WIDTH - H_M), F32)
    return jnp.concatenate([cbd, ncol, pad], axis=2)


def _unpack_state(s, m):
    b = s.shape[0]
    blocks = s[:, :, :ML_WIDTH].reshape(b, H_M, D_M, H_M, D_M)
    c = jnp.stack([blocks[:, h, :, h, :] for h in range(H_M)], axis=1)
    ncols = s[:, :, ML_WIDTH:ML_WIDTH + H_M].reshape(b, H_M, D_M, H_M)
    n = jnp.stack([ncols[:, h, :, h] for h in range(H_M)], axis=1)
    return c, n, m[:, 0, :H_M]


def _layer(x, layer, wts, attn_fn, mem_k, mem_v, ml_state, conv_prev, g_final, final,
           batch, seq, tm_proj, L, tm_post):
    (g_mix, w_proj, w_gt, gate_b, conv_w, lam_p, g_att, g_ml, w_out,
     g_cross, w_cq, w_co, g_ffn, w_gate, w_up, w_down) = wts
    t = batch * seq
    lam_init = 0.8 - 0.6 * math.exp(-0.3 * layer)
    qa, kf, kb, vf, vb, qkvm, og, bch, gt = _in_proj(x, g_mix, w_proj, w_gt, tm_proj)
    oa = attn_fn(qa, kb, vb, lam_p, g_att, lam_init)
    gt = jnp.transpose(gt.reshape(8, batch, seq // L, L), (1, 2, 0, 3))
    tok3 = lambda a: a.reshape(batch, seq, a.shape[-1])
    hm, oc, s1, m1, cv1 = _recurrent(gate_b, tok3(qkvm), tok3(og), gt, tok3(bch), conv_w, g_ml,
                                     ml_state[0], ml_state[1], conv_prev, batch, seq, L,
                                     min(RECURRENT_STREAMS, batch))
    hm, oc = hm.reshape(t, ML_WIDTH), oc.reshape(t, CONV_CH)
    x = _post(x, oa, hm, oc, mem_k, mem_v, w_out, g_cross, w_cq, w_co, g_ffn, w_gate, w_up, w_down,
              g_final, tm_post, seq, final)
    c1, n1, mm1 = _unpack_state(s1, m1)
    return x, kf, vf, c1, n1, mm1, cv1


def kernel(x_prompt, x_sample, mem_prompt, cache_att_k, cache_att_v, cache_mem_k, cache_mem_v,
           state_mlstm_C, state_mlstm_n, state_mlstm_m, state_conv,
           norm_mix, w_in, mlstm_gate_bias, conv_w, lambda_params, norm_att_heads, norm_mlstm_heads,
           w_out, norm_cross, w_cq, w_ck, w_cv, w_co, norm_ffn, w_gate, w_up, w_down,
           rel_bias, norm_final):
    bp, sp, _ = x_prompt.shape
    bs, ss, _ = x_sample.shape
    depth = w_in.shape[0]
    past = cache_att_k.shape[2]
    n_mem = mem_prompt.shape[1]
    t_attn = 256
    l_prompt = 128

    xp = x_prompt.reshape(bp * sp, D_MODEL)
    xs = x_sample.reshape(bs * ss, D_MODEL)
    mem = mem_prompt.reshape(bp * n_mem, D_MODEL)
    g_final = norm_final.reshape(1, D_MODEL)

    bias_tiles = _prompt_bias_tiles(rel_bias, t_attn)
    rel_s = jnp.arange(past + ss, dtype=jnp.int32)[None, :] - (past + jnp.arange(ss, dtype=jnp.int32))[:, None]
    bias_s = _bias_table(rel_bias, rel_s) * LOG2E
    bias_s_past, bias_s_new = lax.optimization_barrier((bias_s[:, :, :past], bias_s[:, :, past:]))

    zero_state = (jnp.zeros((bp, ML_WIDTH, ML_STATE_W), F32), jnp.zeros((bp, 1, 128), F32))
    zero_conv = jnp.zeros((bp, CONV_K - 1, CONV_CH), F32)

    outs = {k: [] for k in ('pk', 'pv', 'pmk', 'pmv', 'pC', 'pn', 'pm', 'pcv', 'sk', 'sv', 'sC', 'sn', 'sm', 'scv')}
    gate_lo = 3 * ATT_WIDTH + 4 * ML_WIDTH
    for l in range(depth):
        w = w_in[l]
        wts = (norm_mix[l].reshape(1, D_MODEL),
               jnp.concatenate([w[:, :gate_lo], w[:, gate_lo + 2 * H_M:]], axis=1).astype(BF16),
               w[:, gate_lo:gate_lo + 2 * H_M].T.astype(BF16),
               mlstm_gate_bias[l].astype(F32), conv_w[l].astype(F32), lambda_params[l].astype(F32),
               norm_att_heads[l].reshape(1, ATT_WIDTH), norm_mlstm_heads[l].reshape(1, ML_WIDTH),
               w_out[l].astype(BF16), norm_cross[l].reshape(1, D_MODEL),
               w_cq[l].astype(BF16), w_co[l].astype(BF16), norm_ffn[l].reshape(1, D_MODEL),
               w_gate[l].astype(BF16), w_up[l].astype(BF16), w_down[l].astype(BF16))
        final = l == depth - 1

        w_kv = jnp.concatenate([w_ck[l], w_cv[l]], axis=1).astype(BF16)
        mkf, mvf, mkb, mvb = _mem_kv(mem, w_kv, 512)
        attn_p = lambda q, k, v, lam_p, g, li: _attn_prompt(q, k, v, bias_tiles, lam_p, g, bp, sp, t_attn, li)
        xp, kf, vf, c1, n1, m1, cv1 = _layer(
            xp, l, wts, attn_p, mkb.reshape(bp, n_mem, -1), mvb.reshape(bp, n_mem, -1),
            zero_state, zero_conv, g_final, final, bp, sp, 512, l_prompt, 512)
        outs['pk'].append(kf.reshape(bp, sp, H_A, 2 * D_A))
        outs['pv'].append(vf.reshape(bp, sp, H_A, DV_A))
        outs['pmk'].append(mkf.reshape(bp, n_mem, H_X, D_X))
        outs['pmv'].append(mvf.reshape(bp, n_mem, H_X, D_X))
        outs['pC'].append(c1); outs['pn'].append(n1); outs['pm'].append(m1); outs['pcv'].append(cv1)

        k_past = cache_att_k[l].reshape(bs, past, ATT_WIDTH)
        v_past = cache_att_v[l].reshape(bs, past, ATT_WIDTH)
        attn_s = lambda q, k, v, lam_p, g, li: _attn_sample(q, k, v, k_past, v_past, bias_s_past, bias_s_new,
                                                            lam_p, g, bs, ss, li)
        s0 = _block_diag_state(state_mlstm_C[l], state_mlstm_n[l])
        m0 = jnp.pad(state_mlstm_m[l].astype(F32), ((0, 0), (0, 128 - H_M))).reshape(bs, 1, 128)
        xs, kf, vf, c1, n1, m1, cv1 = _layer(
            xs, l, wts, attn_s,
            cache_mem_k[l].reshape(bs, n_mem, -1).astype(BF16), cache_mem_v[l].reshape(bs, n_mem, -1).astype(BF16),
            (s0, m0), state_conv[l].astype(F32), g_final, final, bs, ss, bs * ss, ss, bs * ss)
        outs['sk'].append(kf.reshape(bs, ss, H_A, 2 * D_A))
        outs['sv'].append(vf.reshape(bs, ss, H_A, DV_A))
        outs['sC'].append(c1); outs['sn'].append(n1); outs['sm'].append(m1); outs['scv'].append(cv1)

    st = lambda k: jnp.stack(outs[k])
    return (xp.reshape(bp, sp, D_MODEL), xs.reshape(bs, ss, D_MODEL),
            st('pk'), st('pv'), st('pmk'), st('pmv'), st('pC'), st('pn'), st('pm'), st('pcv'),
            st('sk'), st('sv'), st('sC'), st('sn'), st('sm'), st('scv'))
```

```python
import functools
import math

import jax
import jax.numpy as jnp
from jax import lax
from jax.experimental import pallas as pl
from jax.experimental.pallas import tpu as pltpu

F32 = jnp.float32
BF16 = jnp.bfloat16

D_MODEL = 1024
CHUNK = 64
H_A = 4
D_A = 64
DV_A = 128
ATT_WIDTH = H_A * DV_A
H_M = 4
D_M = 64
ML_WIDTH = H_M * D_M
CONV_CH = 256
CONV_K = 3
H_X = 4
D_X = 256
D_FF = 2816
NUM_BUCKETS = 32
MAX_DISTANCE = 128
EPS = 1e-6

N_PROJ = 3 * ATT_WIDTH + 4 * ML_WIDTH + 3 * CONV_CH
ML_STATE_W = ML_WIDTH + 128

V7X_VMEM_LIMIT = 56 * 1024 * 1024

LOG2E = math.log2(math.e)
RECURRENT_STREAMS = 8
ATTN_HEADS_PER_STEP = 4

_NT = (((1,), (1,)), ((), ()))
_TN = (((0,), (0,)), ((), ()))


def _const_spec(shape):
    nd = len(shape)
    return pl.BlockSpec(shape, lambda *_: (0,) * nd, pipeline_mode=pl.Buffered(1))


def _rms(x, g):
    ms = jnp.mean(x * x, axis=-1, keepdims=True)
    return x * lax.rsqrt(ms + EPS) * g


def _in_proj_kernel(x_ref, g_ref, w_ref, wgt_ref, *refs):
    qa_ref, kf_ref, kb_ref, vf_ref, vb_ref, qkvm_ref, om_ref, bch_ref, gt_ref = refs[-9:]
    xn = _rms(x_ref[...], g_ref[...]).astype(BF16)

    def proj(lo, hi):
        return jnp.dot(xn, w_ref[:, lo:hi], preferred_element_type=F32)

    a = ATT_WIDTH
    qa_ref[...] = (proj(0, a) * (D_A ** -0.5 * LOG2E)).astype(BF16)
    k = proj(a, 2 * a)
    kf_ref[...] = k
    kb_ref[...] = k.astype(BF16)
    v = proj(2 * a, 3 * a)
    vf_ref[...] = v
    vb_ref[...] = v.astype(BF16)
    o = 3 * a
    m = ML_WIDTH
    qkvm_ref[:, 0:m] = proj(o, o + m).astype(BF16)
    qkvm_ref[:, m:2 * m] = (proj(o + m, o + 2 * m) * (D_M ** -0.5)).astype(BF16)
    qkvm_ref[:, 2 * m:3 * m] = proj(o + 2 * m, o + 3 * m).astype(BF16)
    om_ref[...] = proj(o + 3 * m, o + 4 * m)
    bch_ref[...] = proj(o + 4 * m, N_PROJ)
    gt_ref[...] = lax.dot_general(wgt_ref[...], xn, _NT, preferred_element_type=F32)


def _in_proj(x, g, w, wgt, tm, layer, depth, kv_all):
    t = x.shape[0]
    a, m = ATT_WIDTH, ML_WIDTH
    row = lambda n: pl.BlockSpec((tm, n), lambda i: (i, 0))
    slab = pl.BlockSpec((None, tm, a), lambda i: (layer, i, 0))
    outs = [
        (jax.ShapeDtypeStruct((t, a), BF16), row(a)),
        (jax.ShapeDtypeStruct((depth, t, a), F32), slab),
        (jax.ShapeDtypeStruct((t, a), BF16), row(a)),
        (jax.ShapeDtypeStruct((depth, t, a), F32), slab),
        (jax.ShapeDtypeStruct((t, a), BF16), row(a)),
        (jax.ShapeDtypeStruct((t, 3 * m), BF16), row(3 * m)),
        (jax.ShapeDtypeStruct((t, m), F32), row(m)),
        (jax.ShapeDtypeStruct((t, 3 * CONV_CH), F32), row(3 * CONV_CH)),
        (jax.ShapeDtypeStruct((8, t), F32), pl.BlockSpec((8, tm), lambda i: (0, i))),
    ]
    carried = [] if kv_all is None else list(kv_all)
    return pl.pallas_call(
        _in_proj_kernel,
        grid=(t // tm,),
        in_specs=[row(D_MODEL), _const_spec((1, D_MODEL)), _const_spec((D_MODEL, N_PROJ)),
                  _const_spec((8, D_MODEL))] + [pl.BlockSpec(memory_space=pl.ANY)] * len(carried),
        out_specs=[s for _, s in outs],
        out_shape=[s for s, _ in outs],
        input_output_aliases={4: 1, 5: 3} if carried else {},
        compiler_params=pltpu.CompilerParams(dimension_semantics=("parallel",),
                                             vmem_limit_bytes=V7X_VMEM_LIMIT),
        name="in_proj",
    )(x, g, w, wgt, *carried)


def _lambda_value(lp, lam_init):
    a = jnp.sum(lp[0:1] * lp[1:2], axis=1, keepdims=True)
    b = jnp.sum(lp[2:3] * lp[3:4], axis=1, keepdims=True)
    return jnp.exp(a) - jnp.exp(b) + lam_init


def _stack_maps(q):
    lane = lax.broadcasted_iota(jnp.int32, q.shape, 1)
    zero = jnp.zeros_like(q)
    return jnp.concatenate([jnp.where(lane < D_A, q, zero), jnp.where(lane >= D_A, q, zero)], axis=0)


def _attn_finish(acc, lam, g, lam_init, t):
    o = acc[:, :DV_A] / acc[:, DV_A:]
    d = o[:t] - lam * o[t:]
    return (_rms(d, g) * (1.0 - lam_init)).astype(BF16)


def _with_ones(v):
    return jnp.concatenate([v, jnp.ones_like(v)], axis=1)


def _attn_prompt_kernel(lam_ref, g_ref, q_ref, k_ref, v_ref, bias_ref, o_ref,
                        m_scr, l_scr, acc_scr, sa_scr, sb_scr, p_scr, *, t, hp, lam_init):
    i = pl.program_id(2)
    heads = [slice(hh * DV_A, (hh + 1) * DV_A) for hh in range(hp)]
    cols = [slice(hh * 2 * t, (hh + 1) * 2 * t) for hh in range(hp)]
    qz = [_stack_maps(q_ref[:, hd]) for hd in heads]

    def rows(kb):
        return pl.ds(pl.multiple_of(kb * t, t), t)

    def qk(kb):
        r = rows(kb)
        return jnp.concatenate([lax.dot_general(k_ref[r, hd], z, _NT, preferred_element_type=F32)
                                for hd, z in zip(heads, qz)], axis=1)

    def pv(p, kb):
        r = rows(kb)
        return jnp.concatenate([lax.dot_general(v_ref[r, hd], p[:, c], _TN, preferred_element_type=F32)
                                for hd, c in zip(heads, cols)], axis=1)

    def bias(lo):
        return jnp.concatenate([bias_ref[hh, 0, lo:lo + t, :] for hh in range(hp)], axis=1)

    def softmax_step(m_prev, s):
        m_new = jnp.maximum(m_prev, jnp.max(s, axis=0, keepdims=True))
        p = jnp.exp2(s - m_new)
        return m_new, jnp.exp2(m_prev - m_new), p, jnp.sum(p, axis=0, keepdims=True)

    n_far = jnp.maximum(i - 1, 0)
    last = n_far - 1
    kb_prev = jnp.maximum(i - 1, 0)

    s_diag = qk(i)
    s_prev = qk(kb_prev)
    sa_scr[...] = qk(0)
    s_diag = s_diag + bias(t)
    m = jnp.max(s_diag, axis=0, keepdims=True)
    p = jnp.exp2(s_diag - m)
    l = jnp.sum(p, axis=0, keepdims=True)
    acc = pv(p.astype(BF16), i)
    w_prev = (i >= 1).astype(F32)
    m, alpha, p, p_sum = softmax_step(m, s_prev + bias(0))
    m_scr[...] = m
    l_scr[...] = alpha * l + w_prev * p_sum
    acc_scr[...] = alpha * acc
    p_scr[...] = p.astype(BF16)

    def far_pair(j, carry):
        kb1 = jnp.minimum(2 * j + 1, last)
        w1 = (2 * j + 1 <= last).astype(F32)
        kb_pending = jnp.where(j == 0, kb_prev, 2 * j - 1)
        sb_scr[...] = qk(kb1)
        owed = pv(p_scr[...], kb_pending)
        m, alpha, p, p_sum = softmax_step(m_scr[...], sa_scr[...])
        l = alpha * l_scr[...] + p_sum
        acc = alpha * (acc_scr[...] + owed)
        sa_scr[...] = qk(jnp.minimum(2 * j + 2, last))
        owed = pv(p.astype(BF16), 2 * j)
        m, alpha, p, p_sum = softmax_step(m, sb_scr[...])
        m_scr[...] = m
        l_scr[...] = alpha * l + w1 * p_sum
        acc_scr[...] = alpha * (acc + owed)
        p_scr[...] = p.astype(BF16)
        return carry

    trips = (n_far + 1) // 2
    lax.fori_loop(0, trips, far_pair, 0)

    kb_pending = jnp.where(trips == 0, kb_prev, jnp.minimum(2 * trips - 1, last))
    w_pending = jnp.where(trips == 0, w_prev, (n_far % 2 == 0).astype(F32))
    acc = acc_scr[...] + w_pending * pv(p_scr[...], kb_pending)

    lam = _lambda_value(lam_ref[...], lam_init)
    o = acc * (1.0 / l_scr[...])
    for hd, c in zip(heads, cols):
        d = o[:, c][:, :t] - lam * o[:, c][:, t:]
        ms = jnp.mean(d * d, axis=0, keepdims=True)
        y = jnp.transpose(d * lax.rsqrt(ms + EPS))
        o_ref[:, hd] = (y * g_ref[:, hd] * (1.0 - lam_init)).astype(BF16)


def _attn_prompt(q, k, v, bias, lam_p, g_att, batch, seq, t, lam_init):
    nq = seq // t
    hp = ATTN_HEADS_PER_STEP
    w = hp * 2 * t
    kv_spec = pl.BlockSpec((seq, hp * DV_A), lambda b, h, i: (b, h))
    return pl.pallas_call(
        functools.partial(_attn_prompt_kernel, t=t, hp=hp, lam_init=lam_init),
        grid=(batch, H_A // hp, nq),
        in_specs=[_const_spec((4, D_A)),
                  pl.BlockSpec((1, hp * DV_A), lambda b, h, i: (0, h)),
                  pl.BlockSpec((t, hp * DV_A), lambda b, h, i: (b * nq + i, h)),
                  kv_spec, kv_spec,
                  pl.BlockSpec((hp, 1, 2 * t, 2 * t), lambda b, h, i: (h, jnp.minimum(i, 1), 0, 0))],
        out_specs=pl.BlockSpec((t, hp * DV_A), lambda b, h, i: (b * nq + i, h)),
        out_shape=jax.ShapeDtypeStruct((batch * seq, ATT_WIDTH), BF16),
        scratch_shapes=[pltpu.VMEM((1, w), F32), pltpu.VMEM((1, w), F32),
                        pltpu.VMEM((DV_A, w), F32),
                        pltpu.VMEM((t, w), F32), pltpu.VMEM((t, w), F32),
                        pltpu.VMEM((t, w), BF16)],
        compiler_params=pltpu.CompilerParams(dimension_semantics=("parallel", "parallel", "arbitrary"),
                                             vmem_limit_bytes=V7X_VMEM_LIMIT),
        name="attn_prompt",
    )(lam_p, g_att, q, k, v, bias)


def _attn_sample_kernel(lam_ref, g_ref, q_ref, kp_ref, vp_ref, kn_ref, vn_ref, bp_ref, bn_ref, o_ref,
                        *, lq, lam_init):
    qz = _stack_maps(q_ref[...])
    bp = bp_ref[0]
    bn = bn_ref[0]
    sp = lax.dot_general(qz, kp_ref[0].astype(BF16), _NT, preferred_element_type=F32)
    sp = sp + jnp.concatenate([bp, bp], axis=0)
    sn = lax.dot_general(qz, kn_ref[...], _NT, preferred_element_type=F32)
    sn = sn + jnp.concatenate([bn, bn], axis=0)
    m = jnp.maximum(jnp.max(sp, axis=1, keepdims=True), jnp.max(sn, axis=1, keepdims=True))
    pp = jnp.exp2(sp - m)
    pn = jnp.exp2(sn - m)
    acc = (jnp.dot(pp.astype(BF16), _with_ones(vp_ref[0].astype(BF16)), preferred_element_type=F32)
           + jnp.dot(pn.astype(BF16), _with_ones(vn_ref[...]), preferred_element_type=F32))
    lam = _lambda_value(lam_ref[...], lam_init)
    o_ref[...] = _attn_finish(acc, lam, g_ref[...], lam_init, lq)


def _attn_sample(q, k_new, v_new, k_past, v_past, bias_past, bias_new, lam_p, g_att, batch, lq, lam_init):
    past = k_past.shape[1]
    new_spec = pl.BlockSpec((lq, DV_A), lambda b, h: (b, h))
    past_spec = pl.BlockSpec((1, past, DV_A), lambda b, h: (b, 0, h))
    return pl.pallas_call(
        functools.partial(_attn_sample_kernel, lq=lq, lam_init=lam_init),
        grid=(batch, H_A),
        in_specs=[_const_spec((4, D_A)),
                  pl.BlockSpec((1, DV_A), lambda b, h: (0, h)),
                  new_spec, past_spec, past_spec, new_spec, new_spec,
                  pl.BlockSpec((1, lq, past), lambda b, h: (h, 0, 0)),
                  pl.BlockSpec((1, lq, lq), lambda b, h: (h, 0, 0))],
        out_specs=new_spec,
        out_shape=jax.ShapeDtypeStruct((batch * lq, ATT_WIDTH), BF16),
        compiler_params=pltpu.CompilerParams(dimension_semantics=("parallel", "parallel"),
                                             vmem_limit_bytes=V7X_VMEM_LIMIT),
        name="attn_sample",
    )(lam_p, g_att, q, k_past, v_past, k_new, v_new, bias_past, bias_new)


def _per_stream(fn, *arrays):
    return jnp.stack([fn(*(a[g] for a in arrays)) for g in range(arrays[0].shape[0])])


def _recurrent_kernel(gb_ref, q_ref, k_ref, v_ref, og_ref, gt_ref, b_ref, c_ref, hc_ref, cw_ref, gml_ref,
                      s0_ref, m0_ref, cv0_ref,
                      hm_ref, oc_ref, s_ref, m_ref, cv_ref, *, L):
    @pl.when(pl.program_id(1) == 0)
    def _():
        s_ref[...] = s0_ref[...]
        m_ref[...] = m0_ref[...]
        cv_ref[...] = cv0_ref[...]

    q = q_ref[...]
    k = k_ref[...]
    v = v_ref[...]
    G = q.shape[0]
    state = s_ref[...]
    q_state = _per_stream(lambda a, b: jnp.dot(a, b, preferred_element_type=F32),
                          q, state.astype(BF16))
    q_c = q_state[:, :, :ML_WIDTH]

    lane_head = lax.broadcasted_iota(jnp.int32, (1, L, ML_WIDTH), 2) // D_M
    row = lax.broadcasted_iota(jnp.int32, (1, L, L), 1)
    col = lax.broadcasted_iota(jnp.int32, (1, L, L), 2)
    causal = col <= row
    eye = col == row
    st_lane = lax.broadcasted_iota(jnp.int32, (1, 1, ML_STATE_W), 2)
    st_lane_head = jnp.where(st_lane < ML_WIDTH, st_lane // D_M, st_lane - ML_WIDTH)
    m_lane = lax.broadcasted_iota(jnp.int32, (1, 1, 128), 2)

    gt = gt_ref[:, 0]
    m_all = m_ref[...]
    num = jnp.zeros((G, L, ML_WIDTH), F32)
    w_state = jnp.zeros((G, L, ML_WIDTH), F32)
    decay = jnp.zeros((G, 1, ML_STATE_W), F32)
    m_next = jnp.zeros((G, 1, 128), F32)
    for h in range(H_M):
        ig = gt[:, h:h + 1, :] + gb_ref[0, h]
        fz = gt[:, H_M + h:H_M + h + 1, :] + gb_ref[1, h]
        lf = jnp.minimum(fz, 0.0) - jnp.log1p(jnp.exp(-jnp.abs(fz)))
        f_col = jnp.sum(jnp.where(causal, lf, 0.0), axis=2, keepdims=True)
        f_row = jnp.sum(jnp.where(eye, f_col, 0.0), axis=1, keepdims=True)
        d = jnp.where(causal, f_col + (ig - f_row), -jnp.inf)
        m_prev = m_all[:, :, h:h + 1]
        inter = f_col + m_prev
        mt = jnp.maximum(inter, jnp.max(d, axis=2, keepdims=True))
        w_intra = jnp.exp(d - mt)
        w_inter = jnp.exp(inter - mt)
        head = lane_head == h
        qk = _per_stream(lambda a, b: lax.dot_general(a, b, _NT, preferred_element_type=F32),
                         jnp.where(head, q, jnp.zeros_like(q)), k)
        s = qk * w_intra
        den = jnp.sum(s, axis=2, keepdims=True) + w_inter * q_state[:, :, ML_WIDTH + h:ML_WIDTH + h + 1]
        inv = 1.0 / jnp.maximum(jnp.abs(den), jnp.exp(-mt))
        sv = _per_stream(lambda a, b: jnp.dot(a, b, preferred_element_type=F32), s.astype(BF16), v)
        num = jnp.where(head, (sv + w_inter * q_c) * inv, num)
        m_new = mt[:, L - 1:L, :]
        w_col = jnp.sum(jnp.where(eye, w_intra[:, L - 1:L, :], 0.0), axis=2, keepdims=True)
        w_state = jnp.where(head, w_col, w_state)
        dec = jnp.exp(f_col[:, L - 1:L, :] + m_prev - m_new)
        decay = jnp.where(st_lane_head == h, dec, decay)
        m_next = jnp.where(m_lane == h, m_new, m_next)

    kw = (k.astype(F32) * w_state).astype(BF16)
    v_aug = jnp.concatenate([v, jnp.ones((G, L, ML_STATE_W - ML_WIDTH), BF16)], axis=2)
    upd = _per_stream(lambda a, b: lax.dot_general(a, b, _TN, preferred_element_type=F32), kw, v_aug)
    s_row_head = lax.broadcasted_iota(jnp.int32, (1, ML_WIDTH, ML_STATE_W), 1) // D_M
    s_col = lax.broadcasted_iota(jnp.int32, (1, ML_WIDTH, ML_STATE_W), 2)
    s_col_head = jnp.where(s_col < ML_WIDTH, s_col // D_M, s_col - ML_WIDTH)
    s_ref[...] = decay * state + jnp.where(s_row_head == s_col_head, upd, 0.0)
    m_ref[...] = m_next

    hg = num * jax.nn.sigmoid(og_ref[...])
    scale = jnp.zeros((G, L, ML_WIDTH), F32)
    for h in range(H_M):
        head = lane_head == h
        ss = jnp.sum(jnp.where(head, hg * hg, 0.0), axis=2, keepdims=True)
        scale = jnp.where(head, lax.rsqrt(ss * (1.0 / D_M) + EPS), scale)
    hm_ref[...] = (hg * scale * gml_ref[...]).astype(BF16)

    u = c_ref[...] * hc_ref[...]
    prev = cv_ref[...]
    ri = lax.broadcasted_iota(jnp.int32, (1, L, CONV_CH), 1)
    u1 = jnp.where(ri == 0, prev[:, 1:2], _per_stream(lambda a: pltpu.roll(a, 1, 0), u))
    u2 = jnp.where(ri == 0, prev[:, 0:1],
                   jnp.where(ri == 1, prev[:, 1:2], _per_stream(lambda a: pltpu.roll(a, 2, 0), u)))
    w = cw_ref[...]
    oc_ref[...] = (b_ref[...] * (w[0:1] * u2 + w[1:2] * u1 + w[2:3] * u)).astype(BF16)
    cv_ref[:, 0:1, :] = u[:, L - 2:L - 1]
    cv_ref[:, 1:2, :] = u[:, L - 1:L]


def _recurrent(gate_b, qkvm, og, gt, bch, conv_w, g_ml, s0, m0, cv0, batch, seq, L, G):
    assert batch % G == 0 and seq % L == 0
    nc = seq // L
    tok = lambda j: pl.BlockSpec((G, L, ML_WIDTH), lambda b, c: (b, c, j))
    per_b = lambda shape: pl.BlockSpec((G,) + shape, lambda b, c: (b, 0, 0))
    return pl.pallas_call(
        functools.partial(_recurrent_kernel, L=L),
        grid=(batch // G, nc),
        in_specs=[pl.BlockSpec(memory_space=pltpu.SMEM),
                  tok(0), tok(1), tok(2), tok(0),
                  pl.BlockSpec((G, 1, 8, L), lambda b, c: (b, c, 0, 0)),
                  tok(0), tok(1), tok(2),
                  _const_spec((CONV_K, CONV_CH)), _const_spec((1, ML_WIDTH)),
                  per_b((ML_WIDTH, ML_STATE_W)), per_b((1, 128)), per_b((CONV_K - 1, CONV_CH))],
        out_specs=[tok(0), tok(0),
                   per_b((ML_WIDTH, ML_STATE_W)), per_b((1, 128)), per_b((CONV_K - 1, CONV_CH))],
        out_shape=[jax.ShapeDtypeStruct((batch, seq, ML_WIDTH), BF16),
                   jax.ShapeDtypeStruct((batch, seq, CONV_CH), BF16),
                   jax.ShapeDtypeStruct((batch, ML_WIDTH, ML_STATE_W), F32),
                   jax.ShapeDtypeStruct((batch, 1, 128), F32),
                   jax.ShapeDtypeStruct((batch, CONV_K - 1, CONV_CH), F32)],
        compiler_params=pltpu.CompilerParams(dimension_semantics=("parallel", "arbitrary"),
                                             vmem_limit_bytes=V7X_VMEM_LIMIT),
        name="recurrent",
    )(gate_b, qkvm, qkvm, qkvm, og, gt, bch, bch, bch, conv_w, g_ml, s0, m0, cv0)


def _mem_kv_kernel(x_ref, w_ref, kf_ref, vf_ref, kb_ref, vb_ref):
    x = x_ref[...].astype(BF16)
    n = H_X * D_X
    k = jnp.dot(x, w_ref[:, :n], preferred_element_type=F32)
    v = jnp.dot(x, w_ref[:, n:], preferred_element_type=F32)
    kf_ref[...] = k
    vf_ref[...] = v
    kb_ref[...] = k.astype(BF16)
    vb_ref[...] = v.astype(BF16)


def _mem_kv(mem, w_kv, tm):
    t = mem.shape[0]
    n = H_X * D_X
    row = pl.BlockSpec((tm, n), lambda i: (i, 0))
    return pl.pallas_call(
        _mem_kv_kernel,
        grid=(t // tm,),
        in_specs=[pl.BlockSpec((tm, D_MODEL), lambda i: (i, 0)), _const_spec((D_MODEL, 2 * n))],
        out_specs=[row, row, row, row],
        out_shape=[jax.ShapeDtypeStruct((t, n), F32), jax.ShapeDtypeStruct((t, n), F32),
                   jax.ShapeDtypeStruct((t, n), BF16), jax.ShapeDtypeStruct((t, n), BF16)],
        compiler_params=pltpu.CompilerParams(dimension_semantics=("parallel",),
                                             vmem_limit_bytes=V7X_VMEM_LIMIT),
        name="mem_kv",
    )(mem, w_kv)


FF_CHUNK = 256


def _post_kernel(x_ref, oa_ref, hm_ref, oc_ref, mk_ref, mv_ref, wout_ref, gc_ref, wcq_ref, wco_ref,
                 gf_ref, wg_ref, wu_ref, wd_ref, gfin_ref, o_ref, *, final):
    a, m = ATT_WIDTH, ML_WIDTH
    x = x_ref[...]
    x = x + (jnp.dot(oa_ref[...], wout_ref[0:a, :], preferred_element_type=F32)
             + jnp.dot(hm_ref[...], wout_ref[a:a + m, :], preferred_element_type=F32)
             + jnp.dot(oc_ref[...], wout_ref[a + m:, :], preferred_element_type=F32))

    xn = _rms(x, gc_ref[...]).astype(BF16)
    qc = (jnp.dot(xn, wcq_ref[...], preferred_element_type=F32) * (D_X ** -0.5)).astype(BF16)
    cross = jnp.zeros_like(x)
    n_streams = mk_ref.shape[0]
    rows = x.shape[0] // n_streams
    for h in range(H_X):
        sl = slice(h * D_X, (h + 1) * D_X)
        outs = []
        for b in range(n_streams):
            r = slice(b * rows, (b + 1) * rows)
            s = lax.dot_general(qc[r, sl], mk_ref[b, :, sl], _NT, preferred_element_type=F32)
            p = jnp.exp(s - jnp.max(s, axis=1, keepdims=True))
            p = p * (1.0 / jnp.sum(p, axis=1, keepdims=True))
            outs.append(jnp.dot(p.astype(BF16), mv_ref[b, :, sl], preferred_element_type=F32).astype(BF16))
        o = outs[0] if n_streams == 1 else jnp.concatenate(outs, axis=0)
        cross = cross + jnp.dot(o, wco_ref[sl, :], preferred_element_type=F32)
    x = x + cross

    xn = _rms(x, gf_ref[...]).astype(BF16)
    ff = jnp.zeros_like(x)
    for j in range(D_FF // FF_CHUNK):
        sl = slice(j * FF_CHUNK, (j + 1) * FF_CHUNK)
        g = jnp.dot(xn, wg_ref[:, sl], preferred_element_type=F32)
        u = jnp.dot(xn, wu_ref[:, sl], preferred_element_type=F32)
        act = (g * jax.nn.sigmoid(g) * u).astype(BF16)
        ff = ff + jnp.dot(act, wd_ref[sl, :], preferred_element_type=F32)
    x = x + ff
    if final:
        x = _rms(x, gfin_ref[...])
    o_ref[...] = x


def _post(x, oa, hm, oc, mk, mv, w_out, g_cross, w_cq, w_co, g_ffn, w_gate, w_up, w_down, g_final,
          tm, seq, final):
    t = x.shape[0]
    row = lambda n: pl.BlockSpec((tm, n), lambda i: (i, 0))
    if tm <= seq:
        tiles_per_stream = seq // tm
        mem = pl.BlockSpec((1,) + mk.shape[1:], lambda i: (i // tiles_per_stream, 0, 0))
    else:
        mem = pl.BlockSpec((tm // seq,) + mk.shape[1:], lambda i: (i, 0, 0))
    vec = _const_spec((1, D_MODEL))
    return pl.pallas_call(
        functools.partial(_post_kernel, final=final),
        grid=(t // tm,),
        in_specs=[row(D_MODEL), row(ATT_WIDTH), row(ML_WIDTH), row(CONV_CH), mem, mem,
                  _const_spec(w_out.shape), vec, _const_spec(w_cq.shape), _const_spec(w_co.shape),
                  vec, _const_spec(w_gate.shape), _const_spec(w_up.shape), _const_spec(w_down.shape), vec],
        out_specs=row(D_MODEL),
        out_shape=jax.ShapeDtypeStruct((t, D_MODEL), F32),
        compiler_params=pltpu.CompilerParams(dimension_semantics=("parallel",),
                                             vmem_limit_bytes=V7X_VMEM_LIMIT),
        name="post",
    )(x, oa, hm, oc, mk, mv, w_out, g_cross, w_cq, w_co, g_ffn, w_gate, w_up, w_down, g_final)


def _rel_bucket(rel):
    half = NUM_BUCKETS // 2
    max_exact = half // 2
    n = jnp.abs(rel)
    large = max_exact + (jnp.log(jnp.maximum(n, 1).astype(F32) / max_exact)
                         / math.log(MAX_DISTANCE / max_exact) * (half - max_exact)).astype(jnp.int32)
    large = jnp.minimum(large, half - 1)
    return jnp.where(rel > 0, half, 0) + jnp.where(n < max_exact, n, large)


def _bias_table(rel_bias, rel):
    bucket = _rel_bucket(rel)[None]
    table = rel_bias.astype(F32)
    out = jnp.zeros((table.shape[1],) + rel.shape, F32)
    for b in range(NUM_BUCKETS):
        out = jnp.where(bucket == b, table[b][:, None, None], out)
    return out


def _prompt_bias_tiles(rel_bias, t):
    key = jnp.arange(t, dtype=jnp.int32)[:, None]
    qry = jnp.arange(t, dtype=jnp.int32)[None, :]
    far = rel_bias[_rel_bucket(jnp.int32(-MAX_DISTANCE))].astype(F32)[:, None, None]
    prev = (_bias_table(rel_bias, key - t - qry) - far) * LOG2E
    diag = jnp.where((key // CHUNK) <= (qry // CHUNK), (_bias_table(rel_bias, key - qry) - far) * LOG2E, -jnp.inf)
    both_maps = lambda a: jnp.concatenate([a, a], axis=2)
    tiles = jnp.stack([jnp.concatenate([diag, diag], axis=1), jnp.concatenate([prev, diag], axis=1)], axis=1)
    return lax.optimization_barrier(both_maps(tiles.reshape(-1, 2 * t, t)).reshape(-1, 2, 2 * t, 2 * t))


def _block_diag_state(c, n):
    b = c.shape[0]
    eye = jnp.eye(H_M, dtype=F32)
    cbd = jnp.einsum('bhdv,hg->bhdgv', c.astype(F32), eye).reshape(b, ML_WIDTH, ML_WIDTH)
    ncol = jnp.einsum('bhd,hg->bhdg', n.astype(F32), eye).reshape(b, ML_WIDTH, H_M)
    pad = jnp.zeros((b, ML_WIDTH, ML_STATE_W - ML_WIDTH - H_M), F32)
    return jnp.concatenate([cbd, ncol, pad], axis=2)


def _unpack_state(s, m):
    b = s.shape[0]
    blocks = s[:, :, :ML_WIDTH].reshape(b, H_M, D_M, H_M, D_M)
    c = jnp.stack([blocks[:, h, :, h, :] for h in range(H_M)], axis=1)
    ncols = s[:, :, ML_WIDTH:ML_WIDTH + H_M].reshape(b, H_M, D_M, H_M)
    n = jnp.stack([ncols[:, h, :, h] for h in range(H_M)], axis=1)
    return c, n, m[:, 0, :H_M]


def _layer(x, layer, depth, kv_all, wts, attn_fn, mem_k, mem_v, ml_state, conv_prev, g_final,
           batch, seq, tm_proj, L, tm_post):
    final = layer == depth - 1
    (g_mix, w_proj, w_gt, gate_b, conv_w, lam_p, g_att, g_ml, w_out,
     g_cross, w_cq, w_co, g_ffn, w_gate, w_up, w_down) = wts
    t = batch * seq
    lam_init = 0.8 - 0.6 * math.exp(-0.3 * layer)
    qa, kf, kb, vf, vb, qkvm, og, bch, gt = _in_proj(x, g_mix, w_proj, w_gt, tm_proj, layer, depth, kv_all)
    oa = attn_fn(qa, kb, vb, lam_p, g_att, lam_init)
    gt = jnp.transpose(gt.reshape(8, batch, seq // L, L), (1, 2, 0, 3))
    tok3 = lambda a: a.reshape(batch, seq, a.shape[-1])
    hm, oc, s1, m1, cv1 = _recurrent(gate_b, tok3(qkvm), tok3(og), gt, tok3(bch), conv_w, g_ml,
                                     ml_state[0], ml_state[1], conv_prev, batch, seq, L,
                                     min(RECURRENT_STREAMS, batch))
    hm, oc = hm.reshape(t, ML_WIDTH), oc.reshape(t, CONV_CH)
    x = _post(x, oa, hm, oc, mem_k, mem_v, w_out, g_cross, w_cq, w_co, g_ffn, w_gate, w_up, w_down,
              g_final, tm_post, seq, final)
    c1, n1, mm1 = _unpack_state(s1, m1)
    return x, (kf, vf), c1, n1, mm1, cv1


def kernel(x_prompt, x_sample, mem_prompt, cache_att_k, cache_att_v, cache_mem_k, cache_mem_v,
           state_mlstm_C, state_mlstm_n, state_mlstm_m, state_conv,
           norm_mix, w_in, mlstm_gate_bias, conv_w, lambda_params, norm_att_heads, norm_mlstm_heads,
           w_out, norm_cross, w_cq, w_ck, w_cv, w_co, norm_ffn, w_gate, w_up, w_down,
           rel_bias, norm_final):
    bp, sp, _ = x_prompt.shape
    bs, ss, _ = x_sample.shape
    depth = w_in.shape[0]
    past = cache_att_k.shape[2]
    n_mem = mem_prompt.shape[1]
    t_attn = 256
    l_prompt = 128

    xp = x_prompt.reshape(bp * sp, D_MODEL)
    xs = x_sample.reshape(bs * ss, D_MODEL)
    mem = mem_prompt.reshape(bp * n_mem, D_MODEL)
    g_final = norm_final.reshape(1, D_MODEL)

    bias_tiles = _prompt_bias_tiles(rel_bias, t_attn)
    rel_s = jnp.arange(past + ss, dtype=jnp.int32)[None, :] - (past + jnp.arange(ss, dtype=jnp.int32))[:, None]
    bias_s = _bias_table(rel_bias, rel_s) * LOG2E
    bias_s_past, bias_s_new = lax.optimization_barrier((bias_s[:, :, :past], bias_s[:, :, past:]))

    zero_state = (jnp.zeros((bp, ML_WIDTH, ML_STATE_W), F32), jnp.zeros((bp, 1, 128), F32))
    zero_conv = jnp.zeros((bp, CONV_K - 1, CONV_CH), F32)

    outs = {k: [] for k in ('pmk', 'pmv', 'pC', 'pn', 'pm', 'pcv', 'sC', 'sn', 'sm', 'scv')}
    kv_p = kv_s = None
    gate_lo = 3 * ATT_WIDTH + 4 * ML_WIDTH
    for l in range(depth):
        w = w_in[l]
        wts = (norm_mix[l].reshape(1, D_MODEL),
               jnp.concatenate([w[:, :gate_lo], w[:, gate_lo + 2 * H_M:]], axis=1).astype(BF16),
               w[:, gate_lo:gate_lo + 2 * H_M].T.astype(BF16),
               mlstm_gate_bias[l].astype(F32), conv_w[l].astype(F32), lambda_params[l].astype(F32),
               norm_att_heads[l].reshape(1, ATT_WIDTH), norm_mlstm_heads[l].reshape(1, ML_WIDTH),
               w_out[l].astype(BF16), norm_cross[l].reshape(1, D_MODEL),
               w_cq[l].astype(BF16), w_co[l].astype(BF16), norm_ffn[l].reshape(1, D_MODEL),
               w_gate[l].astype(BF16), w_up[l].astype(BF16), w_down[l].astype(BF16))

        w_kv = jnp.concatenate([w_ck[l], w_cv[l]], axis=1).astype(BF16)
        mkf, mvf, mkb, mvb = _mem_kv(mem, w_kv, 512)
        attn_p = lambda q, k, v, lam_p, g, li: _attn_prompt(q, k, v, bias_tiles, lam_p, g, bp, sp, t_attn, li)
        xp, kv_p, c1, n1, m1, cv1 = _layer(
            xp, l, depth, kv_p, wts, attn_p, mkb.reshape(bp, n_mem, -1), mvb.reshape(bp, n_mem, -1),
            zero_state, zero_conv, g_final, bp, sp, 512, l_prompt, 512)
        outs['pmk'].append(mkf.reshape(bp, n_mem, H_X, D_X))
        outs['pmv'].append(mvf.reshape(bp, n_mem, H_X, D_X))
        outs['pC'].append(c1); outs['pn'].append(n1); outs['pm'].append(m1); outs['pcv'].append(cv1)

        k_past = cache_att_k[l].reshape(bs, past, ATT_WIDTH)
        v_past = cache_att_v[l].reshape(bs, past, ATT_WIDTH)
        attn_s = lambda q, k, v, lam_p, g, li: _attn_sample(q, k, v, k_past, v_past, bias_s_past, bias_s_new,
                                                            lam_p, g, bs, ss, li)
        s0 = _block_diag_state(state_mlstm_C[l], state_mlstm_n[l])
        m0 = jnp.pad(state_mlstm_m[l].astype(F32), ((0, 0), (0, 128 - H_M))).reshape(bs, 1, 128)
        xs, kv_s, c1, n1, m1, cv1 = _layer(
            xs, l, depth, kv_s, wts, attn_s,
            cache_mem_k[l].reshape(bs, n_mem, -1).astype(BF16), cache_mem_v[l].reshape(bs, n_mem, -1).astype(BF16),
            (s0, m0), state_conv[l].astype(F32), g_final, bs, ss, bs * ss, ss, bs * ss)
        outs['sC'].append(c1); outs['sn'].append(n1); outs['sm'].append(m1); outs['scv'].append(cv1)

    st = lambda k: jnp.stack(outs[k])
    per_head = lambda a, b, s: a.reshape(depth, b, s, H_A, DV_A)
    return (xp.reshape(bp, sp, D_MODEL), xs.reshape(bs, ss, D_MODEL),
            per_head(kv_p[0], bp, sp), per_head(kv_p[1], bp, sp),
            st('pmk'), st('pmv'), st('pC'), st('pn'), st('pm'), st('pcv'),
            per_head(kv_s[0], bs, ss), per_head(kv_s[1], bs, ss),
            st('sC'), st('sn'), st('sm'), st('scv'))
```

```python
import functools
import math

import jax
import jax.numpy as jnp
from jax import lax
from jax.experimental import pallas as pl
from jax.experimental.pallas import tpu as pltpu

F32 = jnp.float32
BF16 = jnp.bfloat16

D_MODEL = 1024
CHUNK = 64
H_A = 4
D_A = 64
DV_A = 128
ATT_WIDTH = H_A * DV_A
H_M = 4
D_M = 64
ML_WIDTH = H_M * D_M
CONV_CH = 256
CONV_K = 3
H_X = 4
D_X = 256
D_FF = 2816
NUM_BUCKETS = 32
MAX_DISTANCE = 128
EPS = 1e-6

N_PROJ = 3 * ATT_WIDTH + 4 * ML_WIDTH + 3 * CONV_CH
ML_STATE_W = ML_WIDTH + 128

V7X_VMEM_LIMIT = 56 * 1024 * 1024

LOG2E = math.log2(math.e)
RECURRENT_STREAMS = 8
ATTN_HEADS_PER_STEP = 4

_NT = (((1,), (1,)), ((), ()))
_TN = (((0,), (0,)), ((), ()))


def _const_spec(shape):
    nd = len(shape)
    return pl.BlockSpec(shape, lambda *_: (0,) * nd, pipeline_mode=pl.Buffered(1))


def _rms(x, g):
    ms = jnp.mean(x * x, axis=-1, keepdims=True)
    return x * lax.rsqrt(ms + EPS) * g


def _in_proj_kernel(x_ref, g_ref, w_ref, wgt_ref, *refs):
    qa_ref, kf_ref, kb_ref, vf_ref, vb_ref, qkvm_ref, om_ref, bch_ref, gt_ref = refs[-9:]
    xn = _rms(x_ref[...], g_ref[...]).astype(BF16)

    def proj(lo, hi):
        return jnp.dot(xn, w_ref[:, lo:hi], preferred_element_type=F32)

    a = ATT_WIDTH
    qa_ref[...] = (proj(0, a) * (D_A ** -0.5 * LOG2E)).astype(BF16)
    k = proj(a, 2 * a)
    kf_ref[...] = k
    kb_ref[...] = k.astype(BF16)
    v = proj(2 * a, 3 * a)
    vf_ref[...] = v
    vb_ref[...] = v.astype(BF16)
    o = 3 * a
    m = ML_WIDTH
    qkvm_ref[:, 0:m] = proj(o, o + m).astype(BF16)
    qkvm_ref[:, m:2 * m] = (proj(o + m, o + 2 * m) * (D_M ** -0.5)).astype(BF16)
    qkvm_ref[:, 2 * m:3 * m] = proj(o + 2 * m, o + 3 * m).astype(BF16)
    om_ref[...] = proj(o + 3 * m, o + 4 * m)
    bch_ref[...] = proj(o + 4 * m, N_PROJ)
    gt_ref[...] = lax.dot_general(wgt_ref[...], xn, _NT, preferred_element_type=F32)


def _in_proj(x, g, w, wgt, tm, layer, depth, kv_all):
    t = x.shape[0]
    a, m = ATT_WIDTH, ML_WIDTH
    row = lambda n: pl.BlockSpec((tm, n), lambda i: (i, 0))
    slab = pl.BlockSpec((None, tm, a), lambda i: (layer, i, 0))
    outs = [
        (jax.ShapeDtypeStruct((t, a), BF16), row(a)),
        (jax.ShapeDtypeStruct((depth, t, a), F32), slab),
        (jax.ShapeDtypeStruct((t, a), BF16), row(a)),
        (jax.ShapeDtypeStruct((depth, t, a), F32), slab),
        (jax.ShapeDtypeStruct((t, a), BF16), row(a)),
        (jax.ShapeDtypeStruct((t, 3 * m), BF16), row(3 * m)),
        (jax.ShapeDtypeStruct((t, m), F32), row(m)),
        (jax.ShapeDtypeStruct((t, 3 * CONV_CH), F32), row(3 * CONV_CH)),
        (jax.ShapeDtypeStruct((8, t), F32), pl.BlockSpec((8, tm), lambda i: (0, i))),
    ]
    carried = [] if kv_all is None else list(kv_all)
    return pl.pallas_call(
        _in_proj_kernel,
        grid=(t // tm,),
        in_specs=[row(D_MODEL), _const_spec((1, D_MODEL)), _const_spec((D_MODEL, N_PROJ)),
                  _const_spec((8, D_MODEL))] + [pl.BlockSpec(memory_space=pl.ANY)] * len(carried),
        out_specs=[s for _, s in outs],
        out_shape=[s for s, _ in outs],
        input_output_aliases={4: 1, 5: 3} if carried else {},
        compiler_params=pltpu.CompilerParams(dimension_semantics=("parallel",),
                                             vmem_limit_bytes=V7X_VMEM_LIMIT),
        name="in_proj",
    )(x, g, w, wgt, *carried)


def _lambda_value(lp, lam_init):
    a = jnp.sum(lp[0:1] * lp[1:2], axis=1, keepdims=True)
    b = jnp.sum(lp[2:3] * lp[3:4], axis=1, keepdims=True)
    return jnp.exp(a) - jnp.exp(b) + lam_init


def _stack_maps(q):
    lane = lax.broadcasted_iota(jnp.int32, q.shape, 1)
    zero = jnp.zeros_like(q)
    return jnp.concatenate([jnp.where(lane < D_A, q, zero), jnp.where(lane >= D_A, q, zero)], axis=0)


def _attn_finish(acc, lam, g, lam_init, t):
    o = acc[:, :DV_A] / acc[:, DV_A:]
    d = o[:t] - lam * o[t:]
    return (_rms(d, g) * (1.0 - lam_init)).astype(BF16)


def _with_ones(v):
    return jnp.concatenate([v, jnp.ones_like(v)], axis=1)


def _attn_prompt_kernel(lam_ref, g_ref, q_ref, k_ref, v_ref, bias_ref, o_ref,
                        m_scr, l_scr, acc_scr, sa_scr, sb_scr, p_scr, *, t, hp, lam_init):
    i = pl.program_id(2)
    heads = [slice(hh * DV_A, (hh + 1) * DV_A) for hh in range(hp)]
    cols = [slice(hh * 2 * t, (hh + 1) * 2 * t) for hh in range(hp)]
    qz = [_stack_maps(q_ref[:, hd]) for hd in heads]

    def rows(kb):
        return pl.ds(pl.multiple_of(kb * t, t), t)

    def qk(kb):
        r = rows(kb)
        return jnp.concatenate([lax.dot_general(k_ref[r, hd], z, _NT, preferred_element_type=F32)
                                for hd, z in zip(heads, qz)], axis=1)

    def pv(p, kb):
        r = rows(kb)
        return jnp.concatenate([lax.dot_general(v_ref[r, hd], p[:, c], _TN, preferred_element_type=F32)
                                for hd, c in zip(heads, cols)], axis=1)

    def bias(lo):
        return jnp.concatenate([bias_ref[hh, 0, lo:lo + t, :] for hh in range(hp)], axis=1)

    def softmax_step(m_prev, s):
        m_new = jnp.maximum(m_prev, jnp.max(s, axis=0, keepdims=True))
        p = jnp.exp2(s - m_new)
        return m_new, jnp.exp2(m_prev - m_new), p, jnp.sum(p, axis=0, keepdims=True)

    n_far = jnp.maximum(i - 1, 0)
    last = n_far - 1
    kb_prev = jnp.maximum(i - 1, 0)

    s_diag = qk(i)
    s_prev = qk(kb_prev)
    sa_scr[...] = qk(0)
    s_diag = s_diag + bias(t)
    m = jnp.max(s_diag, axis=0, keepdims=True)
    p = jnp.exp2(s_diag - m)
    l = jnp.sum(p, axis=0, keepdims=True)
    acc = pv(p.astype(BF16), i)
    w_prev = (i >= 1).astype(F32)
    m, alpha, p, p_sum = softmax_step(m, s_prev + bias(0))
    m_scr[...] = m
    l_scr[...] = alpha * l + w_prev * p_sum
    acc_scr[...] = alpha * acc
    p_scr[...] = p.astype(BF16)

    def far_pair(j, carry):
        kb1 = jnp.minimum(2 * j + 1, last)
        w1 = (2 * j + 1 <= last).astype(F32)
        kb_pending = jnp.where(j == 0, kb_prev, 2 * j - 1)
        sb_scr[...] = qk(kb1)
        owed = pv(p_scr[...], kb_pending)
        m, alpha, p, p_sum = softmax_step(m_scr[...], sa_scr[...])
        l = alpha * l_scr[...] + p_sum
        acc = alpha * (acc_scr[...] + owed)
        sa_scr[...] = qk(jnp.minimum(2 * j + 2, last))
        owed = pv(p.astype(BF16), 2 * j)
        m, alpha, p, p_sum = softmax_step(m, sb_scr[...])
        m_scr[...] = m
        l_scr[...] = alpha * l + w1 * p_sum
        acc_scr[...] = alpha * (acc + owed)
        p_scr[...] = p.astype(BF16)
        return carry

    trips = (n_far + 1) // 2
    lax.fori_loop(0, trips, far_pair, 0)

    kb_pending = jnp.where(trips == 0, kb_prev, jnp.minimum(2 * trips - 1, last))
    w_pending = jnp.where(trips == 0, w_prev, (n_far % 2 == 0).astype(F32))
    acc = acc_scr[...] + w_pending * pv(p_scr[...], kb_pending)

    lam = _lambda_value(lam_ref[...], lam_init)
    o = acc * (1.0 / l_scr[...])
    for hd, c in zip(heads, cols):
        d = o[:, c][:, :t] - lam * o[:, c][:, t:]
        ms = jnp.mean(d * d, axis=0, keepdims=True)
        y = jnp.transpose(d * lax.rsqrt(ms + EPS))
        o_ref[:, hd] = (y * g_ref[:, hd] * (1.0 - lam_init)).astype(BF16)


def _attn_prompt(q, k, v, bias, lam_p, g_att, batch, seq, t, lam_init):
    nq = seq // t
    hp = ATTN_HEADS_PER_STEP
    w = hp * 2 * t
    kv_spec = pl.BlockSpec((seq, hp * DV_A), lambda b, h, i: (b, h))
    return pl.pallas_call(
        functools.partial(_attn_prompt_kernel, t=t, hp=hp, lam_init=lam_init),
        grid=(batch, H_A // hp, nq),
        in_specs=[_const_spec((4, D_A)),
                  pl.BlockSpec((1, hp * DV_A), lambda b, h, i: (0, h)),
                  pl.BlockSpec((t, hp * DV_A), lambda b, h, i: (b * nq + i, h)),
                  kv_spec, kv_spec,
                  pl.BlockSpec((hp, 1, 2 * t, 2 * t), lambda b, h, i: (h, jnp.minimum(i, 1), 0, 0))],
        out_specs=pl.BlockSpec((t, hp * DV_A), lambda b, h, i: (b * nq + i, h)),
        out_shape=jax.ShapeDtypeStruct((batch * seq, ATT_WIDTH), BF16),
        scratch_shapes=[pltpu.VMEM((1, w), F32), pltpu.VMEM((1, w), F32),
                        pltpu.VMEM((DV_A, w), F32),
                        pltpu.VMEM((t, w), F32), pltpu.VMEM((t, w), F32),
                        pltpu.VMEM((t, w), BF16)],
        compiler_params=pltpu.CompilerParams(dimension_semantics=("parallel", "parallel", "arbitrary"),
                                             vmem_limit_bytes=V7X_VMEM_LIMIT),
        name="attn_prompt",
    )(lam_p, g_att, q, k, v, bias)


def _attn_sample_kernel(lam_ref, g_ref, q_ref, kp_ref, vp_ref, kn_ref, vn_ref, bp_ref, bn_ref, o_ref,
                        *, lq, lam_init):
    qz = _stack_maps(q_ref[...])
    bp = bp_ref[0]
    bn = bn_ref[0]
    sp = lax.dot_general(qz, kp_ref[0].astype(BF16), _NT, preferred_element_type=F32)
    sp = sp + jnp.concatenate([bp, bp], axis=0)
    sn = lax.dot_general(qz, kn_ref[...], _NT, preferred_element_type=F32)
    sn = sn + jnp.concatenate([bn, bn], axis=0)
    m = jnp.maximum(jnp.max(sp, axis=1, keepdims=True), jnp.max(sn, axis=1, keepdims=True))
    pp = jnp.exp2(sp - m)
    pn = jnp.exp2(sn - m)
    acc = (jnp.dot(pp.astype(BF16), _with_ones(vp_ref[0].astype(BF16)), preferred_element_type=F32)
           + jnp.dot(pn.astype(BF16), _with_ones(vn_ref[...]), preferred_element_type=F32))
    lam = _lambda_value(lam_ref[...], lam_init)
    o_ref[...] = _attn_finish(acc, lam, g_ref[...], lam_init, lq)


def _attn_sample(q, k_new, v_new, k_past, v_past, bias_past, bias_new, lam_p, g_att, batch, lq, lam_init):
    past = k_past.shape[1]
    new_spec = pl.BlockSpec((lq, DV_A), lambda b, h: (b, h))
    past_spec = pl.BlockSpec((1, past, DV_A), lambda b, h: (b, 0, h))
    return pl.pallas_call(
        functools.partial(_attn_sample_kernel, lq=lq, lam_init=lam_init),
        grid=(batch, H_A),
        in_specs=[_const_spec((4, D_A)),
                  pl.BlockSpec((1, DV_A), lambda b, h: (0, h)),
                  new_spec, past_spec, past_spec, new_spec, new_spec,
                  pl.BlockSpec((1, lq, past), lambda b, h: (h, 0, 0)),
                  pl.BlockSpec((1, lq, lq), lambda b, h: (h, 0, 0))],
        out_specs=new_spec,
        out_shape=jax.ShapeDtypeStruct((batch * lq, ATT_WIDTH), BF16),
        compiler_params=pltpu.CompilerParams(dimension_semantics=("parallel", "parallel"),
                                             vmem_limit_bytes=V7X_VMEM_LIMIT),
        name="attn_sample",
    )(lam_p, g_att, q, k_past, v_past, k_new, v_new, bias_past, bias_new)


def _per_stream(fn, *arrays):
    return jnp.stack([fn(*(a[g] for a in arrays)) for g in range(arrays[0].shape[0])])


def _recurrent_kernel(gb_ref, q_ref, k_ref, v_ref, og_ref, gt_ref, b_ref, c_ref, hc_ref, cw_ref, gml_ref,
                      s0_ref, m0_ref, cv0_ref,
                      hm_ref, oc_ref, s_ref, m_ref, cv_ref, *, L):
    @pl.when(pl.program_id(1) == 0)
    def _():
        s_ref[...] = s0_ref[...]
        m_ref[...] = m0_ref[...]
        cv_ref[...] = cv0_ref[...]

    q = q_ref[...]
    k = k_ref[...]
    v = v_ref[...]
    G = q.shape[0]
    state = s_ref[...]
    q_state = _per_stream(lambda a, b: jnp.dot(a, b, preferred_element_type=F32),
                          q, state.astype(BF16))
    q_c = q_state[:, :, :ML_WIDTH]

    lane_head = lax.broadcasted_iota(jnp.int32, (1, L, ML_WIDTH), 2) // D_M
    row = lax.broadcasted_iota(jnp.int32, (1, L, L), 1)
    col = lax.broadcasted_iota(jnp.int32, (1, L, L), 2)
    causal = col <= row
    eye = col == row
    st_lane = lax.broadcasted_iota(jnp.int32, (1, 1, ML_STATE_W), 2)
    st_lane_head = jnp.where(st_lane < ML_WIDTH, st_lane // D_M, st_lane - ML_WIDTH)
    m_lane = lax.broadcasted_iota(jnp.int32, (1, 1, 128), 2)

    gt = gt_ref[:, 0]
    m_all = m_ref[...]
    num = jnp.zeros((G, L, ML_WIDTH), F32)
    w_state = jnp.zeros((G, L, ML_WIDTH), F32)
    decay = jnp.zeros((G, 1, ML_STATE_W), F32)
    m_next = jnp.zeros((G, 1, 128), F32)
    for h in range(H_M):
        ig = gt[:, h:h + 1, :] + gb_ref[0, h]
        fz = gt[:, H_M + h:H_M + h + 1, :] + gb_ref[1, h]
        lf = jnp.minimum(fz, 0.0) - jnp.log1p(jnp.exp(-jnp.abs(fz)))
        f_col = jnp.sum(jnp.where(causal, lf, 0.0), axis=2, keepdims=True)
        f_row = jnp.sum(jnp.where(eye, f_col, 0.0), axis=1, keepdims=True)
        d = jnp.where(causal, f_col + (ig - f_row), -jnp.inf)
        m_prev = m_all[:, :, h:h + 1]
        inter = f_col + m_prev
        mt = jnp.maximum(inter, jnp.max(d, axis=2, keepdims=True))
        w_intra = jnp.exp(d - mt)
        w_inter = jnp.exp(inter - mt)
        head = lane_head == h
        qk = _per_stream(lambda a, b: lax.dot_general(a, b, _NT, preferred_element_type=F32),
                         jnp.where(head, q, jnp.zeros_like(q)), k)
        s = qk * w_intra
        den = jnp.sum(s, axis=2, keepdims=True) + w_inter * q_state[:, :, ML_WIDTH + h:ML_WIDTH + h + 1]
        inv = 1.0 / jnp.maximum(jnp.abs(den), jnp.exp(-mt))
        sv = _per_stream(lambda a, b: jnp.dot(a, b, preferred_element_type=F32), s.astype(BF16), v)
        num = jnp.where(head, (sv + w_inter * q_c) * inv, num)
        m_new = mt[:, L - 1:L, :]
        w_col = jnp.sum(jnp.where(eye, w_intra[:, L - 1:L, :], 0.0), axis=2, keepdims=True)
        w_state = jnp.where(head, w_col, w_state)
        dec = jnp.exp(f_col[:, L - 1:L, :] + m_prev - m_new)
        decay = jnp.where(st_lane_head == h, dec, decay)
        m_next = jnp.where(m_lane == h, m_new, m_next)

    kw = (k.astype(F32) * w_state).astype(BF16)
    v_aug = jnp.concatenate([v, jnp.ones((G, L, ML_STATE_W - ML_WIDTH), BF16)], axis=2)
    upd = _per_stream(lambda a, b: lax.dot_general(a, b, _TN, preferred_element_type=F32), kw, v_aug)
    s_row_head = lax.broadcasted_iota(jnp.int32, (1, ML_WIDTH, ML_STATE_W), 1) // D_M
    s_col = lax.broadcasted_iota(jnp.int32, (1, ML_WIDTH, ML_STATE_W), 2)
    s_col_head = jnp.where(s_col < ML_WIDTH, s_col // D_M, s_col - ML_WIDTH)
    s_ref[...] = decay * state + jnp.where(s_row_head == s_col_head, upd, 0.0)
    m_ref[...] = m_next

    hg = num * jax.nn.sigmoid(og_ref[...])
    scale = jnp.zeros((G, L, ML_WIDTH), F32)
    for h in range(H_M):
        head = lane_head == h
        ss = jnp.sum(jnp.where(head, hg * hg, 0.0), axis=2, keepdims=True)
        scale = jnp.where(head, lax.rsqrt(ss * (1.0 / D_M) + EPS), scale)
    hm_ref[...] = (hg * scale * gml_ref[...]).astype(BF16)

    u = c_ref[...] * hc_ref[...]
    prev = cv_ref[...]
    ri = lax.broadcasted_iota(jnp.int32, (1, L, CONV_CH), 1)
    u1 = jnp.where(ri == 0, prev[:, 1:2], _per_stream(lambda a: pltpu.roll(a, 1, 0), u))
    u2 = jnp.where(ri == 0, prev[:, 0:1],
                   jnp.where(ri == 1, prev[:, 1:2], _per_stream(lambda a: pltpu.roll(a, 2, 0), u)))
    w = cw_ref[...]
    oc_ref[...] = (b_ref[...] * (w[0:1] * u2 + w[1:2] * u1 + w[2:3] * u)).astype(BF16)
    cv_ref[:, 0:1, :] = u[:, L - 2:L - 1]
    cv_ref[:, 1:2, :] = u[:, L - 1:L]


def _recurrent(gate_b, qkvm, og, gt, bch, conv_w, g_ml, s0, m0, cv0, batch, seq, L, G):
    assert batch % G == 0 and seq % L == 0
    nc = seq // L
    tok = lambda j: pl.BlockSpec((G, L, ML_WIDTH), lambda b, c: (b, c, j))
    per_b = lambda shape: pl.BlockSpec((G,) + shape, lambda b, c: (b, 0, 0))
    return pl.pallas_call(
        functools.partial(_recurrent_kernel, L=L),
        grid=(batch // G, nc),
        in_specs=[pl.BlockSpec(memory_space=pltpu.SMEM),
                  tok(0), tok(1), tok(2), tok(0),
                  pl.BlockSpec((G, 1, 8, L), lambda b, c: (b, c, 0, 0)),
                  tok(0), tok(1), tok(2),
                  _const_spec((CONV_K, CONV_CH)), _const_spec((1, ML_WIDTH)),
                  per_b((ML_WIDTH, ML_STATE_W)), per_b((1, 128)), per_b((CONV_K - 1, CONV_CH))],
        out_specs=[tok(0), tok(0),
                   per_b((ML_WIDTH, ML_STATE_W)), per_b((1, 128)), per_b((CONV_K - 1, CONV_CH))],
        out_shape=[jax.ShapeDtypeStruct((batch, seq, ML_WIDTH), BF16),
                   jax.ShapeDtypeStruct((batch, seq, CONV_CH), BF16),
                   jax.ShapeDtypeStruct((batch, ML_WIDTH, ML_STATE_W), F32),
                   jax.ShapeDtypeStruct((batch, 1, 128), F32),
                   jax.ShapeDtypeStruct((batch, CONV_K - 1, CONV_CH), F32)],
        compiler_params=pltpu.CompilerParams(dimension_semantics=("parallel", "arbitrary"),
                                             vmem_limit_bytes=V7X_VMEM_LIMIT),
        name="recurrent",
    )(gate_b, qkvm, qkvm, qkvm, og, gt, bch, bch, bch, conv_w, g_ml, s0, m0, cv0)


def _mem_kv_kernel(x_ref, w_ref, kf_ref, vf_ref, kb_ref, vb_ref):
    x = x_ref[...].astype(BF16)
    n = H_X * D_X
    k = jnp.dot(x, w_ref[:, :n], preferred_element_type=F32)
    v = jnp.dot(x, w_ref[:, n:], preferred_element_type=F32)
    kf_ref[...] = k
    vf_ref[...] = v
    kb_ref[...] = k.astype(BF16)
    vb_ref[...] = v.astype(BF16)


def _mem_kv(mem, w_kv, tm):
    t = mem.shape[0]
    n = H_X * D_X
    row = pl.BlockSpec((tm, n), lambda i: (i, 0))
    return pl.pallas_call(
        _mem_kv_kernel,
        grid=(t // tm,),
        in_specs=[pl.BlockSpec((tm, D_MODEL), lambda i: (i, 0)), _const_spec((D_MODEL, 2 * n))],
        out_specs=[row, row, row, row],
        out_shape=[jax.ShapeDtypeStruct((t, n), F32), jax.ShapeDtypeStruct((t, n), F32),
                   jax.ShapeDtypeStruct((t, n), BF16), jax.ShapeDtypeStruct((t, n), BF16)],
        compiler_params=pltpu.CompilerParams(dimension_semantics=("parallel",),
                                             vmem_limit_bytes=V7X_VMEM_LIMIT),
        name="mem_kv",
    )(mem, w_kv)


FF_CHUNK = 256


def _post_kernel(x_ref, oa_ref, hm_ref, oc_ref, mk_ref, mv_ref, wout_ref, gc_ref, wcq_ref, wco_ref,
                 gf_ref, wg_ref, wu_ref, wd_ref, gfin_ref, o_ref, *, final):
    a, m = ATT_WIDTH, ML_WIDTH
    x = x_ref[...]
    x = x + (jnp.dot(oa_ref[...], wout_ref[0:a, :], preferred_element_type=F32)
             + jnp.dot(hm_ref[...], wout_ref[a:a + m, :], preferred_element_type=F32)
             + jnp.dot(oc_ref[...], wout_ref[a + m:, :], preferred_element_type=F32))

    xn = _rms(x, gc_ref[...]).astype(BF16)
    qc = (jnp.dot(xn, wcq_ref[...], preferred_element_type=F32) * (D_X ** -0.5)).astype(BF16)
    cross = jnp.zeros_like(x)
    n_streams = mk_ref.shape[0]
    rows = x.shape[0] // n_streams
    heads = [slice(h * D_X, (h + 1) * D_X) for h in range(H_X)]
    streams = [slice(b * rows, (b + 1) * rows) for b in range(n_streams)]
    scores = [[lax.dot_general(qc[r, sl], mk_ref[b, :, sl], _NT, preferred_element_type=F32)
               for b, r in enumerate(streams)] for sl in heads]
    for sl, s_head in zip(heads, scores):
        outs = []
        for b, s in enumerate(s_head):
            p = jnp.exp(s - jnp.max(s, axis=1, keepdims=True))
            p = p * (1.0 / jnp.sum(p, axis=1, keepdims=True))
            outs.append(jnp.dot(p.astype(BF16), mv_ref[b, :, sl], preferred_element_type=F32).astype(BF16))
        o = outs[0] if n_streams == 1 else jnp.concatenate(outs, axis=0)
        cross = cross + jnp.dot(o, wco_ref[sl, :], preferred_element_type=F32)
    x = x + cross

    xn = _rms(x, gf_ref[...]).astype(BF16)
    ff = jnp.zeros_like(x)
    for j in range(D_FF // FF_CHUNK):
        sl = slice(j * FF_CHUNK, (j + 1) * FF_CHUNK)
        g = jnp.dot(xn, wg_ref[:, sl], preferred_element_type=F32)
        u = jnp.dot(xn, wu_ref[:, sl], preferred_element_type=F32)
        act = (g * jax.nn.sigmoid(g) * u).astype(BF16)
        ff = ff + jnp.dot(act, wd_ref[sl, :], preferred_element_type=F32)
    x = x + ff
    if final:
        x = _rms(x, gfin_ref[...])
    o_ref[...] = x


def _post(x, oa, hm, oc, mk, mv, w_out, g_cross, w_cq, w_co, g_ffn, w_gate, w_up, w_down, g_final,
          tm, seq, final):
    t = x.shape[0]
    row = lambda n: pl.BlockSpec((tm, n), lambda i: (i, 0))
    if tm <= seq:
        tiles_per_stream = seq // tm
        mem = pl.BlockSpec((1,) + mk.shape[1:], lambda i: (i // tiles_per_stream, 0, 0))
    else:
        mem = pl.BlockSpec((tm // seq,) + mk.shape[1:], lambda i: (i, 0, 0))
    vec = _const_spec((1, D_MODEL))
    return pl.pallas_call(
        functools.partial(_post_kernel, final=final),
        grid=(t // tm,),
        in_specs=[row(D_MODEL), row(ATT_WIDTH), row(ML_WIDTH), row(CONV_CH), mem, mem,
                  _const_spec(w_out.shape), vec, _const_spec(w_cq.shape), _const_spec(w_co.shape),
                  vec, _const_spec(w_gate.shape), _const_spec(w_up.shape), _const_spec(w_down.shape), vec],
        out_specs=row(D_MODEL),
        out_shape=jax.ShapeDtypeStruct((t, D_MODEL), F32),
        compiler_params=pltpu.CompilerParams(dimension_semantics=("parallel",),
                                             vmem_limit_bytes=V7X_VMEM_LIMIT),
        name="post",
    )(x, oa, hm, oc, mk, mv, w_out, g_cross, w_cq, w_co, g_ffn, w_gate, w_up, w_down, g_final)


def _rel_bucket(rel):
    half = NUM_BUCKETS // 2
    max_exact = half // 2
    n = jnp.abs(rel)
    large = max_exact + (jnp.log(jnp.maximum(n, 1).astype(F32) / max_exact)
                         / math.log(MAX_DISTANCE / max_exact) * (half - max_exact)).astype(jnp.int32)
    large = jnp.minimum(large, half - 1)
    return jnp.where(rel > 0, half, 0) + jnp.where(n < max_exact, n, large)


def _bias_table(rel_bias, rel):
    bucket = _rel_bucket(rel)[None]
    table = rel_bias.astype(F32)
    out = jnp.zeros((table.shape[1],) + rel.shape, F32)
    for b in range(NUM_BUCKETS):
        out = jnp.where(bucket == b, table[b][:, None, None], out)
    return out


def _prompt_bias_tiles(rel_bias, t):
    key = jnp.arange(t, dtype=jnp.int32)[:, None]
    qry = jnp.arange(t, dtype=jnp.int32)[None, :]
    far = rel_bias[_rel_bucket(jnp.int32(-MAX_DISTANCE))].astype(F32)[:, None, None]
    prev = (_bias_table(rel_bias, key - t - qry) - far) * LOG2E
    diag = jnp.where((key // CHUNK) <= (qry // CHUNK), (_bias_table(rel_bias, key - qry) - far) * LOG2E, -jnp.inf)
    both_maps = lambda a: jnp.concatenate([a, a], axis=2)
    tiles = jnp.stack([jnp.concatenate([diag, diag], axis=1), jnp.concatenate([prev, diag], axis=1)], axis=1)
    return lax.optimization_barrier(both_maps(tiles.reshape(-1, 2 * t, t)).reshape(-1, 2, 2 * t, 2 * t))


def _block_diag_state(c, n):
    b = c.shape[0]
    eye = jnp.eye(H_M, dtype=F32)
    cbd = jnp.einsum('bhdv,hg->bhdgv', c.astype(F32), eye).reshape(b, ML_WIDTH, ML_WIDTH)
    ncol = jnp.einsum('bhd,hg->bhdg', n.astype(F32), eye).reshape(b, ML_WIDTH, H_M)
    pad = jnp.zeros((b, ML_WIDTH, ML_STATE_W - ML_WIDTH - H_M), F32)
    return jnp.concatenate([cbd, ncol, pad], axis=2)


def _unpack_state(s, m):
    b = s.shape[0]
    blocks = s[:, :, :ML_WIDTH].reshape(b, H_M, D_M, H_M, D_M)
    c = jnp.stack([blocks[:, h, :, h, :] for h in range(H_M)], axis=1)
    ncols = s[:, :, ML_WIDTH:ML_WIDTH + H_M].reshape(b, H_M, D_M, H_M)
    n = jnp.stack([ncols[:, h, :, h] for h in range(H_M)], axis=1)
    return c, n, m[:, 0, :H_M]


def _layer(x, layer, depth, kv_all, wts, attn_fn, mem_k, mem_v, ml_state, conv_prev, g_final,
           batch, seq, tm_proj, L, tm_post):
    final = layer == depth - 1
    (g_mix, w_proj, w_gt, gate_b, conv_w, lam_p, g_att, g_ml, w_out,
     g_cross, w_cq, w_co, g_ffn, w_gate, w_up, w_down) = wts
    t = batch * seq
    lam_init = 0.8 - 0.6 * math.exp(-0.3 * layer)
    qa, kf, kb, vf, vb, qkvm, og, bch, gt = _in_proj(x, g_mix, w_proj, w_gt, tm_proj, layer, depth, kv_all)
    oa = attn_fn(qa, kb, vb, lam_p, g_att, lam_init)
    gt = jnp.transpose(gt.reshape(8, batch, seq // L, L), (1, 2, 0, 3))
    tok3 = lambda a: a.reshape(batch, seq, a.shape[-1])
    hm, oc, s1, m1, cv1 = _recurrent(gate_b, tok3(qkvm), tok3(og), gt, tok3(bch), conv_w, g_ml,
                                     ml_state[0], ml_state[1], conv_prev, batch, seq, L,
                                     min(RECURRENT_STREAMS, batch))
    hm, oc = hm.reshape(t, ML_WIDTH), oc.reshape(t, CONV_CH)
    x = _post(x, oa, hm, oc, mem_k, mem_v, w_out, g_cross, w_cq, w_co, g_ffn, w_gate, w_up, w_down,
              g_final, tm_post, seq, final)
    c1, n1, mm1 = _unpack_state(s1, m1)
    return x, (kf, vf), c1, n1, mm1, cv1


def kernel(x_prompt, x_sample, mem_prompt, cache_att_k, cache_att_v, cache_mem_k, cache_mem_v,
           state_mlstm_C, state_mlstm_n, state_mlstm_m, state_conv,
           norm_mix, w_in, mlstm_gate_bias, conv_w, lambda_params, norm_att_heads, norm_mlstm_heads,
           w_out, norm_cross, w_cq, w_ck, w_cv, w_co, norm_ffn, w_gate, w_up, w_down,
           rel_bias, norm_final):
    bp, sp, _ = x_prompt.shape
    bs, ss, _ = x_sample.shape
    depth = w_in.shape[0]
    past = cache_att_k.shape[2]
    n_mem = mem_prompt.shape[1]
    t_attn = 256
    l_prompt = 128

    xp = x_prompt.reshape(bp * sp, D_MODEL)
    xs = x_sample.reshape(bs * ss, D_MODEL)
    mem = mem_prompt.reshape(bp * n_mem, D_MODEL)
    g_final = norm_final.reshape(1, D_MODEL)

    bias_tiles = _prompt_bias_tiles(rel_bias, t_attn)
    rel_s = jnp.arange(past + ss, dtype=jnp.int32)[None, :] - (past + jnp.arange(ss, dtype=jnp.int32))[:, None]
    bias_s = _bias_table(rel_bias, rel_s) * LOG2E
    bias_s_past, bias_s_new = lax.optimization_barrier((bias_s[:, :, :past], bias_s[:, :, past:]))

    zero_state = (jnp.zeros((bp, ML_WIDTH, ML_STATE_W), F32), jnp.zeros((bp, 1, 128), F32))
    zero_conv = jnp.zeros((bp, CONV_K - 1, CONV_CH), F32)

    outs = {k: [] for k in ('pmk', 'pmv', 'pC', 'pn', 'pm', 'pcv', 'sC', 'sn', 'sm', 'scv')}
    kv_p = kv_s = None
    gate_lo = 3 * ATT_WIDTH + 4 * ML_WIDTH
    for l in range(depth):
        w = w_in[l]
        wts = (norm_mix[l].reshape(1, D_MODEL),
               jnp.concatenate([w[:, :gate_lo], w[:, gate_lo + 2 * H_M:]], axis=1).astype(BF16),
               w[:, gate_lo:gate_lo + 2 * H_M].T.astype(BF16),
               mlstm_gate_bias[l].astype(F32), conv_w[l].astype(F32), lambda_params[l].astype(F32),
               norm_att_heads[l].reshape(1, ATT_WIDTH), norm_mlstm_heads[l].reshape(1, ML_WIDTH),
               w_out[l].astype(BF16), norm_cross[l].reshape(1, D_MODEL),
               w_cq[l].astype(BF16), w_co[l].astype(BF16), norm_ffn[l].reshape(1, D_MODEL),
               w_gate[l].astype(BF16), w_up[l].astype(BF16), w_down[l].astype(BF16))

        w_kv = jnp.concatenate([w_ck[l], w_cv[l]], axis=1).astype(BF16)
        mkf, mvf, mkb, mvb = _mem_kv(mem, w_kv, 512)
        attn_p = lambda q, k, v, lam_p, g, li: _attn_prompt(q, k, v, bias_tiles, lam_p, g, bp, sp, t_attn, li)
        xp, kv_p, c1, n1, m1, cv1 = _layer(
            xp, l, depth, kv_p, wts, attn_p, mkb.reshape(bp, n_mem, -1), mvb.reshape(bp, n_mem, -1),
            zero_state, zero_conv, g_final, bp, sp, 512, l_prompt, 512)
        outs['pmk'].append(mkf.reshape(bp, n_mem, H_X, D_X))
        outs['pmv'].append(mvf.reshape(bp, n_mem, H_X, D_X))
        outs['pC'].append(c1); outs['pn'].append(n1); outs['pm'].append(m1); outs['pcv'].append(cv1)

        k_past = cache_att_k[l].reshape(bs, past, ATT_WIDTH)
        v_past = cache_att_v[l].reshape(bs, past, ATT_WIDTH)
        attn_s = lambda q, k, v, lam_p, g, li: _attn_sample(q, k, v, k_past, v_past, bias_s_past, bias_s_new,
                                                            lam_p, g, bs, ss, li)
        s0 = _block_diag_state(state_mlstm_C[l], state_mlstm_n[l])
        m0 = jnp.pad(state_mlstm_m[l].astype(F32), ((0, 0), (0, 128 - H_M))).reshape(bs, 1, 128)
        xs, kv_s, c1, n1, m1, cv1 = _layer(
            xs, l, depth, kv_s, wts, attn_s,
            cache_mem_k[l].reshape(bs, n_mem, -1).astype(BF16), cache_mem_v[l].reshape(bs, n_mem, -1).astype(BF16),
            (s0, m0), state_conv[l].astype(F32), g_final, bs, ss, bs * ss, ss, bs * ss)
        outs['sC'].append(c1); outs['sn'].append(n1); outs['sm'].append(m1); outs['scv'].append(cv1)

    st = lambda k: jnp.stack(outs[k])
    per_head = lambda a, b, s: a.reshape(depth, b, s, H_A, DV_A)
    return (xp.reshape(bp, sp, D_MODEL), xs.reshape(bs, ss, D_MODEL),
            per_head(kv_p[0], bp, sp), per_head(kv_p[1], bp, sp),
            st('pmk'), st('pmv'), st('pC'), st('pn'), st('pm'), st('pcv'),
            per_head(kv_s[0], bs, ss), per_head(kv_s[1], bs, ss),
            st('sC'), st('sn'), st('sm'), st('scv'))
```

```python
import functools
import math

import jax
import jax.numpy as jnp
from jax import lax
from jax.experimental import pallas as pl
from jax.experimental.pallas import tpu as pltpu

F32 = jnp.float32
BF16 = jnp.bfloat16

D_MODEL = 1024
CHUNK = 64
H_A = 4
D_A = 64
DV_A = 128
ATT_WIDTH = H_A * DV_A
H_M = 4
D_M = 64
ML_WIDTH = H_M * D_M
CONV_CH = 256
CONV_K = 3
H_X = 4
D_X = 256
D_FF = 2816
NUM_BUCKETS = 32
MAX_DISTANCE = 128
EPS = 1e-6

N_PROJ = 3 * ATT_WIDTH + 4 * ML_WIDTH + 3 * CONV_CH
ML_STATE_W = ML_WIDTH + 128

V7X_VMEM_LIMIT = 56 * 1024 * 1024

LOG2E = math.log2(math.e)
RECURRENT_STREAMS = 8
ATTN_HEADS_PER_STEP = 4

_NT = (((1,), (1,)), ((), ()))
_TN = (((0,), (0,)), ((), ()))


def _const_spec(shape):
    nd = len(shape)
    return pl.BlockSpec(shape, lambda *_: (0,) * nd, pipeline_mode=pl.Buffered(1))


def _rms(x, g):
    ms = jnp.mean(x * x, axis=-1, keepdims=True)
    return x * lax.rsqrt(ms + EPS) * g


def _in_proj_kernel(x_ref, g_ref, w_ref, wgt_ref, *refs):
    qa_ref, kf_ref, kb_ref, vf_ref, vb_ref, qkvm_ref, om_ref, bch_ref, gt_ref = refs[-9:]
    n_prev = kf_ref.shape[0] - 1
    if n_prev:
        kf_ref[0:n_prev] = refs[0][...]
        vf_ref[0:n_prev] = refs[1][...]
    xn = _rms(x_ref[...], g_ref[...]).astype(BF16)

    def proj(lo, hi):
        return jnp.dot(xn, w_ref[:, lo:hi], preferred_element_type=F32)

    a = ATT_WIDTH
    qa_ref[...] = (proj(0, a) * (D_A ** -0.5 * LOG2E)).astype(BF16)
    k = proj(a, 2 * a)
    kf_ref[n_prev] = k
    kb_ref[...] = k.astype(BF16)
    v = proj(2 * a, 3 * a)
    vf_ref[n_prev] = v
    vb_ref[...] = v.astype(BF16)
    o = 3 * a
    m = ML_WIDTH
    qkvm_ref[:, 0:m] = proj(o, o + m).astype(BF16)
    qkvm_ref[:, m:2 * m] = (proj(o + m, o + 2 * m) * (D_M ** -0.5)).astype(BF16)
    qkvm_ref[:, 2 * m:3 * m] = proj(o + 2 * m, o + 3 * m).astype(BF16)
    om_ref[...] = proj(o + 3 * m, o + 4 * m)
    bch_ref[...] = proj(o + 4 * m, N_PROJ)
    gt_ref[...] = lax.dot_general(wgt_ref[...], xn, _NT, preferred_element_type=F32)


def _in_proj(x, g, w, wgt, tm, kv_prev):
    t = x.shape[0]
    a, m = ATT_WIDTH, ML_WIDTH
    n_prev = 0 if kv_prev is None else kv_prev[0].shape[0]
    row = lambda n: pl.BlockSpec((tm, n), lambda i: (i, 0))
    slabs = lambda n: pl.BlockSpec((n, tm, a), lambda i: (0, i, 0))
    outs = [
        (jax.ShapeDtypeStruct((t, a), BF16), row(a)),
        (jax.ShapeDtypeStruct((n_prev + 1, t, a), F32), slabs(n_prev + 1)),
        (jax.ShapeDtypeStruct((t, a), BF16), row(a)),
        (jax.ShapeDtypeStruct((n_prev + 1, t, a), F32), slabs(n_prev + 1)),
        (jax.ShapeDtypeStruct((t, a), BF16), row(a)),
        (jax.ShapeDtypeStruct((t, 3 * m), BF16), row(3 * m)),
        (jax.ShapeDtypeStruct((t, m), F32), row(m)),
        (jax.ShapeDtypeStruct((t, 3 * CONV_CH), F32), row(3 * CONV_CH)),
        (jax.ShapeDtypeStruct((8, t), F32), pl.BlockSpec((8, tm), lambda i: (0, i))),
    ]
    carried = [] if kv_prev is None else list(kv_prev)
    return pl.pallas_call(
        _in_proj_kernel,
        grid=(t // tm,),
        in_specs=[row(D_MODEL), _const_spec((1, D_MODEL)), _const_spec((D_MODEL, N_PROJ)),
                  _const_spec((8, D_MODEL))] + [slabs(n_prev)] * len(carried),
        out_specs=[s for _, s in outs],
        out_shape=[s for s, _ in outs],
        compiler_params=pltpu.CompilerParams(dimension_semantics=("parallel",),
                                             vmem_limit_bytes=V7X_VMEM_LIMIT),
        name="in_proj",
    )(x, g, w, wgt, *carried)


def _lambda_value(lp, lam_init):
    a = jnp.sum(lp[0:1] * lp[1:2], axis=1, keepdims=True)
    b = jnp.sum(lp[2:3] * lp[3:4], axis=1, keepdims=True)
    return jnp.exp(a) - jnp.exp(b) + lam_init


def _stack_maps(q):
    lane = lax.broadcasted_iota(jnp.int32, q.shape, 1)
    zero = jnp.zeros_like(q)
    return jnp.concatenate([jnp.where(lane < D_A, q, zero), jnp.where(lane >= D_A, q, zero)], axis=0)


def _attn_finish(acc, lam, g, lam_init, t):
    o = acc[:, :DV_A] / acc[:, DV_A:]
    d = o[:t] - lam * o[t:]
    return (_rms(d, g) * (1.0 - lam_init)).astype(BF16)


def _with_ones(v):
    return jnp.concatenate([v, jnp.ones_like(v)], axis=1)


def _attn_prompt_kernel(lam_ref, g_ref, q_ref, k_ref, v_ref, bias_ref, o_ref,
                        m_scr, l_scr, acc_scr, sa_scr, sb_scr, p_scr, *, t, hp, lam_init):
    i = pl.program_id(2)
    heads = [slice(hh * DV_A, (hh + 1) * DV_A) for hh in range(hp)]
    cols = [slice(hh * 2 * t, (hh + 1) * 2 * t) for hh in range(hp)]
    qz = [_stack_maps(q_ref[:, hd]) for hd in heads]

    def rows(kb):
        return pl.ds(pl.multiple_of(kb * t, t), t)

    def qk(kb):
        r = rows(kb)
        return jnp.concatenate([lax.dot_general(k_ref[r, hd], z, _NT, preferred_element_type=F32)
                                for hd, z in zip(heads, qz)], axis=1)

    def pv(p, kb):
        r = rows(kb)
        return jnp.concatenate([lax.dot_general(v_ref[r, hd], p[:, c], _TN, preferred_element_type=F32)
                                for hd, c in zip(heads, cols)], axis=1)

    def bias(lo):
        return jnp.concatenate([bias_ref[hh, 0, lo:lo + t, :] for hh in range(hp)], axis=1)

    def softmax_step(m_prev, s):
        m_new = jnp.maximum(m_prev, jnp.max(s, axis=0, keepdims=True))
        p = jnp.exp2(s - m_new)
        return m_new, jnp.exp2(m_prev - m_new), p, jnp.sum(p, axis=0, keepdims=True)

    n_far = jnp.maximum(i - 1, 0)
    last = n_far - 1
    kb_prev = jnp.maximum(i - 1, 0)

    s_diag = qk(i)
    s_prev = qk(kb_prev)
    sa_scr[...] = qk(0)
    s_diag = s_diag + bias(t)
    m = jnp.max(s_diag, axis=0, keepdims=True)
    p = jnp.exp2(s_diag - m)
    l = jnp.sum(p, axis=0, keepdims=True)
    acc = pv(p.astype(BF16), i)
    w_prev = (i >= 1).astype(F32)
    m, alpha, p, p_sum = softmax_step(m, s_prev + bias(0))
    m_scr[...] = m
    l_scr[...] = alpha * l + w_prev * p_sum
    acc_scr[...] = alpha * acc
    p_scr[...] = p.astype(BF16)

    def far_pair(j, carry):
        kb1 = jnp.minimum(2 * j + 1, last)
        w1 = (2 * j + 1 <= last).astype(F32)
        kb_pending = jnp.where(j == 0, kb_prev, 2 * j - 1)
        sb_scr[...] = qk(kb1)
        owed = pv(p_scr[...], kb_pending)
        m, alpha, p, p_sum = softmax_step(m_scr[...], sa_scr[...])
        l = alpha * l_scr[...] + p_sum
        acc = alpha * (acc_scr[...] + owed)
        sa_scr[...] = qk(jnp.minimum(2 * j + 2, last))
        owed = pv(p.astype(BF16), 2 * j)
        m, alpha, p, p_sum = softmax_step(m, sb_scr[...])
        m_scr[...] = m
        l_scr[...] = alpha * l + w1 * p_sum
        acc_scr[...] = alpha * (acc + owed)
        p_scr[...] = p.astype(BF16)
        return carry

    trips = (n_far + 1) // 2
    lax.fori_loop(0, trips, far_pair, 0)

    kb_pending = jnp.where(trips == 0, kb_prev, jnp.minimum(2 * trips - 1, last))
    w_pending = jnp.where(trips == 0, w_prev, (n_far % 2 == 0).astype(F32))
    acc = acc_scr[...] + w_pending * pv(p_scr[...], kb_pending)

    lam = _lambda_value(lam_ref[...], lam_init)
    o = acc * (1.0 / l_scr[...])
    for hd, c in zip(heads, cols):
        d = o[:, c][:, :t] - lam * o[:, c][:, t:]
        ms = jnp.mean(d * d, axis=0, keepdims=True)
        y = jnp.transpose(d * lax.rsqrt(ms + EPS))
        o_ref[:, hd] = (y * g_ref[:, hd] * (1.0 - lam_init)).astype(BF16)


def _attn_prompt(q, k, v, bias, lam_p, g_att, batch, seq, t, lam_init):
    nq = seq // t
    hp = ATTN_HEADS_PER_STEP
    w = hp * 2 * t
    kv_spec = pl.BlockSpec((seq, hp * DV_A), lambda b, h, i: (b, h))
    return pl.pallas_call(
        functools.partial(_attn_prompt_kernel, t=t, hp=hp, lam_init=lam_init),
        grid=(batch, H_A // hp, nq),
        in_specs=[_const_spec((4, D_A)),
                  pl.BlockSpec((1, hp * DV_A), lambda b, h, i: (0, h)),
                  pl.BlockSpec((t, hp * DV_A), lambda b, h, i: (b * nq + i, h)),
                  kv_spec, kv_spec,
                  pl.BlockSpec((hp, 1, 2 * t, 2 * t), lambda b, h, i: (h, jnp.minimum(i, 1), 0, 0))],
        out_specs=pl.BlockSpec((t, hp * DV_A), lambda b, h, i: (b * nq + i, h)),
        out_shape=jax.ShapeDtypeStruct((batch * seq, ATT_WIDTH), BF16),
        scratch_shapes=[pltpu.VMEM((1, w), F32), pltpu.VMEM((1, w), F32),
                        pltpu.VMEM((DV_A, w), F32),
                        pltpu.VMEM((t, w), F32), pltpu.VMEM((t, w), F32),
                        pltpu.VMEM((t, w), BF16)],
        compiler_params=pltpu.CompilerParams(dimension_semantics=("parallel", "parallel", "arbitrary"),
                                             vmem_limit_bytes=V7X_VMEM_LIMIT),
        name="attn_prompt",
    )(lam_p, g_att, q, k, v, bias)


def _attn_sample_kernel(lam_ref, g_ref, q_ref, kp_ref, vp_ref, kn_ref, vn_ref, bp_ref, bn_ref, o_ref,
                        *, lq, lam_init):
    qz = _stack_maps(q_ref[...])
    bp = bp_ref[0]
    bn = bn_ref[0]
    sp = lax.dot_general(qz, kp_ref[0].astype(BF16), _NT, preferred_element_type=F32)
    sp = sp + jnp.concatenate([bp, bp], axis=0)
    sn = lax.dot_general(qz, kn_ref[...], _NT, preferred_element_type=F32)
    sn = sn + jnp.concatenate([bn, bn], axis=0)
    m = jnp.maximum(jnp.max(sp, axis=1, keepdims=True), jnp.max(sn, axis=1, keepdims=True))
    pp = jnp.exp2(sp - m)
    pn = jnp.exp2(sn - m)
    acc = (jnp.dot(pp.astype(BF16), _with_ones(vp_ref[0].astype(BF16)), preferred_element_type=F32)
           + jnp.dot(pn.astype(BF16), _with_ones(vn_ref[...]), preferred_element_type=F32))
    lam = _lambda_value(lam_ref[...], lam_init)
    o_ref[...] = _attn_finish(acc, lam, g_ref[...], lam_init, lq)


def _attn_sample(q, k_new, v_new, k_past, v_past, bias_past, bias_new, lam_p, g_att, batch, lq, lam_init):
    past = k_past.shape[1]
    new_spec = pl.BlockSpec((lq, DV_A), lambda b, h: (b, h))
    past_spec = pl.BlockSpec((1, past, DV_A), lambda b, h: (b, 0, h))
    return pl.pallas_call(
        functools.partial(_attn_sample_kernel, lq=lq, lam_init=lam_init),
        grid=(batch, H_A),
        in_specs=[_const_spec((4, D_A)),
                  pl.BlockSpec((1, DV_A), lambda b, h: (0, h)),
                  new_spec, past_spec, past_spec, new_spec, new_spec,
                  pl.BlockSpec((1, lq, past), lambda b, h: (h, 0, 0)),
                  pl.BlockSpec((1, lq, lq), lambda b, h: (h, 0, 0))],
        out_specs=new_spec,
        out_shape=jax.ShapeDtypeStruct((batch * lq, ATT_WIDTH), BF16),
        compiler_params=pltpu.CompilerParams(dimension_semantics=("parallel", "parallel"),
                                             vmem_limit_bytes=V7X_VMEM_LIMIT),
        name="attn_sample",
    )(lam_p, g_att, q, k_past, v_past, k_new, v_new, bias_past, bias_new)


def _per_stream(fn, *arrays):
    return jnp.stack([fn(*(a[g] for a in arrays)) for g in range(arrays[0].shape[0])])


def _recurrent_kernel(gb_ref, q_ref, k_ref, v_ref, og_ref, gt_ref, b_ref, c_ref, hc_ref, cw_ref, gml_ref,
                      s0_ref, m0_ref, cv0_ref,
                      hm_ref, oc_ref, s_ref, m_ref, cv_ref, *, L):
    @pl.when(pl.program_id(1) == 0)
    def _():
        s_ref[...] = s0_ref[...]
        m_ref[...] = m0_ref[...]
        cv_ref[...] = cv0_ref[...]

    q = q_ref[...]
    k = k_ref[...]
    v = v_ref[...]
    G = q.shape[0]
    state = s_ref[...]
    q_state = _per_stream(lambda a, b: jnp.dot(a, b, preferred_element_type=F32),
                          q, state.astype(BF16))
    q_c = q_state[:, :, :ML_WIDTH]

    lane_head = lax.broadcasted_iota(jnp.int32, (1, L, ML_WIDTH), 2) // D_M
    row = lax.broadcasted_iota(jnp.int32, (1, L, L), 1)
    col = lax.broadcasted_iota(jnp.int32, (1, L, L), 2)
    causal = col <= row
    eye = col == row
    st_lane = lax.broadcasted_iota(jnp.int32, (1, 1, ML_STATE_W), 2)
    st_lane_head = jnp.where(st_lane < ML_WIDTH, st_lane // D_M, st_lane - ML_WIDTH)
    m_lane = lax.broadcasted_iota(jnp.int32, (1, 1, 128), 2)

    gt = gt_ref[:, 0]
    m_all = m_ref[...]
    num = jnp.zeros((G, L, ML_WIDTH), F32)
    w_state = jnp.zeros((G, L, ML_WIDTH), F32)
    decay = jnp.zeros((G, 1, ML_STATE_W), F32)
    m_next = jnp.zeros((G, 1, 128), F32)
    for h in range(H_M):
        ig = gt[:, h:h + 1, :] + gb_ref[0, h]
        fz = gt[:, H_M + h:H_M + h + 1, :] + gb_ref[1, h]
        lf = jnp.minimum(fz, 0.0) - jnp.log1p(jnp.exp(-jnp.abs(fz)))
        f_col = jnp.sum(jnp.where(causal, lf, 0.0), axis=2, keepdims=True)
        f_row = jnp.sum(jnp.where(eye, f_col, 0.0), axis=1, keepdims=True)
        d = jnp.where(causal, f_col + (ig - f_row), -jnp.inf)
        m_prev = m_all[:, :, h:h + 1]
        inter = f_col + m_prev
        mt = jnp.maximum(inter, jnp.max(d, axis=2, keepdims=True))
        w_intra = jnp.exp(d - mt)
        w_inter = jnp.exp(inter - mt)
        head = lane_head == h
        qk = _per_stream(lambda a, b: lax.dot_general(a, b, _NT, preferred_element_type=F32),
                         jnp.where(head, q, jnp.zeros_like(q)), k)
        s = qk * w_intra
        den = jnp.sum(s, axis=2, keepdims=True) + w_inter * q_state[:, :, ML_WIDTH + h:ML_WIDTH + h + 1]
        inv = 1.0 / jnp.maximum(jnp.abs(den), jnp.exp(-mt))
        sv = _per_stream(lambda a, b: jnp.dot(a, b, preferred_element_type=F32), s.astype(BF16), v)
        num = jnp.where(head, (sv + w_inter * q_c) * inv, num)
        m_new = mt[:, L - 1:L, :]
        w_col = jnp.sum(jnp.where(eye, w_intra[:, L - 1:L, :], 0.0), axis=2, keepdims=True)
        w_state = jnp.where(head, w_col, w_state)
        dec = jnp.exp(f_col[:, L - 1:L, :] + m_prev - m_new)
        decay = jnp.where(st_lane_head == h, dec, decay)
        m_next = jnp.where(m_lane == h, m_new, m_next)

    kw = (k.astype(F32) * w_state).astype(BF16)
    v_aug = jnp.concatenate([v, jnp.ones((G, L, ML_STATE_W - ML_WIDTH), BF16)], axis=2)
    upd = _per_stream(lambda a, b: lax.dot_general(a, b, _TN, preferred_element_type=F32), kw, v_aug)
    s_row_head = lax.broadcasted_iota(jnp.int32, (1, ML_WIDTH, ML_STATE_W), 1) // D_M
    s_col = lax.broadcasted_iota(jnp.int32, (1, ML_WIDTH, ML_STATE_W), 2)
    s_col_head = jnp.where(s_col < ML_WIDTH, s_col // D_M, s_col - ML_WIDTH)
    s_ref[...] = decay * state + jnp.where(s_row_head == s_col_head, upd, 0.0)
    m_ref[...] = m_next

    hg = num * jax.nn.sigmoid(og_ref[...])
    scale = jnp.zeros((G, L, ML_WIDTH), F32)
    for h in range(H_M):
        head = lane_head == h
        ss = jnp.sum(jnp.where(head, hg * hg, 0.0), axis=2, keepdims=True)
        scale = jnp.where(head, lax.rsqrt(ss * (1.0 / D_M) + EPS), scale)
    hm_ref[...] = (hg * scale * gml_ref[...]).astype(BF16)

    u = c_ref[...] * hc_ref[...]
    prev = cv_ref[...]
    ri = lax.broadcasted_iota(jnp.int32, (1, L, CONV_CH), 1)
    u1 = jnp.where(ri == 0, prev[:, 1:2], _per_stream(lambda a: pltpu.roll(a, 1, 0), u))
    u2 = jnp.where(ri == 0, prev[:, 0:1],
                   jnp.where(ri == 1, prev[:, 1:2], _per_stream(lambda a: pltpu.roll(a, 2, 0), u)))
    w = cw_ref[...]
    oc_ref[...] = (b_ref[...] * (w[0:1] * u2 + w[1:2] * u1 + w[2:3] * u)).astype(BF16)
    cv_ref[:, 0:1, :] = u[:, L - 2:L - 1]
    cv_ref[:, 1:2, :] = u[:, L - 1:L]


def _recurrent(gate_b, qkvm, og, gt, bch, conv_w, g_ml, s0, m0, cv0, batch, seq, L, G):
    assert batch % G == 0 and seq % L == 0
    nc = seq // L
    tok = lambda j: pl.BlockSpec((G, L, ML_WIDTH), lambda b, c: (b, c, j))
    per_b = lambda shape: pl.BlockSpec((G,) + shape, lambda b, c: (b, 0, 0))
    return pl.pallas_call(
        functools.partial(_recurrent_kernel, L=L),
        grid=(batch // G, nc),
        in_specs=[pl.BlockSpec(memory_space=pltpu.SMEM),
                  tok(0), tok(1), tok(2), tok(0),
                  pl.BlockSpec((G, 1, 8, L), lambda b, c: (b, c, 0, 0)),
                  tok(0), tok(1), tok(2),
                  _const_spec((CONV_K, CONV_CH)), _const_spec((1, ML_WIDTH)),
                  per_b((ML_WIDTH, ML_STATE_W)), per_b((1, 128)), per_b((CONV_K - 1, CONV_CH))],
        out_specs=[tok(0), tok(0),
                   per_b((ML_WIDTH, ML_STATE_W)), per_b((1, 128)), per_b((CONV_K - 1, CONV_CH))],
        out_shape=[jax.ShapeDtypeStruct((batch, seq, ML_WIDTH), BF16),
                   jax.ShapeDtypeStruct((batch, seq, CONV_CH), BF16),
                   jax.ShapeDtypeStruct((batch, ML_WIDTH, ML_STATE_W), F32),
                   jax.ShapeDtypeStruct((batch, 1, 128), F32),
                   jax.ShapeDtypeStruct((batch, CONV_K - 1, CONV_CH), F32)],
        compiler_params=pltpu.CompilerParams(dimension_semantics=("parallel", "arbitrary"),
                                             vmem_limit_bytes=V7X_VMEM_LIMIT),
        name="recurrent",
    )(gate_b, qkvm, qkvm, qkvm, og, gt, bch, bch, bch, conv_w, g_ml, s0, m0, cv0)


def _mem_kv_kernel(x_ref, w_ref, kf_ref, vf_ref, kb_ref, vb_ref):
    x = x_ref[...].astype(BF16)
    n = H_X * D_X
    k = jnp.dot(x, w_ref[:, :n], preferred_element_type=F32)
    v = jnp.dot(x, w_ref[:, n:], preferred_element_type=F32)
    kf_ref[...] = k
    vf_ref[...] = v
    kb_ref[...] = k.astype(BF16)
    vb_ref[...] = v.astype(BF16)


def _mem_kv(mem, w_kv, tm):
    t = mem.shape[0]
    n = H_X * D_X
    row = pl.BlockSpec((tm, n), lambda i: (i, 0))
    return pl.pallas_call(
        _mem_kv_kernel,
        grid=(t // tm,),
        in_specs=[pl.BlockSpec((tm, D_MODEL), lambda i: (i, 0)), _const_spec((D_MODEL, 2 * n))],
        out_specs=[row, row, row, row],
        out_shape=[jax.ShapeDtypeStruct((t, n), F32), jax.ShapeDtypeStruct((t, n), F32),
                   jax.ShapeDtypeStruct((t, n), BF16), jax.ShapeDtypeStruct((t, n), BF16)],
        compiler_params=pltpu.CompilerParams(dimension_semantics=("parallel",),
                                             vmem_limit_bytes=V7X_VMEM_LIMIT),
        name="mem_kv",
    )(mem, w_kv)


FF_CHUNK = 256


def _post_kernel(x_ref, oa_ref, hm_ref, oc_ref, mk_ref, mv_ref, wout_ref, gc_ref, wcq_ref, wco_ref,
                 gf_ref, wg_ref, wu_ref, wd_ref, gfin_ref, o_ref, *, final):
    a, m = ATT_WIDTH, ML_WIDTH
    x = x_ref[...]
    x = x + (jnp.dot(oa_ref[...], wout_ref[0:a, :], preferred_element_type=F32)
             + jnp.dot(hm_ref[...], wout_ref[a:a + m, :], preferred_element_type=F32)
             + jnp.dot(oc_ref[...], wout_ref[a + m:, :], preferred_element_type=F32))

    xn = _rms(x, gc_ref[...]).astype(BF16)
    qc = (jnp.dot(xn, wcq_ref[...], preferred_element_type=F32) * (D_X ** -0.5)).astype(BF16)
    cross = jnp.zeros_like(x)
    n_streams = mk_ref.shape[0]
    rows = x.shape[0] // n_streams
    heads = [slice(h * D_X, (h + 1) * D_X) for h in range(H_X)]
    streams = [slice(b * rows, (b + 1) * rows) for b in range(n_streams)]
    scores = [[lax.dot_general(qc[r, sl], mk_ref[b, :, sl], _NT, preferred_element_type=F32)
               for b, r in enumerate(streams)] for sl in heads]
    for sl, s_head in zip(heads, scores):
        outs = []
        for b, s in enumerate(s_head):
            p = jnp.exp(s - jnp.max(s, axis=1, keepdims=True))
            p = p * (1.0 / jnp.sum(p, axis=1, keepdims=True))
            outs.append(jnp.dot(p.astype(BF16), mv_ref[b, :, sl], preferred_element_type=F32).astype(BF16))
        o = outs[0] if n_streams == 1 else jnp.concatenate(outs, axis=0)
        cross = cross + jnp.dot(o, wco_ref[sl, :], preferred_element_type=F32)
    x = x + cross

    xn = _rms(x, gf_ref[...]).astype(BF16)
    ff = jnp.zeros_like(x)
    for j in range(D_FF // FF_CHUNK):
        sl = slice(j * FF_CHUNK, (j + 1) * FF_CHUNK)
        g = jnp.dot(xn, wg_ref[:, sl], preferred_element_type=F32)
        u = jnp.dot(xn, wu_ref[:, sl], preferred_element_type=F32)
        act = (g * jax.nn.sigmoid(g) * u).astype(BF16)
        ff = ff + jnp.dot(act, wd_ref[sl, :], preferred_element_type=F32)
    x = x + ff
    if final:
        x = _rms(x, gfin_ref[...])
    o_ref[...] = x


def _post(x, oa, hm, oc, mk, mv, w_out, g_cross, w_cq, w_co, g_ffn, w_gate, w_up, w_down, g_final,
          tm, seq, final):
    t = x.shape[0]
    row = lambda n: pl.BlockSpec((tm, n), lambda i: (i, 0))
    if tm <= seq:
        tiles_per_stream = seq // tm
        mem = pl.BlockSpec((1,) + mk.shape[1:], lambda i: (i // tiles_per_stream, 0, 0))
    else:
        mem = pl.BlockSpec((tm // seq,) + mk.shape[1:], lambda i: (i, 0, 0))
    vec = _const_spec((1, D_MODEL))
    return pl.pallas_call(
        functools.partial(_post_kernel, final=final),
        grid=(t // tm,),
        in_specs=[row(D_MODEL), row(ATT_WIDTH), row(ML_WIDTH), row(CONV_CH), mem, mem,
                  _const_spec(w_out.shape), vec, _const_spec(w_cq.shape), _const_spec(w_co.shape),
                  vec, _const_spec(w_gate.shape), _const_spec(w_up.shape), _const_spec(w_down.shape), vec],
        out_specs=row(D_MODEL),
        out_shape=jax.ShapeDtypeStruct((t, D_MODEL), F32),
        compiler_params=pltpu.CompilerParams(dimension_semantics=("parallel",),
                                             vmem_limit_bytes=V7X_VMEM_LIMIT),
        name="post",
    )(x, oa, hm, oc, mk, mv, w_out, g_cross, w_cq, w_co, g_ffn, w_gate, w_up, w_down, g_final)


def _rel_bucket(rel):
    half = NUM_BUCKETS // 2
    max_exact = half // 2
    n = jnp.abs(rel)
    large = max_exact + (jnp.log(jnp.maximum(n, 1).astype(F32) / max_exact)
                         / math.log(MAX_DISTANCE / max_exact) * (half - max_exact)).astype(jnp.int32)
    large = jnp.minimum(large, half - 1)
    return jnp.where(rel > 0, half, 0) + jnp.where(n < max_exact, n, large)


def _bias_table(rel_bias, rel):
    bucket = _rel_bucket(rel)[None]
    table = rel_bias.astype(F32)
    out = jnp.zeros((table.shape[1],) + rel.shape, F32)
    for b in range(NUM_BUCKETS):
        out = jnp.where(bucket == b, table[b][:, None, None], out)
    return out


def _prompt_bias_tiles(rel_bias, t):
    key = jnp.arange(t, dtype=jnp.int32)[:, None]
    qry = jnp.arange(t, dtype=jnp.int32)[None, :]
    far = rel_bias[_rel_bucket(jnp.int32(-MAX_DISTANCE))].astype(F32)[:, None, None]
    prev = (_bias_table(rel_bias, key - t - qry) - far) * LOG2E
    diag = jnp.where((key // CHUNK) <= (qry // CHUNK), (_bias_table(rel_bias, key - qry) - far) * LOG2E, -jnp.inf)
    both_maps = lambda a: jnp.concatenate([a, a], axis=2)
    tiles = jnp.stack([jnp.concatenate([diag, diag], axis=1), jnp.concatenate([prev, diag], axis=1)], axis=1)
    return lax.optimization_barrier(both_maps(tiles.reshape(-1, 2 * t, t)).reshape(-1, 2, 2 * t, 2 * t))


def _block_diag_state(c, n):
    b = c.shape[0]
    eye = jnp.eye(H_M, dtype=F32)
    cbd = jnp.einsum('bhdv,hg->bhdgv', c.astype(F32), eye).reshape(b, ML_WIDTH, ML_WIDTH)
    ncol = jnp.einsum('bhd,hg->bhdg', n.astype(F32), eye).reshape(b, ML_WIDTH, H_M)
    pad = jnp.zeros((b, ML_WIDTH, ML_STATE_W - ML_WIDTH - H_M), F32)
    return jnp.concatenate([cbd, ncol, pad], axis=2)


def _unpack_state(s, m):
    b = s.shape[0]
    blocks = s[:, :, :ML_WIDTH].reshape(b, H_M, D_M, H_M, D_M)
    c = jnp.stack([blocks[:, h, :, h, :] for h in range(H_M)], axis=1)
    ncols = s[:, :, ML_WIDTH:ML_WIDTH + H_M].reshape(b, H_M, D_M, H_M)
    n = jnp.stack([ncols[:, h, :, h] for h in range(H_M)], axis=1)
    return c, n, m[:, 0, :H_M]


def _layer(x, layer, depth, kv_all, wts, attn_fn, mem_k, mem_v, ml_state, conv_prev, g_final,
           batch, seq, tm_proj, L, tm_post):
    final = layer == depth - 1
    (g_mix, w_proj, w_gt, gate_b, conv_w, lam_p, g_att, g_ml, w_out,
     g_cross, w_cq, w_co, g_ffn, w_gate, w_up, w_down) = wts
    t = batch * seq
    lam_init = 0.8 - 0.6 * math.exp(-0.3 * layer)
    qa, kf, kb, vf, vb, qkvm, og, bch, gt = _in_proj(x, g_mix, w_proj, w_gt, tm_proj, kv_all)
    oa = attn_fn(qa, kb, vb, lam_p, g_att, lam_init)
    gt = jnp.transpose(gt.reshape(8, batch, seq // L, L), (1, 2, 0, 3))
    tok3 = lambda a: a.reshape(batch, seq, a.shape[-1])
    hm, oc, s1, m1, cv1 = _recurrent(gate_b, tok3(qkvm), tok3(og), gt, tok3(bch), conv_w, g_ml,
                                     ml_state[0], ml_state[1], conv_prev, batch, seq, L,
                                     min(RECURRENT_STREAMS, batch))
    hm, oc = hm.reshape(t, ML_WIDTH), oc.reshape(t, CONV_CH)
    x = _post(x, oa, hm, oc, mem_k, mem_v, w_out, g_cross, w_cq, w_co, g_ffn, w_gate, w_up, w_down,
              g_final, tm_post, seq, final)
    c1, n1, mm1 = _unpack_state(s1, m1)
    return x, (kf, vf), c1, n1, mm1, cv1


def kernel(x_prompt, x_sample, mem_prompt, cache_att_k, cache_att_v, cache_mem_k, cache_mem_v,
           state_mlstm_C, state_mlstm_n, state_mlstm_m, state_conv,
           norm_mix, w_in, mlstm_gate_bias, conv_w, lambda_params, norm_att_heads, norm_mlstm_heads,
           w_out, norm_cross, w_cq, w_ck, w_cv, w_co, norm_ffn, w_gate, w_up, w_down,
           rel_bias, norm_final):
    bp, sp, _ = x_prompt.shape
    bs, ss, _ = x_sample.shape
    depth = w_in.shape[0]
    past = cache_att_k.shape[2]
    n_mem = mem_prompt.shape[1]
    t_attn = 256
    l_prompt = 128

    xp = x_prompt.reshape(bp * sp, D_MODEL)
    xs = x_sample.reshape(bs * ss, D_MODEL)
    mem = mem_prompt.reshape(bp * n_mem, D_MODEL)
    g_final = norm_final.reshape(1, D_MODEL)

    bias_tiles = _prompt_bias_tiles(rel_bias, t_attn)
    rel_s = jnp.arange(past + ss, dtype=jnp.int32)[None, :] - (past + jnp.arange(ss, dtype=jnp.int32))[:, None]
    bias_s = _bias_table(rel_bias, rel_s) * LOG2E
    bias_s_past, bias_s_new = lax.optimization_barrier((bias_s[:, :, :past], bias_s[:, :, past:]))

    zero_state = (jnp.zeros((bp, ML_WIDTH, ML_STATE_W), F32), jnp.zeros((bp, 1, 128), F32))
    zero_conv = jnp.zeros((bp, CONV_K - 1, CONV_CH), F32)

    outs = {k: [] for k in ('pmk', 'pmv', 'pC', 'pn', 'pm', 'pcv', 'sC', 'sn', 'sm', 'scv')}
    kv_p = kv_s = None
    gate_lo = 3 * ATT_WIDTH + 4 * ML_WIDTH
    for l in range(depth):
        w = w_in[l]
        wts = (norm_mix[l].reshape(1, D_MODEL),
               jnp.concatenate([w[:, :gate_lo], w[:, gate_lo + 2 * H_M:]], axis=1).astype(BF16),
               w[:, gate_lo:gate_lo + 2 * H_M].T.astype(BF16),
               mlstm_gate_bias[l].astype(F32), conv_w[l].astype(F32), lambda_params[l].astype(F32),
               norm_att_heads[l].reshape(1, ATT_WIDTH), norm_mlstm_heads[l].reshape(1, ML_WIDTH),
               w_out[l].astype(BF16), norm_cross[l].reshape(1, D_MODEL),
               w_cq[l].astype(BF16), w_co[l].astype(BF16), norm_ffn[l].reshape(1, D_MODEL),
               w_gate[l].astype(BF16), w_up[l].astype(BF16), w_down[l].astype(BF16))

        w_kv = jnp.concatenate([w_ck[l], w_cv[l]], axis=1).astype(BF16)
        mkf, mvf, mkb, mvb = _mem_kv(mem, w_kv, 512)
        attn_p = lambda q, k, v, lam_p, g, li: _attn_prompt(q, k, v, bias_tiles, lam_p, g, bp, sp, t_attn, li)
        xp, kv_p, c1, n1, m1, cv1 = _layer(
            xp, l, depth, kv_p, wts, attn_p, mkb.reshape(bp, n_mem, -1), mvb.reshape(bp, n_mem, -1),
            zero_state, zero_conv, g_final, bp, sp, 512, l_prompt, 512)
        outs['pmk'].append(mkf.reshape(bp, n_mem, H_X, D_X))
        outs['pmv'].append(mvf.reshape(bp, n_mem, H_X, D_X))
        outs['pC'].append(c1); outs['pn'].append(n1); outs['pm'].append(m1); outs['pcv'].append(cv1)

        k_past = cache_att_k[l].reshape(bs, past, ATT_WIDTH)
        v_past = cache_att_v[l].reshape(bs, past, ATT_WIDTH)
        attn_s = lambda q, k, v, lam_p, g, li: _attn_sample(q, k, v, k_past, v_past, bias_s_past, bias_s_new,
                                                            lam_p, g, bs, ss, li)
        s0 = _block_diag_state(state_mlstm_C[l], state_mlstm_n[l])
        m0 = jnp.pad(state_mlstm_m[l].astype(F32), ((0, 0), (0, 128 - H_M))).reshape(bs, 1, 128)
        xs, kv_s, c1, n1, m1, cv1 = _layer(
            xs, l, depth, kv_s, wts, attn_s,
            cache_mem_k[l].reshape(bs, n_mem, -1).astype(BF16), cache_mem_v[l].reshape(bs, n_mem, -1).astype(BF16),
            (s0, m0), state_conv[l].astype(F32), g_final, bs, ss, bs * ss, ss, bs * ss)
        outs['sC'].append(c1); outs['sn'].append(n1); outs['sm'].append(m1); outs['scv'].append(cv1)

    st = lambda k: jnp.stack(outs[k])
    per_head = lambda a, b, s: a.reshape(depth, b, s, H_A, DV_A)
    return (xp.reshape(bp, sp, D_MODEL), xs.reshape(bs, ss, D_MODEL),
            per_head(kv_p[0], bp, sp), per_head(kv_p[1], bp, sp),
            st('pmk'), st('pmv'), st('pC'), st('pn'), st('pm'), st('pcv'),
            per_head(kv_s[0], bs, ss), per_head(kv_s[1], bs, ss),
            st('sC'), st('sn'), st('sm'), st('scv'))
```

```python
import functools
import math

import jax
import jax.numpy as jnp
from jax import lax
from jax.experimental import pallas as pl
from jax.experimental.pallas import tpu as pltpu

F32 = jnp.float32
BF16 = jnp.bfloat16

D_MODEL = 1024
CHUNK = 64
H_A = 4
D_A = 64
DV_A = 128
ATT_WIDTH = H_A * DV_A
H_M = 4
D_M = 64
ML_WIDTH = H_M * D_M
CONV_CH = 256
CONV_K = 3
H_X = 4
D_X = 256
D_FF = 2816
NUM_BUCKETS = 32
MAX_DISTANCE = 128
EPS = 1e-6

N_PROJ = 3 * ATT_WIDTH + 4 * ML_WIDTH + 3 * CONV_CH
ML_STATE_W = ML_WIDTH + 128

V7X_VMEM_LIMIT = 56 * 1024 * 1024

LOG2E = math.log2(math.e)
RECURRENT_STREAMS = 8
ATTN_HEADS_PER_STEP = 4

_NT = (((1,), (1,)), ((), ()))
_TN = (((0,), (0,)), ((), ()))


def _const_spec(shape):
    nd = len(shape)
    return pl.BlockSpec(shape, lambda *_: (0,) * nd, pipeline_mode=pl.Buffered(1))


def _rms(x, g):
    ms = jnp.mean(x * x, axis=-1, keepdims=True)
    return x * lax.rsqrt(ms + EPS) * g


def _in_proj_kernel(x_ref, g_ref, w_ref, wgt_ref, *refs):
    qa_ref, kf_ref, kb_ref, vf_ref, vb_ref, qkvm_ref, om_ref, bch_ref, gt_ref = refs[-9:]
    n_prev = kf_ref.shape[0] - 1
    if n_prev:
        kf_ref[0:n_prev] = refs[0][...]
        vf_ref[0:n_prev] = refs[1][...]
    xn = _rms(x_ref[...], g_ref[...]).astype(BF16)

    def proj(lo, hi):
        return jnp.dot(xn, w_ref[:, lo:hi], preferred_element_type=F32)

    a = ATT_WIDTH
    qa_ref[...] = (proj(0, a) * (D_A ** -0.5 * LOG2E)).astype(BF16)
    k = proj(a, 2 * a)
    kb_ref[...] = k.astype(BF16)
    v = proj(2 * a, 3 * a)
    for h in range(H_A):
        kf_ref[n_prev, :, h, :] = k[:, h * DV_A:(h + 1) * DV_A]
        vf_ref[n_prev, :, h, :] = v[:, h * DV_A:(h + 1) * DV_A]
    vb_ref[...] = v.astype(BF16)
    o = 3 * a
    m = ML_WIDTH
    qkvm_ref[:, 0:m] = proj(o, o + m).astype(BF16)
    qkvm_ref[:, m:2 * m] = (proj(o + m, o + 2 * m) * (D_M ** -0.5)).astype(BF16)
    qkvm_ref[:, 2 * m:3 * m] = proj(o + 2 * m, o + 3 * m).astype(BF16)
    om_ref[...] = proj(o + 3 * m, o + 4 * m)
    bch_ref[...] = proj(o + 4 * m, N_PROJ)
    gt_ref[...] = lax.dot_general(wgt_ref[...], xn, _NT, preferred_element_type=F32)


def _in_proj(x, g, w, wgt, tm, kv_prev):
    t = x.shape[0]
    a, m = ATT_WIDTH, ML_WIDTH
    n_prev = 0 if kv_prev is None else kv_prev[0].shape[0]
    row = lambda n: pl.BlockSpec((tm, n), lambda i: (i, 0))
    slabs = lambda n: pl.BlockSpec((n, tm, H_A, DV_A), lambda i: (0, i, 0, 0))
    outs = [
        (jax.ShapeDtypeStruct((t, a), BF16), row(a)),
        (jax.ShapeDtypeStruct((n_prev + 1, t, H_A, DV_A), F32), slabs(n_prev + 1)),
        (jax.ShapeDtypeStruct((t, a), BF16), row(a)),
        (jax.ShapeDtypeStruct((n_prev + 1, t, H_A, DV_A), F32), slabs(n_prev + 1)),
        (jax.ShapeDtypeStruct((t, a), BF16), row(a)),
        (jax.ShapeDtypeStruct((t, 3 * m), BF16), row(3 * m)),
        (jax.ShapeDtypeStruct((t, m), F32), row(m)),
        (jax.ShapeDtypeStruct((t, 3 * CONV_CH), F32), row(3 * CONV_CH)),
        (jax.ShapeDtypeStruct((8, t), F32), pl.BlockSpec((8, tm), lambda i: (0, i))),
    ]
    carried = [] if kv_prev is None else list(kv_prev)
    return pl.pallas_call(
        _in_proj_kernel,
        grid=(t // tm,),
        in_specs=[row(D_MODEL), _const_spec((1, D_MODEL)), _const_spec((D_MODEL, N_PROJ)),
                  _const_spec((8, D_MODEL))] + [slabs(n_prev)] * len(carried),
        out_specs=[s for _, s in outs],
        out_shape=[s for s, _ in outs],
        compiler_params=pltpu.CompilerParams(dimension_semantics=("parallel",),
                                             vmem_limit_bytes=V7X_VMEM_LIMIT),
        name="in_proj",
    )(x, g, w, wgt, *carried)


def _lambda_value(lp, lam_init):
    a = jnp.sum(lp[0:1] * lp[1:2], axis=1, keepdims=True)
    b = jnp.sum(lp[2:3] * lp[3:4], axis=1, keepdims=True)
    return jnp.exp(a) - jnp.exp(b) + lam_init


def _stack_maps(q):
    lane = lax.broadcasted_iota(jnp.int32, q.shape, 1)
    zero = jnp.zeros_like(q)
    return jnp.concatenate([jnp.where(lane < D_A, q, zero), jnp.where(lane >= D_A, q, zero)], axis=0)


def _attn_finish(acc, lam, g, lam_init, t):
    o = acc[:, :DV_A] / acc[:, DV_A:]
    d = o[:t] - lam * o[t:]
    return (_rms(d, g) * (1.0 - lam_init)).astype(BF16)


def _with_ones(v):
    return jnp.concatenate([v, jnp.ones_like(v)], axis=1)


def _attn_prompt_kernel(lam_ref, g_ref, q_ref, k_ref, v_ref, bias_ref, o_ref,
                        m_scr, l_scr, acc_scr, sa_scr, sb_scr, p_scr, *, t, hp, lam_init):
    i = pl.program_id(2)
    heads = [slice(hh * DV_A, (hh + 1) * DV_A) for hh in range(hp)]
    cols = [slice(hh * 2 * t, (hh + 1) * 2 * t) for hh in range(hp)]
    qz = [_stack_maps(q_ref[:, hd]) for hd in heads]

    def rows(kb):
        return pl.ds(pl.multiple_of(kb * t, t), t)

    def qk(kb):
        r = rows(kb)
        return jnp.concatenate([lax.dot_general(k_ref[r, hd], z, _NT, preferred_element_type=F32)
                                for hd, z in zip(heads, qz)], axis=1)

    def pv(p, kb):
        r = rows(kb)
        return jnp.concatenate([lax.dot_general(v_ref[r, hd], p[:, c], _TN, preferred_element_type=F32)
                                for hd, c in zip(heads, cols)], axis=1)

    def bias(lo):
        return jnp.concatenate([bias_ref[hh, 0, lo:lo + t, :] for hh in range(hp)], axis=1)

    def softmax_step(m_prev, s):
        m_new = jnp.maximum(m_prev, jnp.max(s, axis=0, keepdims=True))
        p = jnp.exp2(s - m_new)
        return m_new, jnp.exp2(m_prev - m_new), p, jnp.sum(p, axis=0, keepdims=True)

    n_far = jnp.maximum(i - 1, 0)
    last = n_far - 1
    kb_prev = jnp.maximum(i - 1, 0)

    s_diag = qk(i)
    s_prev = qk(kb_prev)
    sa_scr[...] = qk(0)
    s_diag = s_diag + bias(t)
    m = jnp.max(s_diag, axis=0, keepdims=True)
    p = jnp.exp2(s_diag - m)
    l = jnp.sum(p, axis=0, keepdims=True)
    acc = pv(p.astype(BF16), i)
    w_prev = (i >= 1).astype(F32)
    m, alpha, p, p_sum = softmax_step(m, s_prev + bias(0))
    m_scr[...] = m
    l_scr[...] = alpha * l + w_prev * p_sum
    acc_scr[...] = alpha * acc
    p_scr[...] = p.astype(BF16)

    def far_pair(j, carry):
        kb1 = jnp.minimum(2 * j + 1, last)
        w1 = (2 * j + 1 <= last).astype(F32)
        kb_pending = jnp.where(j == 0, kb_prev, 2 * j - 1)
        sb_scr[...] = qk(kb1)
        owed = pv(p_scr[...], kb_pending)
        m, alpha, p, p_sum = softmax_step(m_scr[...], sa_scr[...])
        l = alpha * l_scr[...] + p_sum
        acc = alpha * (acc_scr[...] + owed)
        sa_scr[...] = qk(jnp.minimum(2 * j + 2, last))
        owed = pv(p.astype(BF16), 2 * j)
        m, alpha, p, p_sum = softmax_step(m, sb_scr[...])
        m_scr[...] = m
        l_scr[...] = alpha * l + w1 * p_sum
        acc_scr[...] = alpha * (acc + owed)
        p_scr[...] = p.astype(BF16)
        return carry

    trips = (n_far + 1) // 2
    lax.fori_loop(0, trips, far_pair, 0)

    kb_pending = jnp.where(trips == 0, kb_prev, jnp.minimum(2 * trips - 1, last))
    w_pending = jnp.where(trips == 0, w_prev, (n_far % 2 == 0).astype(F32))
    acc = acc_scr[...] + w_pending * pv(p_scr[...], kb_pending)

    lam = _lambda_value(lam_ref[...], lam_init)
    o = acc * (1.0 / l_scr[...])
    for hd, c in zip(heads, cols):
        d = o[:, c][:, :t] - lam * o[:, c][:, t:]
        ms = jnp.mean(d * d, axis=0, keepdims=True)
        y = jnp.transpose(d * lax.rsqrt(ms + EPS))
        o_ref[:, hd] = (y * g_ref[:, hd] * (1.0 - lam_init)).astype(BF16)


def _attn_prompt(q, k, v, bias, lam_p, g_att, batch, seq, t, lam_init):
    nq = seq // t
    hp = ATTN_HEADS_PER_STEP
    w = hp * 2 * t
    kv_spec = pl.BlockSpec((seq, hp * DV_A), lambda b, h, i: (b, h))
    return pl.pallas_call(
        functools.partial(_attn_prompt_kernel, t=t, hp=hp, lam_init=lam_init),
        grid=(batch, H_A // hp, nq),
        in_specs=[_const_spec((4, D_A)),
                  pl.BlockSpec((1, hp * DV_A), lambda b, h, i: (0, h)),
                  pl.BlockSpec((t, hp * DV_A), lambda b, h, i: (b * nq + i, h)),
                  kv_spec, kv_spec,
                  pl.BlockSpec((hp, 1, 2 * t, 2 * t), lambda b, h, i: (h, jnp.minimum(i, 1), 0, 0))],
        out_specs=pl.BlockSpec((t, hp * DV_A), lambda b, h, i: (b * nq + i, h)),
        out_shape=jax.ShapeDtypeStruct((batch * seq, ATT_WIDTH), BF16),
        scratch_shapes=[pltpu.VMEM((1, w), F32), pltpu.VMEM((1, w), F32),
                        pltpu.VMEM((DV_A, w), F32),
                        pltpu.VMEM((t, w), F32), pltpu.VMEM((t, w), F32),
                        pltpu.VMEM((t, w), BF16)],
        compiler_params=pltpu.CompilerParams(dimension_semantics=("parallel", "parallel", "arbitrary"),
                                             vmem_limit_bytes=V7X_VMEM_LIMIT),
        name="attn_prompt",
    )(lam_p, g_att, q, k, v, bias)


def _attn_sample_kernel(lam_ref, g_ref, q_ref, kp_ref, vp_ref, kn_ref, vn_ref, bp_ref, bn_ref, o_ref,
                        *, lq, lam_init):
    lam = _lambda_value(lam_ref[...], lam_init)
    for h in range(H_A):
        hd = slice(h * DV_A, (h + 1) * DV_A)
        qz = _stack_maps(q_ref[:, hd])
        bp = bp_ref[h]
        bn = bn_ref[h]
        sp = lax.dot_general(qz, kp_ref[0, :, h, :].astype(BF16), _NT, preferred_element_type=F32)
        sp = sp + jnp.concatenate([bp, bp], axis=0)
        sn = lax.dot_general(qz, kn_ref[:, hd], _NT, preferred_element_type=F32)
        sn = sn + jnp.concatenate([bn, bn], axis=0)
        m = jnp.maximum(jnp.max(sp, axis=1, keepdims=True), jnp.max(sn, axis=1, keepdims=True))
        pp = jnp.exp2(sp - m)
        pn = jnp.exp2(sn - m)
        acc = (jnp.dot(pp.astype(BF16), _with_ones(vp_ref[0, :, h, :].astype(BF16)), preferred_element_type=F32)
               + jnp.dot(pn.astype(BF16), _with_ones(vn_ref[:, hd]), preferred_element_type=F32))
        o_ref[:, hd] = _attn_finish(acc, lam, g_ref[:, hd], lam_init, lq)


def _attn_sample(q, k_new, v_new, k_cache, v_cache, layer, bias_past, bias_new, lam_p, g_att, batch, lq, lam_init):
    past = k_cache.shape[2]
    new_spec = pl.BlockSpec((lq, ATT_WIDTH), lambda b: (b, 0))
    past_spec = pl.BlockSpec((None, 1, past, H_A, DV_A), lambda b: (layer, b, 0, 0, 0))
    return pl.pallas_call(
        functools.partial(_attn_sample_kernel, lq=lq, lam_init=lam_init),
        grid=(batch,),
        in_specs=[_const_spec((4, D_A)), _const_spec((1, ATT_WIDTH)),
                  new_spec, past_spec, past_spec, new_spec, new_spec,
                  _const_spec((H_A, lq, past)), _const_spec((H_A, lq, lq))],
        out_specs=new_spec,
        out_shape=jax.ShapeDtypeStruct((batch * lq, ATT_WIDTH), BF16),
        compiler_params=pltpu.CompilerParams(dimension_semantics=("parallel",),
                                             vmem_limit_bytes=V7X_VMEM_LIMIT),
        name="attn_sample",
    )(lam_p, g_att, q, k_cache, v_cache, k_new, v_new, bias_past, bias_new)


def _per_stream(fn, *arrays):
    return jnp.stack([fn(*(a[g] for a in arrays)) for g in range(arrays[0].shape[0])])


def _recurrent_kernel(gb_ref, q_ref, k_ref, v_ref, og_ref, gt_ref, b_ref, c_ref, hc_ref, cw_ref, gml_ref,
                      s0_ref, m0_ref, cv0_ref,
                      hm_ref, oc_ref, s_ref, m_ref, cv_ref, *, L):
    @pl.when(pl.program_id(1) == 0)
    def _():
        s_ref[...] = s0_ref[...]
        m_ref[...] = m0_ref[...]
        cv_ref[...] = cv0_ref[...]

    q = q_ref[...]
    k = k_ref[...]
    v = v_ref[...]
    G = q.shape[0]
    state = s_ref[...]
    q_state = _per_stream(lambda a, b: jnp.dot(a, b, preferred_element_type=F32),
                          q, state.astype(BF16))
    q_c = q_state[:, :, :ML_WIDTH]

    lane_head = lax.broadcasted_iota(jnp.int32, (1, L, ML_WIDTH), 2) // D_M
    row = lax.broadcasted_iota(jnp.int32, (1, L, L), 1)
    col = lax.broadcasted_iota(jnp.int32, (1, L, L), 2)
    causal = col <= row
    eye = col == row
    st_lane = lax.broadcasted_iota(jnp.int32, (1, 1, ML_STATE_W), 2)
    st_lane_head = jnp.where(st_lane < ML_WIDTH, st_lane // D_M, st_lane - ML_WIDTH)
    m_lane = lax.broadcasted_iota(jnp.int32, (1, 1, 128), 2)

    gt = gt_ref[:, 0]
    m_all = m_ref[...]
    num = jnp.zeros((G, L, ML_WIDTH), F32)
    w_state = jnp.zeros((G, L, ML_WIDTH), F32)
    decay = jnp.zeros((G, 1, ML_STATE_W), F32)
    m_next = jnp.zeros((G, 1, 128), F32)
    for h in range(H_M):
        ig = gt[:, h:h + 1, :] + gb_ref[0, h]
        fz = gt[:, H_M + h:H_M + h + 1, :] + gb_ref[1, h]
        lf = jnp.minimum(fz, 0.0) - jnp.log1p(jnp.exp(-jnp.abs(fz)))
        f_col = jnp.sum(jnp.where(causal, lf, 0.0), axis=2, keepdims=True)
        f_row = jnp.sum(jnp.where(eye, f_col, 0.0), axis=1, keepdims=True)
        d = jnp.where(causal, f_col + (ig - f_row), -jnp.inf)
        m_prev = m_all[:, :, h:h + 1]
        inter = f_col + m_prev
        mt = jnp.maximum(inter, jnp.max(d, axis=2, keepdims=True))
        w_intra = jnp.exp(d - mt)
        w_inter = jnp.exp(inter - mt)
        head = lane_head == h
        qk = _per_stream(lambda a, b: lax.dot_general(a, b, _NT, preferred_element_type=F32),
                         jnp.where(head, q, jnp.zeros_like(q)), k)
        s = qk * w_intra
        den = jnp.sum(s, axis=2, keepdims=True) + w_inter * q_state[:, :, ML_WIDTH + h:ML_WIDTH + h + 1]
        inv = 1.0 / jnp.maximum(jnp.abs(den), jnp.exp(-mt))
        sv = _per_stream(lambda a, b: jnp.dot(a, b, preferred_element_type=F32), s.astype(BF16), v)
        num = jnp.where(head, (sv + w_inter * q_c) * inv, num)
        m_new = mt[:, L - 1:L, :]
        w_col = jnp.sum(jnp.where(eye, w_intra[:, L - 1:L, :], 0.0), axis=2, keepdims=True)
        w_state = jnp.where(head, w_col, w_state)
        dec = jnp.exp(f_col[:, L - 1:L, :] + m_prev - m_new)
        decay = jnp.where(st_lane_head == h, dec, decay)
        m_next = jnp.where(m_lane == h, m_new, m_next)

    kw = (k.astype(F32) * w_state).astype(BF16)
    v_aug = jnp.concatenate([v, jnp.ones((G, L, ML_STATE_W - ML_WIDTH), BF16)], axis=2)
    upd = _per_stream(lambda a, b: lax.dot_general(a, b, _TN, preferred_element_type=F32), kw, v_aug)
    s_row_head = lax.broadcasted_iota(jnp.int32, (1, ML_WIDTH, ML_STATE_W), 1) // D_M
    s_col = lax.broadcasted_iota(jnp.int32, (1, ML_WIDTH, ML_STATE_W), 2)
    s_col_head = jnp.where(s_col < ML_WIDTH, s_col // D_M, s_col - ML_WIDTH)
    s_ref[...] = decay * state + jnp.where(s_row_head == s_col_head, upd, 0.0)
    m_ref[...] = m_next

    hg = num * jax.nn.sigmoid(og_ref[...])
    scale = jnp.zeros((G, L, ML_WIDTH), F32)
    for h in range(H_M):
        head = lane_head == h
        ss = jnp.sum(jnp.where(head, hg * hg, 0.0), axis=2, keepdims=True)
        scale = jnp.where(head, lax.rsqrt(ss * (1.0 / D_M) + EPS), scale)
    hm_ref[...] = (hg * scale * gml_ref[...]).astype(BF16)

    u = c_ref[...] * hc_ref[...]
    prev = cv_ref[...]
    ri = lax.broadcasted_iota(jnp.int32, (1, L, CONV_CH), 1)
    u1 = jnp.where(ri == 0, prev[:, 1:2], _per_stream(lambda a: pltpu.roll(a, 1, 0), u))
    u2 = jnp.where(ri == 0, prev[:, 0:1],
                   jnp.where(ri == 1, prev[:, 1:2], _per_stream(lambda a: pltpu.roll(a, 2, 0), u)))
    w = cw_ref[...]
    oc_ref[...] = (b_ref[...] * (w[0:1] * u2 + w[1:2] * u1 + w[2:3] * u)).astype(BF16)
    cv_ref[:, 0:1, :] = u[:, L - 2:L - 1]
    cv_ref[:, 1:2, :] = u[:, L - 1:L]


def _recurrent(gate_b, qkvm, og, gt, bch, conv_w, g_ml, s0, m0, cv0, batch, seq, L, G):
    assert batch % G == 0 and seq % L == 0
    nc = seq // L
    tok = lambda j: pl.BlockSpec((G, L, ML_WIDTH), lambda b, c: (b, c, j))
    per_b = lambda shape: pl.BlockSpec((G,) + shape, lambda b, c: (b, 0, 0))
    return pl.pallas_call(
        functools.partial(_recurrent_kernel, L=L),
        grid=(batch // G, nc),
        in_specs=[pl.BlockSpec(memory_space=pltpu.SMEM),
                  tok(0), tok(1), tok(2), tok(0),
                  pl.BlockSpec((G, 1, 8, L), lambda b, c: (b, c, 0, 0)),
                  tok(0), tok(1), tok(2),
                  _const_spec((CONV_K, CONV_CH)), _const_spec((1, ML_WIDTH)),
                  per_b((ML_WIDTH, ML_STATE_W)), per_b((1, 128)), per_b((CONV_K - 1, CONV_CH))],
        out_specs=[tok(0), tok(0),
                   per_b((ML_WIDTH, ML_STATE_W)), per_b((1, 128)), per_b((CONV_K - 1, CONV_CH))],
        out_shape=[jax.ShapeDtypeStruct((batch, seq, ML_WIDTH), BF16),
                   jax.ShapeDtypeStruct((batch, seq, CONV_CH), BF16),
                   jax.ShapeDtypeStruct((batch, ML_WIDTH, ML_STATE_W), F32),
                   jax.ShapeDtypeStruct((batch, 1, 128), F32),
                   jax.ShapeDtypeStruct((batch, CONV_K - 1, CONV_CH), F32)],
        compiler_params=pltpu.CompilerParams(dimension_semantics=("parallel", "arbitrary"),
                                             vmem_limit_bytes=V7X_VMEM_LIMIT),
        name="recurrent",
    )(gate_b, qkvm, qkvm, qkvm, og, gt, bch, bch, bch, conv_w, g_ml, s0, m0, cv0)


def _mem_kv_kernel(x_ref, w_ref, kf_ref, vf_ref, kb_ref, vb_ref):
    x = x_ref[...].astype(BF16)
    n = H_X * D_X
    k = jnp.dot(x, w_ref[:, :n], preferred_element_type=F32)
    v = jnp.dot(x, w_ref[:, n:], preferred_element_type=F32)
    kf_ref[...] = k
    vf_ref[...] = v
    kb_ref[...] = k.astype(BF16)
    vb_ref[...] = v.astype(BF16)


def _mem_kv(mem, w_kv, tm):
    t = mem.shape[0]
    n = H_X * D_X
    row = pl.BlockSpec((tm, n), lambda i: (i, 0))
    return pl.pallas_call(
        _mem_kv_kernel,
        grid=(t // tm,),
        in_specs=[pl.BlockSpec((tm, D_MODEL), lambda i: (i, 0)), _const_spec((D_MODEL, 2 * n))],
        out_specs=[row, row, row, row],
        out_shape=[jax.ShapeDtypeStruct((t, n), F32), jax.ShapeDtypeStruct((t, n), F32),
                   jax.ShapeDtypeStruct((t, n), BF16), jax.ShapeDtypeStruct((t, n), BF16)],
        compiler_params=pltpu.CompilerParams(dimension_semantics=("parallel",),
                                             vmem_limit_bytes=V7X_VMEM_LIMIT),
        name="mem_kv",
    )(mem, w_kv)


FF_CHUNK = 256


def _post_kernel(x_ref, oa_ref, hm_ref, oc_ref, mk_ref, mv_ref, wout_ref, gc_ref, wcq_ref, wco_ref,
                 gf_ref, wg_ref, wu_ref, wd_ref, gfin_ref, o_ref, *, final):
    a, m = ATT_WIDTH, ML_WIDTH
    x = x_ref[...]
    x = x + (jnp.dot(oa_ref[...], wout_ref[0:a, :], preferred_element_type=F32)
             + jnp.dot(hm_ref[...], wout_ref[a:a + m, :], preferred_element_type=F32)
             + jnp.dot(oc_ref[...], wout_ref[a + m:, :], preferred_element_type=F32))

    xn = _rms(x, gc_ref[...]).astype(BF16)
    qc = (jnp.dot(xn, wcq_ref[...], preferred_element_type=F32) * (D_X ** -0.5)).astype(BF16)
    cross = jnp.zeros_like(x)
    n_streams = mk_ref.shape[0]
    rows = x.shape[0] // n_streams
    heads = [slice(h * D_X, (h + 1) * D_X) for h in range(H_X)]
    streams = [slice(b * rows, (b + 1) * rows) for b in range(n_streams)]
    scores = [[lax.dot_general(qc[r, sl], mk_ref[b, :, sl], _NT, preferred_element_type=F32)
               for b, r in enumerate(streams)] for sl in heads]
    for sl, s_head in zip(heads, scores):
        outs = []
        for b, s in enumerate(s_head):
            p = jnp.exp(s - jnp.max(s, axis=1, keepdims=True))
            p = p * (1.0 / jnp.sum(p, axis=1, keepdims=True))
            outs.append(jnp.dot(p.astype(BF16), mv_ref[b, :, sl], preferred_element_type=F32).astype(BF16))
        o = outs[0] if n_streams == 1 else jnp.concatenate(outs, axis=0)
        cross = cross + jnp.dot(o, wco_ref[sl, :], preferred_element_type=F32)
    x = x + cross

    xn = _rms(x, gf_ref[...]).astype(BF16)
    ff = jnp.zeros_like(x)
    for j in range(D_FF // FF_CHUNK):
        sl = slice(j * FF_CHUNK, (j + 1) * FF_CHUNK)
        g = jnp.dot(xn, wg_ref[:, sl], preferred_element_type=F32)
        u = jnp.dot(xn, wu_ref[:, sl], preferred_element_type=F32)
        act = (g * jax.nn.sigmoid(g) * u).astype(BF16)
        ff = ff + jnp.dot(act, wd_ref[sl, :], preferred_element_type=F32)
    x = x + ff
    if final:
        x = _rms(x, gfin_ref[...])
    o_ref[...] = x


def _post(x, oa, hm, oc, mk, mv, w_out, g_cross, w_cq, w_co, g_ffn, w_gate, w_up, w_down, g_final,
          tm, seq, final):
    t = x.shape[0]
    row = lambda n: pl.BlockSpec((tm, n), lambda i: (i, 0))
    if tm <= seq:
        tiles_per_stream = seq // tm
        mem = pl.BlockSpec((1,) + mk.shape[1:], lambda i: (i // tiles_per_stream, 0, 0))
    else:
        mem = pl.BlockSpec((tm // seq,) + mk.shape[1:], lambda i: (i, 0, 0))
    vec = _const_spec((1, D_MODEL))
    return pl.pallas_call(
        functools.partial(_post_kernel, final=final),
        grid=(t // tm,),
        in_specs=[row(D_MODEL), row(ATT_WIDTH), row(ML_WIDTH), row(CONV_CH), mem, mem,
                  _const_spec(w_out.shape), vec, _const_spec(w_cq.shape), _const_spec(w_co.shape),
                  vec, _const_spec(w_gate.shape), _const_spec(w_up.shape), _const_spec(w_down.shape), vec],
        out_specs=row(D_MODEL),
        out_shape=jax.ShapeDtypeStruct((t, D_MODEL), F32),
        compiler_params=pltpu.CompilerParams(dimension_semantics=("parallel",),
                                             vmem_limit_bytes=V7X_VMEM_LIMIT),
        name="post",
    )(x, oa, hm, oc, mk, mv, w_out, g_cross, w_cq, w_co, g_ffn, w_gate, w_up, w_down, g_final)


def _rel_bucket(rel):
    half = NUM_BUCKETS // 2
    max_exact = half // 2
    n = jnp.abs(rel)
    large = max_exact + (jnp.log(jnp.maximum(n, 1).astype(F32) / max_exact)
                         / math.log(MAX_DISTANCE / max_exact) * (half - max_exact)).astype(jnp.int32)
    large = jnp.minimum(large, half - 1)
    return jnp.where(rel > 0, half, 0) + jnp.where(n < max_exact, n, large)


def _bias_table(rel_bias, rel):
    bucket = _rel_bucket(rel)[None]
    table = rel_bias.astype(F32)
    out = jnp.zeros((table.shape[1],) + rel.shape, F32)
    for b in range(NUM_BUCKETS):
        out = jnp.where(bucket == b, table[b][:, None, None], out)
    return out


def _prompt_bias_tiles(rel_bias, t):
    key = jnp.arange(t, dtype=jnp.int32)[:, None]
    qry = jnp.arange(t, dtype=jnp.int32)[None, :]
    far = rel_bias[_rel_bucket(jnp.int32(-MAX_DISTANCE))].astype(F32)[:, None, None]
    prev = (_bias_table(rel_bias, key - t - qry) - far) * LOG2E
    diag = jnp.where((key // CHUNK) <= (qry // CHUNK), (_bias_table(rel_bias, key - qry) - far) * LOG2E, -jnp.inf)
    both_maps = lambda a: jnp.concatenate([a, a], axis=2)
    tiles = jnp.stack([jnp.concatenate([diag, diag], axis=1), jnp.concatenate([prev, diag], axis=1)], axis=1)
    return lax.optimization_barrier(both_maps(tiles.reshape(-1, 2 * t, t)).reshape(-1, 2, 2 * t, 2 * t))


def _block_diag_state(c, n):
    b = c.shape[0]
    eye = jnp.eye(H_M, dtype=F32)
    cbd = jnp.einsum('bhdv,hg->bhdgv', c.astype(F32), eye).reshape(b, ML_WIDTH, ML_WIDTH)
    ncol = jnp.einsum('bhd,hg->bhdg', n.astype(F32), eye).reshape(b, ML_WIDTH, H_M)
    pad = jnp.zeros((b, ML_WIDTH, ML_STATE_W - ML_WIDTH - H_M), F32)
    return jnp.concatenate([cbd, ncol, pad], axis=2)


def _unpack_state(s, m):
    b = s.shape[0]
    blocks = s[:, :, :ML_WIDTH].reshape(b, H_M, D_M, H_M, D_M)
    c = jnp.stack([blocks[:, h, :, h, :] for h in range(H_M)], axis=1)
    ncols = s[:, :, ML_WIDTH:ML_WIDTH + H_M].reshape(b, H_M, D_M, H_M)
    n = jnp.stack([ncols[:, h, :, h] for h in range(H_M)], axis=1)
    return c, n, m[:, 0, :H_M]


def _layer(x, layer, depth, kv_all, wts, attn_fn, mem_k, mem_v, ml_state, conv_prev, g_final,
           batch, seq, tm_proj, L, tm_post):
    final = layer == depth - 1
    (g_mix, w_proj, w_gt, gate_b, conv_w, lam_p, g_att, g_ml, w_out,
     g_cross, w_cq, w_co, g_ffn, w_gate, w_up, w_down) = wts
    t = batch * seq
    lam_init = 0.8 - 0.6 * math.exp(-0.3 * layer)
    qa, kf, kb, vf, vb, qkvm, og, bch, gt = _in_proj(x, g_mix, w_proj, w_gt, tm_proj, kv_all)
    oa = attn_fn(qa, kb, vb, lam_p, g_att, lam_init)
    gt = jnp.transpose(gt.reshape(8, batch, seq // L, L), (1, 2, 0, 3))
    tok3 = lambda a: a.reshape(batch, seq, a.shape[-1])
    hm, oc, s1, m1, cv1 = _recurrent(gate_b, tok3(qkvm), tok3(og), gt, tok3(bch), conv_w, g_ml,
                                     ml_state[0], ml_state[1], conv_prev, batch, seq, L,
                                     min(RECURRENT_STREAMS, batch))
    hm, oc = hm.reshape(t, ML_WIDTH), oc.reshape(t, CONV_CH)
    x = _post(x, oa, hm, oc, mem_k, mem_v, w_out, g_cross, w_cq, w_co, g_ffn, w_gate, w_up, w_down,
              g_final, tm_post, seq, final)
    c1, n1, mm1 = _unpack_state(s1, m1)
    return x, (kf, vf), c1, n1, mm1, cv1


def kernel(x_prompt, x_sample, mem_prompt, cache_att_k, cache_att_v, cache_mem_k, cache_mem_v,
           state_mlstm_C, state_mlstm_n, state_mlstm_m, state_conv,
           norm_mix, w_in, mlstm_gate_bias, conv_w, lambda_params, norm_att_heads, norm_mlstm_heads,
           w_out, norm_cross, w_cq, w_ck, w_cv, w_co, norm_ffn, w_gate, w_up, w_down,
           rel_bias, norm_final):
    bp, sp, _ = x_prompt.shape
    bs, ss, _ = x_sample.shape
    depth = w_in.shape[0]
    past = cache_att_k.shape[2]
    n_mem = mem_prompt.shape[1]
    t_attn = 256
    l_prompt = 128

    xp = x_prompt.reshape(bp * sp, D_MODEL)
    xs = x_sample.reshape(bs * ss, D_MODEL)
    mem = mem_prompt.reshape(bp * n_mem, D_MODEL)
    g_final = norm_final.reshape(1, D_MODEL)

    bias_tiles = _prompt_bias_tiles(rel_bias, t_attn)
    rel_s = jnp.arange(past + ss, dtype=jnp.int32)[None, :] - (past + jnp.arange(ss, dtype=jnp.int32))[:, None]
    bias_s = _bias_table(rel_bias, rel_s) * LOG2E
    bias_s_past, bias_s_new = lax.optimization_barrier((bias_s[:, :, :past], bias_s[:, :, past:]))

    zero_state = (jnp.zeros((bp, ML_WIDTH, ML_STATE_W), F32), jnp.zeros((bp, 1, 128), F32))
    zero_conv = jnp.zeros((bp, CONV_K - 1, CONV_CH), F32)

    outs = {k: [] for k in ('pmk', 'pmv', 'pC', 'pn', 'pm', 'pcv', 'sC', 'sn', 'sm', 'scv')}
    kv_p = kv_s = None
    gate_lo = 3 * ATT_WIDTH + 4 * ML_WIDTH
    for l in range(depth):
        w = w_in[l]
        wts = (norm_mix[l].reshape(1, D_MODEL),
               jnp.concatenate([w[:, :gate_lo], w[:, gate_lo + 2 * H_M:]], axis=1).astype(BF16),
               w[:, gate_lo:gate_lo + 2 * H_M].T.astype(BF16),
               mlstm_gate_bias[l].astype(F32), conv_w[l].astype(F32), lambda_params[l].astype(F32),
               norm_att_heads[l].reshape(1, ATT_WIDTH), norm_mlstm_heads[l].reshape(1, ML_WIDTH),
               w_out[l].astype(BF16), norm_cross[l].reshape(1, D_MODEL),
               w_cq[l].astype(BF16), w_co[l].astype(BF16), norm_ffn[l].reshape(1, D_MODEL),
               w_gate[l].astype(BF16), w_up[l].astype(BF16), w_down[l].astype(BF16))

        w_kv = jnp.concatenate([w_ck[l], w_cv[l]], axis=1).astype(BF16)
        mkf, mvf, mkb, mvb = _mem_kv(mem, w_kv, 512)
        attn_p = lambda q, k, v, lam_p, g, li: _attn_prompt(q, k, v, bias_tiles, lam_p, g, bp, sp, t_attn, li)
        xp, kv_p, c1, n1, m1, cv1 = _layer(
            xp, l, depth, kv_p, wts, attn_p, mkb.reshape(bp, n_mem, -1), mvb.reshape(bp, n_mem, -1),
            zero_state, zero_conv, g_final, bp, sp, 512, l_prompt, 512)
        outs['pmk'].append(mkf.reshape(bp, n_mem, H_X, D_X))
        outs['pmv'].append(mvf.reshape(bp, n_mem, H_X, D_X))
        outs['pC'].append(c1); outs['pn'].append(n1); outs['pm'].append(m1); outs['pcv'].append(cv1)

        attn_s = lambda q, k, v, lam_p, g, li, l=l: _attn_sample(q, k, v, cache_att_k, cache_att_v, l,
                                                                 bias_s_past, bias_s_new, lam_p, g, bs, ss, li)
        s0 = _block_diag_state(state_mlstm_C[l], state_mlstm_n[l])
        m0 = jnp.pad(state_mlstm_m[l].astype(F32), ((0, 0), (0, 128 - H_M))).reshape(bs, 1, 128)
        xs, kv_s, c1, n1, m1, cv1 = _layer(
            xs, l, depth, kv_s, wts, attn_s,
            cache_mem_k[l].reshape(bs, n_mem, -1).astype(BF16), cache_mem_v[l].reshape(bs, n_mem, -1).astype(BF16),
            (s0, m0), state_conv[l].astype(F32), g_final, bs, ss, bs * ss, ss, bs * ss)
        outs['sC'].append(c1); outs['sn'].append(n1); outs['sm'].append(m1); outs['scv'].append(cv1)

    st = lambda k: jnp.stack(outs[k])
    per_head = lambda a, b, s: a.reshape(depth, b, s, H_A, DV_A)
    return (xp.reshape(bp, sp, D_MODEL), xs.reshape(bs, ss, D_MODEL),
            per_head(kv_p[0], bp, sp), per_head(kv_p[1], bp, sp),
            st('pmk'), st('pmv'), st('pC'), st('pn'), st('pm'), st('pcv'),
            per_head(kv_s[0], bs, ss), per_head(kv_s[1], bs, ss),
            st('sC'), st('sn'), st('sm'), st('scv'))
```

```python
import functools
import math

import jax
import jax.numpy as jnp
from jax import lax
from jax.experimental import pallas as pl
from jax.experimental.pallas import tpu as pltpu

F32 = jnp.float32
BF16 = jnp.bfloat16

D_MODEL = 1024
CHUNK = 64
H_A = 4
D_A = 64
DV_A = 128
ATT_WIDTH = H_A * DV_A
H_M = 4
D_M = 64
ML_WIDTH = H_M * D_M
CONV_CH = 256
CONV_K = 3
H_X = 4
D_X = 256
D_FF = 2816
NUM_BUCKETS = 32
MAX_DISTANCE = 128
EPS = 1e-6

N_PROJ = 3 * ATT_WIDTH + 4 * ML_WIDTH + 3 * CONV_CH
ML_STATE_W = ML_WIDTH + 128

V7X_VMEM_LIMIT = 56 * 1024 * 1024

LOG2E = math.log2(math.e)
RECURRENT_STREAMS = 8
ATTN_HEADS_PER_STEP = 4
TOKEN_TILE = 512
ATTN_TILE = 256
MLSTM_CHUNK = 128

_NT = (((1,), (1,)), ((), ()))
_TN = (((0,), (0,)), ((), ()))


def _const_spec(shape):
    nd = len(shape)
    return pl.BlockSpec(shape, lambda *_: (0,) * nd, pipeline_mode=pl.Buffered(1))


def _rms(x, g):
    ms = jnp.mean(x * x, axis=-1, keepdims=True)
    return x * lax.rsqrt(ms + EPS) * g


def _in_proj_kernel(x_ref, g_ref, w_ref, wgt_ref, *refs):
    qa_ref, kf_ref, kb_ref, vf_ref, vb_ref, qkvm_ref, om_ref, bch_ref, gt_ref = refs[-9:]
    n_prev = kf_ref.shape[0] - 1
    if n_prev:
        kf_ref[0:n_prev] = refs[0][...]
        vf_ref[0:n_prev] = refs[1][...]
    xn = _rms(x_ref[...], g_ref[...]).astype(BF16)

    def proj(lo, hi):
        return jnp.dot(xn, w_ref[:, lo:hi], preferred_element_type=F32)

    a = ATT_WIDTH
    qa_ref[...] = (proj(0, a) * (D_A ** -0.5 * LOG2E)).astype(BF16)
    k = proj(a, 2 * a)
    kb_ref[...] = k.astype(BF16)
    v = proj(2 * a, 3 * a)
    for h in range(H_A):
        kf_ref[n_prev, :, h, :] = k[:, h * DV_A:(h + 1) * DV_A]
        vf_ref[n_prev, :, h, :] = v[:, h * DV_A:(h + 1) * DV_A]
    vb_ref[...] = v.astype(BF16)
    o = 3 * a
    m = ML_WIDTH
    qkvm_ref[:, 0:m] = proj(o, o + m).astype(BF16)
    qkvm_ref[:, m:2 * m] = (proj(o + m, o + 2 * m) * (D_M ** -0.5)).astype(BF16)
    qkvm_ref[:, 2 * m:3 * m] = proj(o + 2 * m, o + 3 * m).astype(BF16)
    om_ref[...] = proj(o + 3 * m, o + 4 * m)
    bch_ref[...] = proj(o + 4 * m, N_PROJ)
    gt_ref[...] = lax.dot_general(wgt_ref[...], xn, _NT, preferred_element_type=F32)


def _in_proj(x, g, w, wgt, tm, kv_prev):
    t = x.shape[0]
    a, m = ATT_WIDTH, ML_WIDTH
    n_prev = 0 if kv_prev is None else kv_prev[0].shape[0]
    row = lambda n: pl.BlockSpec((tm, n), lambda i: (i, 0))
    slabs = lambda n: pl.BlockSpec((n, tm, H_A, DV_A), lambda i: (0, i, 0, 0))
    outs = [
        (jax.ShapeDtypeStruct((t, a), BF16), row(a)),
        (jax.ShapeDtypeStruct((n_prev + 1, t, H_A, DV_A), F32), slabs(n_prev + 1)),
        (jax.ShapeDtypeStruct((t, a), BF16), row(a)),
        (jax.ShapeDtypeStruct((n_prev + 1, t, H_A, DV_A), F32), slabs(n_prev + 1)),
        (jax.ShapeDtypeStruct((t, a), BF16), row(a)),
        (jax.ShapeDtypeStruct((t, 3 * m), BF16), row(3 * m)),
        (jax.ShapeDtypeStruct((t, m), F32), row(m)),
        (jax.ShapeDtypeStruct((t, 3 * CONV_CH), F32), row(3 * CONV_CH)),
        (jax.ShapeDtypeStruct((8, t), F32), pl.BlockSpec((8, tm), lambda i: (0, i))),
    ]
    carried = [] if kv_prev is None else list(kv_prev)
    return pl.pallas_call(
        _in_proj_kernel,
        grid=(t // tm,),
        in_specs=[row(D_MODEL), _const_spec((1, D_MODEL)), _const_spec((D_MODEL, N_PROJ)),
                  _const_spec((8, D_MODEL))] + [slabs(n_prev)] * len(carried),
        out_specs=[s for _, s in outs],
        out_shape=[s for s, _ in outs],
        compiler_params=pltpu.CompilerParams(dimension_semantics=("parallel",),
                                             vmem_limit_bytes=V7X_VMEM_LIMIT),
        name="in_proj",
    )(x, g, w, wgt, *carried)


def _lambda_value(lp, lam_init):
    a = jnp.sum(lp[0:1] * lp[1:2], axis=1, keepdims=True)
    b = jnp.sum(lp[2:3] * lp[3:4], axis=1, keepdims=True)
    return jnp.exp(a) - jnp.exp(b) + lam_init


def _stack_maps(q):
    lane = lax.broadcasted_iota(jnp.int32, q.shape, 1)
    zero = jnp.zeros_like(q)
    return jnp.concatenate([jnp.where(lane < D_A, q, zero), jnp.where(lane >= D_A, q, zero)], axis=0)


def _attn_finish(acc, lam, g, lam_init, t):
    o = acc[:, :DV_A] / acc[:, DV_A:]
    d = o[:t] - lam * o[t:]
    return (_rms(d, g) * (1.0 - lam_init)).astype(BF16)


def _with_ones(v):
    return jnp.concatenate([v, jnp.ones_like(v)], axis=1)


def _attn_prompt_kernel(lam_ref, g_ref, q_ref, k_ref, v_ref, bias_ref, o_ref,
                        m_scr, l_scr, acc_scr, sa_scr, sb_scr, p_scr, *, t, hp, lam_init):
    i = pl.program_id(2)
    heads = [slice(hh * DV_A, (hh + 1) * DV_A) for hh in range(hp)]
    cols = [slice(hh * 2 * t, (hh + 1) * 2 * t) for hh in range(hp)]
    qz = [_stack_maps(q_ref[:, hd]) for hd in heads]

    def rows(kb):
        return pl.ds(pl.multiple_of(kb * t, t), t)

    def qk(kb):
        r = rows(kb)
        return jnp.concatenate([lax.dot_general(k_ref[r, hd], z, _NT, preferred_element_type=F32)
                                for hd, z in zip(heads, qz)], axis=1)

    def pv(p, kb):
        r = rows(kb)
        return jnp.concatenate([lax.dot_general(v_ref[r, hd], p[:, c], _TN, preferred_element_type=F32)
                                for hd, c in zip(heads, cols)], axis=1)

    def bias(lo):
        return jnp.concatenate([bias_ref[hh, 0, lo:lo + t, :] for hh in range(hp)], axis=1)

    def softmax_step(m_prev, s):
        m_new = jnp.maximum(m_prev, jnp.max(s, axis=0, keepdims=True))
        p = jnp.exp2(s - m_new)
        return m_new, jnp.exp2(m_prev - m_new), p, jnp.sum(p, axis=0, keepdims=True)

    n_far = jnp.maximum(i - 1, 0)
    last = n_far - 1
    kb_prev = jnp.maximum(i - 1, 0)

    s_diag = qk(i)
    s_prev = qk(kb_prev)
    sa_scr[...] = qk(0)
    s_diag = s_diag + bias(t)
    m = jnp.max(s_diag, axis=0, keepdims=True)
    p = jnp.exp2(s_diag - m)
    l = jnp.sum(p, axis=0, keepdims=True)
    acc = pv(p.astype(BF16), i)
    w_prev = (i >= 1).astype(F32)
    m, alpha, p, p_sum = softmax_step(m, s_prev + bias(0))
    m_scr[...] = m
    l_scr[...] = alpha * l + w_prev * p_sum
    acc_scr[...] = alpha * acc
    p_scr[...] = p.astype(BF16)

    def far_pair(j, carry):
        kb1 = jnp.minimum(2 * j + 1, last)
        w1 = (2 * j + 1 <= last).astype(F32)
        kb_pending = jnp.where(j == 0, kb_prev, 2 * j - 1)
        sb_scr[...] = qk(kb1)
        owed = pv(p_scr[...], kb_pending)
        m, alpha, p, p_sum = softmax_step(m_scr[...], sa_scr[...])
        l = alpha * l_scr[...] + p_sum
        acc = alpha * (acc_scr[...] + owed)
        sa_scr[...] = qk(jnp.minimum(2 * j + 2, last))
        owed = pv(p.astype(BF16), 2 * j)
        m, alpha, p, p_sum = softmax_step(m, sb_scr[...])
        m_scr[...] = m
        l_scr[...] = alpha * l + w1 * p_sum
        acc_scr[...] = alpha * (acc + owed)
        p_scr[...] = p.astype(BF16)
        return carry

    trips = (n_far + 1) // 2
    lax.fori_loop(0, trips, far_pair, 0)

    kb_pending = jnp.where(trips == 0, kb_prev, jnp.minimum(2 * trips - 1, last))
    w_pending = jnp.where(trips == 0, w_prev, (n_far % 2 == 0).astype(F32))
    acc = acc_scr[...] + w_pending * pv(p_scr[...], kb_pending)

    lam = _lambda_value(lam_ref[...], lam_init)
    o = acc * (1.0 / l_scr[...])
    for hd, c in zip(heads, cols):
        d = o[:, c][:, :t] - lam * o[:, c][:, t:]
        ms = jnp.mean(d * d, axis=0, keepdims=True)
        y = jnp.transpose(d * lax.rsqrt(ms + EPS))
        o_ref[:, hd] = (y * g_ref[:, hd] * (1.0 - lam_init)).astype(BF16)


def _attn_prompt(q, k, v, bias, lam_p, g_att, batch, seq, t, lam_init):
    nq = seq // t
    hp = ATTN_HEADS_PER_STEP
    w = hp * 2 * t
    kv_spec = pl.BlockSpec((seq, hp * DV_A), lambda b, h, i: (b, h))
    return pl.pallas_call(
        functools.partial(_attn_prompt_kernel, t=t, hp=hp, lam_init=lam_init),
        grid=(batch, H_A // hp, nq),
        in_specs=[_const_spec((4, D_A)),
                  pl.BlockSpec((1, hp * DV_A), lambda b, h, i: (0, h)),
                  pl.BlockSpec((t, hp * DV_A), lambda b, h, i: (b * nq + i, h)),
                  kv_spec, kv_spec,
                  pl.BlockSpec((hp, 1, 2 * t, 2 * t), lambda b, h, i: (h, jnp.minimum(i, 1), 0, 0))],
        out_specs=pl.BlockSpec((t, hp * DV_A), lambda b, h, i: (b * nq + i, h)),
        out_shape=jax.ShapeDtypeStruct((batch * seq, ATT_WIDTH), BF16),
        scratch_shapes=[pltpu.VMEM((1, w), F32), pltpu.VMEM((1, w), F32),
                        pltpu.VMEM((DV_A, w), F32),
                        pltpu.VMEM((t, w), F32), pltpu.VMEM((t, w), F32),
                        pltpu.VMEM((t, w), BF16)],
        compiler_params=pltpu.CompilerParams(dimension_semantics=("parallel", "parallel", "arbitrary"),
                                             vmem_limit_bytes=V7X_VMEM_LIMIT),
        name="attn_prompt",
    )(lam_p, g_att, q, k, v, bias)


def _attn_sample_kernel(lam_ref, g_ref, q_ref, kp_ref, vp_ref, kn_ref, vn_ref, bp_ref, bn_ref, o_ref,
                        *, lq, lam_init):
    lam = _lambda_value(lam_ref[...], lam_init)
    for h in range(H_A):
        hd = slice(h * DV_A, (h + 1) * DV_A)
        qz = _stack_maps(q_ref[:, hd])
        bp = bp_ref[h]
        bn = bn_ref[h]
        sp = lax.dot_general(qz, kp_ref[0, :, h, :].astype(BF16), _NT, preferred_element_type=F32)
        sp = sp + jnp.concatenate([bp, bp], axis=0)
        sn = lax.dot_general(qz, kn_ref[:, hd], _NT, preferred_element_type=F32)
        sn = sn + jnp.concatenate([bn, bn], axis=0)
        m = jnp.maximum(jnp.max(sp, axis=1, keepdims=True), jnp.max(sn, axis=1, keepdims=True))
        pp = jnp.exp2(sp - m)
        pn = jnp.exp2(sn - m)
        acc = (jnp.dot(pp.astype(BF16), _with_ones(vp_ref[0, :, h, :].astype(BF16)), preferred_element_type=F32)
               + jnp.dot(pn.astype(BF16), _with_ones(vn_ref[:, hd]), preferred_element_type=F32))
        o_ref[:, hd] = _attn_finish(acc, lam, g_ref[:, hd], lam_init, lq)


def _attn_sample(q, k_new, v_new, k_cache, v_cache, layer, bias_past, bias_new, lam_p, g_att, batch, lq, lam_init):
    past = k_cache.shape[2]
    new_spec = pl.BlockSpec((lq, ATT_WIDTH), lambda b: (b, 0))
    past_spec = pl.BlockSpec((None, 1, past, H_A, DV_A), lambda b: (layer, b, 0, 0, 0))
    return pl.pallas_call(
        functools.partial(_attn_sample_kernel, lq=lq, lam_init=lam_init),
        grid=(batch,),
        in_specs=[_const_spec((4, D_A)), _const_spec((1, ATT_WIDTH)),
                  new_spec, past_spec, past_spec, new_spec, new_spec,
                  _const_spec((H_A, lq, past)), _const_spec((H_A, lq, lq))],
        out_specs=new_spec,
        out_shape=jax.ShapeDtypeStruct((batch * lq, ATT_WIDTH), BF16),
        compiler_params=pltpu.CompilerParams(dimension_semantics=("parallel",),
                                             vmem_limit_bytes=V7X_VMEM_LIMIT),
        name="attn_sample",
    )(lam_p, g_att, q, k_cache, v_cache, k_new, v_new, bias_past, bias_new)


def _per_stream(fn, *arrays):
    return jnp.stack([fn(*(a[g] for a in arrays)) for g in range(arrays[0].shape[0])])


def _recurrent_kernel(gb_ref, q_ref, k_ref, v_ref, og_ref, gt_ref, b_ref, c_ref, hc_ref, cw_ref, gml_ref,
                      s0_ref, m0_ref, cv0_ref,
                      hm_ref, oc_ref, s_ref, m_ref, cv_ref, *, L):
    @pl.when(pl.program_id(1) == 0)
    def _():
        s_ref[...] = s0_ref[...]
        m_ref[...] = m0_ref[...]
        cv_ref[...] = cv0_ref[...]

    q = q_ref[...]
    k = k_ref[...]
    v = v_ref[...]
    G = q.shape[0]
    state = s_ref[...]
    q_state = _per_stream(lambda a, b: jnp.dot(a, b, preferred_element_type=F32),
                          q, state.astype(BF16))
    q_c = q_state[:, :, :ML_WIDTH]

    lane_head = lax.broadcasted_iota(jnp.int32, (1, L, ML_WIDTH), 2) // D_M
    row = lax.broadcasted_iota(jnp.int32, (1, L, L), 1)
    col = lax.broadcasted_iota(jnp.int32, (1, L, L), 2)
    causal = col <= row
    eye = col == row
    st_lane = lax.broadcasted_iota(jnp.int32, (1, 1, ML_STATE_W), 2)
    st_lane_head = jnp.where(st_lane < ML_WIDTH, st_lane // D_M, st_lane - ML_WIDTH)
    m_lane = lax.broadcasted_iota(jnp.int32, (1, 1, 128), 2)

    gt = gt_ref[:, 0]
    m_all = m_ref[...]
    num = jnp.zeros((G, L, ML_WIDTH), F32)
    w_state = jnp.zeros((G, L, ML_WIDTH), F32)
    decay = jnp.zeros((G, 1, ML_STATE_W), F32)
    m_next = jnp.zeros((G, 1, 128), F32)
    for h in range(H_M):
        ig = gt[:, h:h + 1, :] + gb_ref[0, h]
        fz = gt[:, H_M + h:H_M + h + 1, :] + gb_ref[1, h]
        lf = jnp.minimum(fz, 0.0) - jnp.log1p(jnp.exp(-jnp.abs(fz)))
        f_col = jnp.sum(jnp.where(causal, lf, 0.0), axis=2, keepdims=True)
        f_row = jnp.sum(jnp.where(eye, f_col, 0.0), axis=1, keepdims=True)
        d = jnp.where(causal, f_col + (ig - f_row), -jnp.inf)
        m_prev = m_all[:, :, h:h + 1]
        inter = f_col + m_prev
        mt = jnp.maximum(inter, jnp.max(d, axis=2, keepdims=True))
        w_intra = jnp.exp(d - mt)
        w_inter = jnp.exp(inter - mt)
        head = lane_head == h
        qk = _per_stream(lambda a, b: lax.dot_general(a, b, _NT, preferred_element_type=F32),
                         jnp.where(head, q, jnp.zeros_like(q)), k)
        s = qk * w_intra
        den = jnp.sum(s, axis=2, keepdims=True) + w_inter * q_state[:, :, ML_WIDTH + h:ML_WIDTH + h + 1]
        inv = 1.0 / jnp.maximum(jnp.abs(den), jnp.exp(-mt))
        sv = _per_stream(lambda a, b: jnp.dot(a, b, preferred_element_type=F32), s.astype(BF16), v)
        num = jnp.where(head, (sv + w_inter * q_c) * inv, num)
        m_new = mt[:, L - 1:L, :]
        w_col = jnp.sum(jnp.where(eye, w_intra[:, L - 1:L, :], 0.0), axis=2, keepdims=True)
        w_state = jnp.where(head, w_col, w_state)
        dec = jnp.exp(f_col[:, L - 1:L, :] + m_prev - m_new)
        decay = jnp.where(st_lane_head == h, dec, decay)
        m_next = jnp.where(m_lane == h, m_new, m_next)

    kw = (k.astype(F32) * w_state).astype(BF16)
    v_aug = jnp.concatenate([v, jnp.ones((G, L, ML_STATE_W - ML_WIDTH), BF16)], axis=2)
    upd = _per_stream(lambda a, b: lax.dot_general(a, b, _TN, preferred_element_type=F32), kw, v_aug)
    s_row_head = lax.broadcasted_iota(jnp.int32, (1, ML_WIDTH, ML_STATE_W), 1) // D_M
    s_col = lax.broadcasted_iota(jnp.int32, (1, ML_WIDTH, ML_STATE_W), 2)
    s_col_head = jnp.where(s_col < ML_WIDTH, s_col // D_M, s_col - ML_WIDTH)
    s_ref[...] = decay * state + jnp.where(s_row_head == s_col_head, upd, 0.0)
    m_ref[...] = m_next

    hg = num * jax.nn.sigmoid(og_ref[...])
    scale = jnp.zeros((G, L, ML_WIDTH), F32)
    for h in range(H_M):
        head = lane_head == h
        ss = jnp.sum(jnp.where(head, hg * hg, 0.0), axis=2, keepdims=True)
        scale = jnp.where(head, lax.rsqrt(ss * (1.0 / D_M) + EPS), scale)
    hm_ref[...] = (hg * scale * gml_ref[...]).astype(BF16)

    u = c_ref[...] * hc_ref[...]
    prev = cv_ref[...]
    ri = lax.broadcasted_iota(jnp.int32, (1, L, CONV_CH), 1)
    u1 = jnp.where(ri == 0, prev[:, 1:2], _per_stream(lambda a: pltpu.roll(a, 1, 0), u))
    u2 = jnp.where(ri == 0, prev[:, 0:1],
                   jnp.where(ri == 1, prev[:, 1:2], _per_stream(lambda a: pltpu.roll(a, 2, 0), u)))
    w = cw_ref[...]
    oc_ref[...] = (b_ref[...] * (w[0:1] * u2 + w[1:2] * u1 + w[2:3] * u)).astype(BF16)
    cv_ref[:, 0:1, :] = u[:, L - 2:L - 1]
    cv_ref[:, 1:2, :] = u[:, L - 1:L]


def _recurrent(gate_b, qkvm, og, gt, bch, conv_w, g_ml, s0, m0, cv0, batch, seq, L, G):
    assert batch % G == 0 and seq % L == 0
    nc = seq // L
    tok = lambda j: pl.BlockSpec((G, L, ML_WIDTH), lambda b, c: (b, c, j))
    per_b = lambda shape: pl.BlockSpec((G,) + shape, lambda b, c: (b, 0, 0))
    return pl.pallas_call(
        functools.partial(_recurrent_kernel, L=L),
        grid=(batch // G, nc),
        in_specs=[pl.BlockSpec(memory_space=pltpu.SMEM),
                  tok(0), tok(1), tok(2), tok(0),
                  pl.BlockSpec((G, 1, 8, L), lambda b, c: (b, c, 0, 0)),
                  tok(0), tok(1), tok(2),
                  _const_spec((CONV_K, CONV_CH)), _const_spec((1, ML_WIDTH)),
                  per_b((ML_WIDTH, ML_STATE_W)), per_b((1, 128)), per_b((CONV_K - 1, CONV_CH))],
        out_specs=[tok(0), tok(0),
                   per_b((ML_WIDTH, ML_STATE_W)), per_b((1, 128)), per_b((CONV_K - 1, CONV_CH))],
        out_shape=[jax.ShapeDtypeStruct((batch, seq, ML_WIDTH), BF16),
                   jax.ShapeDtypeStruct((batch, seq, CONV_CH), BF16),
                   jax.ShapeDtypeStruct((batch, ML_WIDTH, ML_STATE_W), F32),
                   jax.ShapeDtypeStruct((batch, 1, 128), F32),
                   jax.ShapeDtypeStruct((batch, CONV_K - 1, CONV_CH), F32)],
        compiler_params=pltpu.CompilerParams(dimension_semantics=("parallel", "arbitrary"),
                                             vmem_limit_bytes=V7X_VMEM_LIMIT),
        name="recurrent",
    )(gate_b, qkvm, qkvm, qkvm, og, gt, bch, bch, bch, conv_w, g_ml, s0, m0, cv0)


def _mem_kv_kernel(x_ref, w_ref, kf_ref, vf_ref, kb_ref, vb_ref):
    x = x_ref[...].astype(BF16)
    n = H_X * D_X
    k = jnp.dot(x, w_ref[:, :n], preferred_element_type=F32)
    v = jnp.dot(x, w_ref[:, n:], preferred_element_type=F32)
    kf_ref[...] = k
    vf_ref[...] = v
    kb_ref[...] = k.astype(BF16)
    vb_ref[...] = v.astype(BF16)


def _mem_kv(mem, w_kv, tm):
    t = mem.shape[0]
    n = H_X * D_X
    row = pl.BlockSpec((tm, n), lambda i: (i, 0))
    return pl.pallas_call(
        _mem_kv_kernel,
        grid=(t // tm,),
        in_specs=[pl.BlockSpec((tm, D_MODEL), lambda i: (i, 0)), _const_spec((D_MODEL, 2 * n))],
        out_specs=[row, row, row, row],
        out_shape=[jax.ShapeDtypeStruct((t, n), F32), jax.ShapeDtypeStruct((t, n), F32),
                   jax.ShapeDtypeStruct((t, n), BF16), jax.ShapeDtypeStruct((t, n), BF16)],
        compiler_params=pltpu.CompilerParams(dimension_semantics=("parallel",),
                                             vmem_limit_bytes=V7X_VMEM_LIMIT),
        name="mem_kv",
    )(mem, w_kv)


FF_CHUNK = 256


def _post_kernel(x_ref, oa_ref, hm_ref, oc_ref, mk_ref, mv_ref, wout_ref, gc_ref, wcq_ref, wco_ref,
                 gf_ref, wg_ref, wu_ref, wd_ref, gfin_ref, o_ref, *, final):
    a, m = ATT_WIDTH, ML_WIDTH
    x = x_ref[...]
    x = x + (jnp.dot(oa_ref[...], wout_ref[0:a, :], preferred_element_type=F32)
             + jnp.dot(hm_ref[...], wout_ref[a:a + m, :], preferred_element_type=F32)
             + jnp.dot(oc_ref[...], wout_ref[a + m:, :], preferred_element_type=F32))

    xn = _rms(x, gc_ref[...]).astype(BF16)
    qc = (jnp.dot(xn, wcq_ref[...], preferred_element_type=F32) * (D_X ** -0.5)).astype(BF16)
    cross = jnp.zeros_like(x)
    n_streams = mk_ref.shape[0]
    rows = x.shape[0] // n_streams
    heads = [slice(h * D_X, (h + 1) * D_X) for h in range(H_X)]
    streams = [slice(b * rows, (b + 1) * rows) for b in range(n_streams)]
    scores = [[lax.dot_general(qc[r, sl], mk_ref[b, :, sl], _NT, preferred_element_type=F32)
               for b, r in enumerate(streams)] for sl in heads]
    for sl, s_head in zip(heads, scores):
        outs = []
        for b, s in enumerate(s_head):
            p = jnp.exp(s - jnp.max(s, axis=1, keepdims=True))
            p = p * (1.0 / jnp.sum(p, axis=1, keepdims=True))
            outs.append(jnp.dot(p.astype(BF16), mv_ref[b, :, sl], preferred_element_type=F32).astype(BF16))
        o = outs[0] if n_streams == 1 else jnp.concatenate(outs, axis=0)
        cross = cross + jnp.dot(o, wco_ref[sl, :], preferred_element_type=F32)
    x = x + cross

    xn = _rms(x, gf_ref[...]).astype(BF16)
    ff = jnp.zeros_like(x)
    for j in range(D_FF // FF_CHUNK):
        sl = slice(j * FF_CHUNK, (j + 1) * FF_CHUNK)
        g = jnp.dot(xn, wg_ref[:, sl], preferred_element_type=F32)
        u = jnp.dot(xn, wu_ref[:, sl], preferred_element_type=F32)
        act = (g * jax.nn.sigmoid(g) * u).astype(BF16)
        ff = ff + jnp.dot(act, wd_ref[sl, :], preferred_element_type=F32)
    x = x + ff
    if final:
        x = _rms(x, gfin_ref[...])
    o_ref[...] = x


def _post(x, oa, hm, oc, mk, mv, w_out, g_cross, w_cq, w_co, g_ffn, w_gate, w_up, w_down, g_final,
          tm, seq, final):
    t = x.shape[0]
    row = lambda n: pl.BlockSpec((tm, n), lambda i: (i, 0))
    if tm <= seq:
        tiles_per_stream = seq // tm
        mem = pl.BlockSpec((1,) + mk.shape[1:], lambda i: (i // tiles_per_stream, 0, 0))
    else:
        mem = pl.BlockSpec((tm // seq,) + mk.shape[1:], lambda i: (i, 0, 0))
    vec = _const_spec((1, D_MODEL))
    return pl.pallas_call(
        functools.partial(_post_kernel, final=final),
        grid=(t // tm,),
        in_specs=[row(D_MODEL), row(ATT_WIDTH), row(ML_WIDTH), row(CONV_CH), mem, mem,
                  _const_spec(w_out.shape), vec, _const_spec(w_cq.shape), _const_spec(w_co.shape),
                  vec, _const_spec(w_gate.shape), _const_spec(w_up.shape), _const_spec(w_down.shape), vec],
        out_specs=row(D_MODEL),
        out_shape=jax.ShapeDtypeStruct((t, D_MODEL), F32),
        compiler_params=pltpu.CompilerParams(dimension_semantics=("parallel",),
                                             vmem_limit_bytes=V7X_VMEM_LIMIT),
        name="post",
    )(x, oa, hm, oc, mk, mv, w_out, g_cross, w_cq, w_co, g_ffn, w_gate, w_up, w_down, g_final)


def _rel_bucket(rel):
    half = NUM_BUCKETS // 2
    max_exact = half // 2
    n = jnp.abs(rel)
    large = max_exact + (jnp.log(jnp.maximum(n, 1).astype(F32) / max_exact)
                         / math.log(MAX_DISTANCE / max_exact) * (half - max_exact)).astype(jnp.int32)
    large = jnp.minimum(large, half - 1)
    return jnp.where(rel > 0, half, 0) + jnp.where(n < max_exact, n, large)


def _bias_table(rel_bias, rel):
    bucket = _rel_bucket(rel)[None]
    table = rel_bias.astype(F32)
    out = jnp.zeros((table.shape[1],) + rel.shape, F32)
    for b in range(NUM_BUCKETS):
        out = jnp.where(bucket == b, table[b][:, None, None], out)
    return out


def _prompt_bias_tiles(rel_bias, t):
    key = jnp.arange(t, dtype=jnp.int32)[:, None]
    qry = jnp.arange(t, dtype=jnp.int32)[None, :]
    far = rel_bias[_rel_bucket(jnp.int32(-MAX_DISTANCE))].astype(F32)[:, None, None]
    prev = (_bias_table(rel_bias, key - t - qry) - far) * LOG2E
    diag = jnp.where((key // CHUNK) <= (qry // CHUNK), (_bias_table(rel_bias, key - qry) - far) * LOG2E, -jnp.inf)
    both_maps = lambda a: jnp.concatenate([a, a], axis=2)
    tiles = jnp.stack([jnp.concatenate([diag, diag], axis=1), jnp.concatenate([prev, diag], axis=1)], axis=1)
    return lax.optimization_barrier(both_maps(tiles.reshape(-1, 2 * t, t)).reshape(-1, 2, 2 * t, 2 * t))


def _block_diag_state(c, n):
    b = c.shape[0]
    eye = jnp.eye(H_M, dtype=F32)
    cbd = jnp.einsum('bhdv,hg->bhdgv', c.astype(F32), eye).reshape(b, ML_WIDTH, ML_WIDTH)
    ncol = jnp.einsum('bhd,hg->bhdg', n.astype(F32), eye).reshape(b, ML_WIDTH, H_M)
    pad = jnp.zeros((b, ML_WIDTH, ML_STATE_W - ML_WIDTH - H_M), F32)
    return jnp.concatenate([cbd, ncol, pad], axis=2)


def _unpack_state(s, m):
    b = s.shape[0]
    blocks = s[:, :, :ML_WIDTH].reshape(b, H_M, D_M, H_M, D_M)
    c = jnp.stack([blocks[:, h, :, h, :] for h in range(H_M)], axis=1)
    ncols = s[:, :, ML_WIDTH:ML_WIDTH + H_M].reshape(b, H_M, D_M, H_M)
    n = jnp.stack([ncols[:, h, :, h] for h in range(H_M)], axis=1)
    return c, n, m[:, 0, :H_M]


def _layer(x, layer, depth, kv_all, wts, attn_fn, mem_k, mem_v, ml_state, conv_prev, g_final,
           batch, seq, tm_proj, L, tm_post):
    final = layer == depth - 1
    (g_mix, w_proj, w_gt, gate_b, conv_w, lam_p, g_att, g_ml, w_out,
     g_cross, w_cq, w_co, g_ffn, w_gate, w_up, w_down) = wts
    t = batch * seq
    lam_init = 0.8 - 0.6 * math.exp(-0.3 * layer)
    qa, kf, kb, vf, vb, qkvm, og, bch, gt = _in_proj(x, g_mix, w_proj, w_gt, tm_proj, kv_all)
    oa = attn_fn(qa, kb, vb, lam_p, g_att, lam_init)
    gt = jnp.transpose(gt.reshape(8, batch, seq // L, L), (1, 2, 0, 3))
    tok3 = lambda a: a.reshape(batch, seq, a.shape[-1])
    hm, oc, s1, m1, cv1 = _recurrent(gate_b, tok3(qkvm), tok3(og), gt, tok3(bch), conv_w, g_ml,
                                     ml_state[0], ml_state[1], conv_prev, batch, seq, L,
                                     min(RECURRENT_STREAMS, batch))
    hm, oc = hm.reshape(t, ML_WIDTH), oc.reshape(t, CONV_CH)
    x = _post(x, oa, hm, oc, mem_k, mem_v, w_out, g_cross, w_cq, w_co, g_ffn, w_gate, w_up, w_down,
              g_final, tm_post, seq, final)
    c1, n1, mm1 = _unpack_state(s1, m1)
    return x, (kf, vf), c1, n1, mm1, cv1


def kernel(x_prompt, x_sample, mem_prompt, cache_att_k, cache_att_v, cache_mem_k, cache_mem_v,
           state_mlstm_C, state_mlstm_n, state_mlstm_m, state_conv,
           norm_mix, w_in, mlstm_gate_bias, conv_w, lambda_params, norm_att_heads, norm_mlstm_heads,
           w_out, norm_cross, w_cq, w_ck, w_cv, w_co, norm_ffn, w_gate, w_up, w_down,
           rel_bias, norm_final):
    bp, sp, _ = x_prompt.shape
    bs, ss, _ = x_sample.shape
    depth = w_in.shape[0]
    past = cache_att_k.shape[2]
    n_mem = mem_prompt.shape[1]
    xp = x_prompt.reshape(bp * sp, D_MODEL)
    xs = x_sample.reshape(bs * ss, D_MODEL)
    mem = mem_prompt.reshape(bp * n_mem, D_MODEL)
    g_final = norm_final.reshape(1, D_MODEL)

    bias_tiles = _prompt_bias_tiles(rel_bias, ATTN_TILE)
    rel_s = jnp.arange(past + ss, dtype=jnp.int32)[None, :] - (past + jnp.arange(ss, dtype=jnp.int32))[:, None]
    bias_s = _bias_table(rel_bias, rel_s) * LOG2E
    bias_s_past, bias_s_new = lax.optimization_barrier((bias_s[:, :, :past], bias_s[:, :, past:]))

    zero_state = (jnp.zeros((bp, ML_WIDTH, ML_STATE_W), F32), jnp.zeros((bp, 1, 128), F32))
    zero_conv = jnp.zeros((bp, CONV_K - 1, CONV_CH), F32)

    outs = {k: [] for k in ('pmk', 'pmv', 'pC', 'pn', 'pm', 'pcv', 'sC', 'sn', 'sm', 'scv')}
    kv_p = kv_s = None
    gate_lo = 3 * ATT_WIDTH + 4 * ML_WIDTH
    for l in range(depth):
        w = w_in[l]
        wts = (norm_mix[l].reshape(1, D_MODEL),
               jnp.concatenate([w[:, :gate_lo], w[:, gate_lo + 2 * H_M:]], axis=1).astype(BF16),
               w[:, gate_lo:gate_lo + 2 * H_M].T.astype(BF16),
               mlstm_gate_bias[l].astype(F32), conv_w[l].astype(F32), lambda_params[l].astype(F32),
               norm_att_heads[l].reshape(1, ATT_WIDTH), norm_mlstm_heads[l].reshape(1, ML_WIDTH),
               w_out[l].astype(BF16), norm_cross[l].reshape(1, D_MODEL),
               w_cq[l].astype(BF16), w_co[l].astype(BF16), norm_ffn[l].reshape(1, D_MODEL),
               w_gate[l].astype(BF16), w_up[l].astype(BF16), w_down[l].astype(BF16))

        w_kv = jnp.concatenate([w_ck[l], w_cv[l]], axis=1).astype(BF16)
        mkf, mvf, mkb, mvb = _mem_kv(mem, w_kv, TOKEN_TILE)
        attn_p = lambda q, k, v, lam_p, g, li: _attn_prompt(q, k, v, bias_tiles, lam_p, g, bp, sp, ATTN_TILE, li)
        xp, kv_p, c1, n1, m1, cv1 = _layer(
            xp, l, depth, kv_p, wts, attn_p, mkb.reshape(bp, n_mem, -1), mvb.reshape(bp, n_mem, -1),
            zero_state, zero_conv, g_final, bp, sp, TOKEN_TILE, MLSTM_CHUNK, TOKEN_TILE)
        outs['pmk'].append(mkf.reshape(bp, n_mem, H_X, D_X))
        outs['pmv'].append(mvf.reshape(bp, n_mem, H_X, D_X))
        outs['pC'].append(c1); outs['pn'].append(n1); outs['pm'].append(m1); outs['pcv'].append(cv1)

        attn_s = lambda q, k, v, lam_p, g, li, l=l: _attn_sample(q, k, v, cache_att_k, cache_att_v, l,
                                                                 bias_s_past, bias_s_new, lam_p, g, bs, ss, li)
        s0 = _block_diag_state(state_mlstm_C[l], state_mlstm_n[l])
        m0 = jnp.pad(state_mlstm_m[l].astype(F32), ((0, 0), (0, 128 - H_M))).reshape(bs, 1, 128)
        xs, kv_s, c1, n1, m1, cv1 = _layer(
            xs, l, depth, kv_s, wts, attn_s,
            cache_mem_k[l].reshape(bs, n_mem, -1).astype(BF16), cache_mem_v[l].reshape(bs, n_mem, -1).astype(BF16),
            (s0, m0), state_conv[l].astype(F32), g_final, bs, ss, bs * ss, ss, bs * ss)
        outs['sC'].append(c1); outs['sn'].append(n1); outs['sm'].append(m1); outs['scv'].append(cv1)

    st = lambda k: jnp.stack(outs[k])
    per_head = lambda a, b, s: a.reshape(depth, b, s, H_A, DV_A)
    return (xp.reshape(bp, sp, D_MODEL), xs.reshape(bs, ss, D_MODEL),
            per_head(kv_p[0], bp, sp), per_head(kv_p[1], bp, sp),
            st('pmk'), st('pmv'), st('pC'), st('pn'), st('pm'), st('pcv'),
            per_head(kv_s[0], bs, ss), per_head(kv_s[1], bs, ss),
            st('sC'), st('sn'), st('sm'), st('scv'))
```

```python
import functools
import math

import jax
import jax.numpy as jnp
from jax import lax
from jax.experimental import pallas as pl
from jax.experimental.pallas import tpu as pltpu

F32 = jnp.float32
BF16 = jnp.bfloat16

D_MODEL = 1024
CHUNK = 64
H_A = 4
D_A = 64
DV_A = 128
ATT_WIDTH = H_A * DV_A
H_M = 4
D_M = 64
ML_WIDTH = H_M * D_M
CONV_CH = 256
CONV_K = 3
H_X = 4
D_X = 256
D_FF = 2816
NUM_BUCKETS = 32
MAX_DISTANCE = 128
EPS = 1e-6

N_PROJ = 3 * ATT_WIDTH + 4 * ML_WIDTH + 3 * CONV_CH
ML_STATE_W = ML_WIDTH + 128

V7X_VMEM_LIMIT = 56 * 1024 * 1024

LOG2E = math.log2(math.e)
RECURRENT_STREAMS = 8
ATTN_HEADS_PER_STEP = 4
TOKEN_TILE = 512
ATTN_TILE = 256
MLSTM_CHUNK = 128

_NT = (((1,), (1,)), ((), ()))
_TN = (((0,), (0,)), ((), ()))


def _const_spec(shape):
    nd = len(shape)
    return pl.BlockSpec(shape, lambda *_: (0,) * nd, pipeline_mode=pl.Buffered(1))


def _rms(x, g):
    ms = jnp.mean(x * x, axis=-1, keepdims=True)
    return x * lax.rsqrt(ms + EPS) * g


def _in_proj_kernel(x_ref, g_ref, w_ref, wgt_ref, *refs):
    qa_ref, kf_ref, kb_ref, vf_ref, vb_ref, qkvm_ref, om_ref, bch_ref, gt_ref = refs[-9:]
    n_prev = kf_ref.shape[0] - 1
    if n_prev:
        kf_ref[0:n_prev] = refs[0][...]
        vf_ref[0:n_prev] = refs[1][...]
    xn = _rms(x_ref[...], g_ref[...]).astype(BF16)

    def proj(lo, hi):
        return jnp.dot(xn, w_ref[:, lo:hi], preferred_element_type=F32)

    a = ATT_WIDTH
    qa_ref[...] = (proj(0, a) * (D_A ** -0.5 * LOG2E)).astype(BF16)
    k = proj(a, 2 * a)
    kb_ref[...] = k.astype(BF16)
    v = proj(2 * a, 3 * a)
    for h in range(H_A):
        kf_ref[n_prev, :, h, :] = k[:, h * DV_A:(h + 1) * DV_A]
        vf_ref[n_prev, :, h, :] = v[:, h * DV_A:(h + 1) * DV_A]
    vb_ref[...] = v.astype(BF16)
    o = 3 * a
    m = ML_WIDTH
    qkvm_ref[:, 0:m] = proj(o, o + m).astype(BF16)
    qkvm_ref[:, m:2 * m] = (proj(o + m, o + 2 * m) * (D_M ** -0.5)).astype(BF16)
    qkvm_ref[:, 2 * m:3 * m] = proj(o + 2 * m, o + 3 * m).astype(BF16)
    om_ref[...] = proj(o + 3 * m, o + 4 * m)
    bch_ref[...] = proj(o + 4 * m, N_PROJ)
    gt_ref[...] = lax.dot_general(wgt_ref[...], xn, _NT, preferred_element_type=F32)


def _in_proj(x, g, w, wgt, tm, kv_prev):
    t = x.shape[0]
    a, m = ATT_WIDTH, ML_WIDTH
    n_prev = 0 if kv_prev is None else kv_prev[0].shape[0]
    row = lambda n: pl.BlockSpec((tm, n), lambda i: (i, 0))
    slabs = lambda n: pl.BlockSpec((n, tm, H_A, DV_A), lambda i: (0, i, 0, 0))
    outs = [
        (jax.ShapeDtypeStruct((t, a), BF16), row(a)),
        (jax.ShapeDtypeStruct((n_prev + 1, t, H_A, DV_A), F32), slabs(n_prev + 1)),
        (jax.ShapeDtypeStruct((t, a), BF16), row(a)),
        (jax.ShapeDtypeStruct((n_prev + 1, t, H_A, DV_A), F32), slabs(n_prev + 1)),
        (jax.ShapeDtypeStruct((t, a), BF16), row(a)),
        (jax.ShapeDtypeStruct((t, 3 * m), BF16), row(3 * m)),
        (jax.ShapeDtypeStruct((t, m), F32), row(m)),
        (jax.ShapeDtypeStruct((t, 3 * CONV_CH), F32), row(3 * CONV_CH)),
        (jax.ShapeDtypeStruct((8, t), F32), pl.BlockSpec((8, tm), lambda i: (0, i))),
    ]
    carried = [] if kv_prev is None else list(kv_prev)
    return pl.pallas_call(
        _in_proj_kernel,
        grid=(t // tm,),
        in_specs=[row(D_MODEL), _const_spec((1, D_MODEL)), _const_spec((D_MODEL, N_PROJ)),
                  _const_spec((8, D_MODEL))] + [slabs(n_prev)] * len(carried),
        out_specs=[s for _, s in outs],
        out_shape=[s for s, _ in outs],
        compiler_params=pltpu.CompilerParams(dimension_semantics=("parallel",),
                                             vmem_limit_bytes=V7X_VMEM_LIMIT),
        name="in_proj",
    )(x, g, w, wgt, *carried)


def _lambda_value(lp, lam_init):
    a = jnp.sum(lp[0:1] * lp[1:2], axis=1, keepdims=True)
    b = jnp.sum(lp[2:3] * lp[3:4], axis=1, keepdims=True)
    return jnp.exp(a) - jnp.exp(b) + lam_init


def _stack_maps(q):
    lane = lax.broadcasted_iota(jnp.int32, q.shape, 1)
    zero = jnp.zeros_like(q)
    return jnp.concatenate([jnp.where(lane < D_A, q, zero), jnp.where(lane >= D_A, q, zero)], axis=0)


def _attn_finish(acc, lam, g, lam_init, t):
    o = acc[:, :DV_A] / acc[:, DV_A:]
    d = o[:t] - lam * o[t:]
    return (_rms(d, g) * (1.0 - lam_init)).astype(BF16)


def _with_ones(v):
    return jnp.concatenate([v, jnp.ones_like(v)], axis=1)


def _attn_prompt_kernel(lam_ref, g_ref, q_ref, k_ref, v_ref, bias_ref, o_ref,
                        m_scr, l_scr, acc_scr, sa_scr, sb_scr, p_scr, *, t, hp, lam_init):
    i = pl.program_id(2)
    heads = [slice(hh * DV_A, (hh + 1) * DV_A) for hh in range(hp)]
    cols = [slice(hh * 2 * t, (hh + 1) * 2 * t) for hh in range(hp)]
    qz = [_stack_maps(q_ref[:, hd]) for hd in heads]

    def rows(kb):
        return pl.ds(pl.multiple_of(kb * t, t), t)

    def qk(kb):
        r = rows(kb)
        return jnp.concatenate([lax.dot_general(k_ref[r, hd], z, _NT, preferred_element_type=F32)
                                for hd, z in zip(heads, qz)], axis=1)

    def pv(p, kb):
        r = rows(kb)
        return jnp.concatenate([lax.dot_general(v_ref[r, hd], p[:, c], _TN, preferred_element_type=F32)
                                for hd, c in zip(heads, cols)], axis=1)

    def bias(lo):
        return jnp.concatenate([bias_ref[hh, 0, lo:lo + t, :] for hh in range(hp)], axis=1)

    def softmax_step(m_prev, s):
        m_new = jnp.maximum(m_prev, jnp.max(s, axis=0, keepdims=True))
        p = jnp.exp2(s - m_new)
        return m_new, jnp.exp2(m_prev - m_new), p, jnp.sum(p, axis=0, keepdims=True)

    n_far = jnp.maximum(i - 1, 0)
    last = n_far - 1
    kb_prev = jnp.maximum(i - 1, 0)

    s_diag = qk(i)
    s_prev = qk(kb_prev)
    sa_scr[...] = qk(0)
    s_diag = s_diag + bias(t)
    m = jnp.max(s_diag, axis=0, keepdims=True)
    p = jnp.exp2(s_diag - m)
    l = jnp.sum(p, axis=0, keepdims=True)
    acc = pv(p.astype(BF16), i)
    w_prev = (i >= 1).astype(F32)
    m, alpha, p, p_sum = softmax_step(m, s_prev + bias(0))
    m_scr[...] = m
    l_scr[...] = alpha * l + w_prev * p_sum
    acc_scr[...] = alpha * acc
    p_scr[...] = p.astype(BF16)

    def far_pair(j, carry):
        kb1 = jnp.minimum(2 * j + 1, last)
        w1 = (2 * j + 1 <= last).astype(F32)
        kb_pending = jnp.where(j == 0, kb_prev, 2 * j - 1)
        sb_scr[...] = qk(kb1)
        owed = pv(p_scr[...], kb_pending)
        m, alpha, p, p_sum = softmax_step(m_scr[...], sa_scr[...])
        l = alpha * l_scr[...] + p_sum
        acc = alpha * (acc_scr[...] + owed)
        sa_scr[...] = qk(jnp.minimum(2 * j + 2, last))
        owed = pv(p.astype(BF16), 2 * j)
        m, alpha, p, p_sum = softmax_step(m, sb_scr[...])
        m_scr[...] = m
        l_scr[...] = alpha * l + w1 * p_sum
        acc_scr[...] = alpha * (acc + owed)
        p_scr[...] = p.astype(BF16)
        return carry

    trips = (n_far + 1) // 2
    lax.fori_loop(0, trips, far_pair, 0)

    kb_pending = jnp.where(trips == 0, kb_prev, jnp.minimum(2 * trips - 1, last))
    w_pending = jnp.where(trips == 0, w_prev, (n_far % 2 == 0).astype(F32))
    acc = acc_scr[...] + w_pending * pv(p_scr[...], kb_pending)

    lam = _lambda_value(lam_ref[...], lam_init)
    o = acc * (1.0 / l_scr[...])
    for hd, c in zip(heads, cols):
        d = o[:, c][:, :t] - lam * o[:, c][:, t:]
        ms = jnp.mean(d * d, axis=0, keepdims=True)
        y = jnp.transpose(d * lax.rsqrt(ms + EPS))
        o_ref[:, hd] = (y * g_ref[:, hd] * (1.0 - lam_init)).astype(BF16)


def _attn_prompt(q, k, v, bias, lam_p, g_att, batch, seq, t, lam_init):
    nq = seq // t
    hp = ATTN_HEADS_PER_STEP
    w = hp * 2 * t
    kv_spec = pl.BlockSpec((seq, hp * DV_A), lambda b, h, i: (b, h))
    return pl.pallas_call(
        functools.partial(_attn_prompt_kernel, t=t, hp=hp, lam_init=lam_init),
        grid=(batch, H_A // hp, nq),
        in_specs=[_const_spec((4, D_A)),
                  pl.BlockSpec((1, hp * DV_A), lambda b, h, i: (0, h)),
                  pl.BlockSpec((t, hp * DV_A), lambda b, h, i: (b * nq + i, h)),
                  kv_spec, kv_spec,
                  pl.BlockSpec((hp, 1, 2 * t, 2 * t), lambda b, h, i: (h, jnp.minimum(i, 1), 0, 0))],
        out_specs=pl.BlockSpec((t, hp * DV_A), lambda b, h, i: (b * nq + i, h)),
        out_shape=jax.ShapeDtypeStruct((batch * seq, ATT_WIDTH), BF16),
        scratch_shapes=[pltpu.VMEM((1, w), F32), pltpu.VMEM((1, w), F32),
                        pltpu.VMEM((DV_A, w), F32),
                        pltpu.VMEM((t, w), F32), pltpu.VMEM((t, w), F32),
                        pltpu.VMEM((t, w), BF16)],
        compiler_params=pltpu.CompilerParams(dimension_semantics=("parallel", "parallel", "arbitrary"),
                                             vmem_limit_bytes=V7X_VMEM_LIMIT),
        name="attn_prompt",
    )(lam_p, g_att, q, k, v, bias)


def _attn_sample_kernel(lam_ref, g_ref, q_ref, kp_ref, vp_ref, kn_ref, vn_ref, bp_ref, bn_ref, o_ref,
                        *, lq, lam_init):
    lam = _lambda_value(lam_ref[...], lam_init)
    for h in range(H_A):
        hd = slice(h * DV_A, (h + 1) * DV_A)
        qz = _stack_maps(q_ref[:, hd])
        bp = bp_ref[h]
        bn = bn_ref[h]
        past_rows = pl.ds(h, kp_ref.shape[1] // H_A, stride=H_A)
        sp = lax.dot_general(qz, kp_ref[0, past_rows, :].astype(BF16), _NT, preferred_element_type=F32)
        sp = sp + jnp.concatenate([bp, bp], axis=0)
        sn = lax.dot_general(qz, kn_ref[:, hd], _NT, preferred_element_type=F32)
        sn = sn + jnp.concatenate([bn, bn], axis=0)
        m = jnp.maximum(jnp.max(sp, axis=1, keepdims=True), jnp.max(sn, axis=1, keepdims=True))
        pp = jnp.exp2(sp - m)
        pn = jnp.exp2(sn - m)
        acc = (jnp.dot(pp.astype(BF16), _with_ones(vp_ref[0, past_rows, :].astype(BF16)), preferred_element_type=F32)
               + jnp.dot(pn.astype(BF16), _with_ones(vn_ref[:, hd]), preferred_element_type=F32))
        o_ref[:, hd] = _attn_finish(acc, lam, g_ref[:, hd], lam_init, lq)


def _attn_sample(q, k_new, v_new, k_cache, v_cache, layer, bias_past, bias_new, lam_p, g_att, batch, lq, lam_init):
    depth, _, past = k_cache.shape[:3]
    k_cache = k_cache.reshape(depth, batch, past * H_A, DV_A)
    v_cache = v_cache.reshape(depth, batch, past * H_A, DV_A)
    new_spec = pl.BlockSpec((lq, ATT_WIDTH), lambda b: (b, 0))
    past_spec = pl.BlockSpec((None, 1, past * H_A, DV_A), lambda b: (layer, b, 0, 0))
    return pl.pallas_call(
        functools.partial(_attn_sample_kernel, lq=lq, lam_init=lam_init),
        grid=(batch,),
        in_specs=[_const_spec((4, D_A)), _const_spec((1, ATT_WIDTH)),
                  new_spec, past_spec, past_spec, new_spec, new_spec,
                  _const_spec((H_A, lq, past)), _const_spec((H_A, lq, lq))],
        out_specs=new_spec,
        out_shape=jax.ShapeDtypeStruct((batch * lq, ATT_WIDTH), BF16),
        compiler_params=pltpu.CompilerParams(dimension_semantics=("parallel",),
                                             vmem_limit_bytes=V7X_VMEM_LIMIT),
        name="attn_sample",
    )(lam_p, g_att, q, k_cache, v_cache, k_new, v_new, bias_past, bias_new)


def _per_stream(fn, *arrays):
    return jnp.stack([fn(*(a[g] for a in arrays)) for g in range(arrays[0].shape[0])])


def _recurrent_kernel(gb_ref, q_ref, k_ref, v_ref, og_ref, gt_ref, b_ref, c_ref, hc_ref, cw_ref, gml_ref,
                      s0_ref, m0_ref, cv0_ref,
                      hm_ref, oc_ref, s_ref, m_ref, cv_ref, *, L):
    @pl.when(pl.program_id(1) == 0)
    def _():
        s_ref[...] = s0_ref[...]
        m_ref[...] = m0_ref[...]
        cv_ref[...] = cv0_ref[...]

    q = q_ref[...]
    k = k_ref[...]
    v = v_ref[...]
    G = q.shape[0]
    state = s_ref[...]
    q_state = _per_stream(lambda a, b: jnp.dot(a, b, preferred_element_type=F32),
                          q, state.astype(BF16))
    q_c = q_state[:, :, :ML_WIDTH]

    lane_head = lax.broadcasted_iota(jnp.int32, (1, L, ML_WIDTH), 2) // D_M
    row = lax.broadcasted_iota(jnp.int32, (1, L, L), 1)
    col = lax.broadcasted_iota(jnp.int32, (1, L, L), 2)
    causal = col <= row
    eye = col == row
    st_lane = lax.broadcasted_iota(jnp.int32, (1, 1, ML_STATE_W), 2)
    st_lane_head = jnp.where(st_lane < ML_WIDTH, st_lane // D_M, st_lane - ML_WIDTH)
    m_lane = lax.broadcasted_iota(jnp.int32, (1, 1, 128), 2)

    gt = gt_ref[:, 0]
    m_all = m_ref[...]
    num = jnp.zeros((G, L, ML_WIDTH), F32)
    w_state = jnp.zeros((G, L, ML_WIDTH), F32)
    decay = jnp.zeros((G, 1, ML_STATE_W), F32)
    m_next = jnp.zeros((G, 1, 128), F32)
    for h in range(H_M):
        ig = gt[:, h:h + 1, :] + gb_ref[0, h]
        fz = gt[:, H_M + h:H_M + h + 1, :] + gb_ref[1, h]
        lf = jnp.minimum(fz, 0.0) - jnp.log1p(jnp.exp(-jnp.abs(fz)))
        f_col = jnp.sum(jnp.where(causal, lf, 0.0), axis=2, keepdims=True)
        f_row = jnp.sum(jnp.where(eye, f_col, 0.0), axis=1, keepdims=True)
        d = jnp.where(causal, f_col + (ig - f_row), -jnp.inf)
        m_prev = m_all[:, :, h:h + 1]
        inter = f_col + m_prev
        mt = jnp.maximum(inter, jnp.max(d, axis=2, keepdims=True))
        w_intra = jnp.exp(d - mt)
        w_inter = jnp.exp(inter - mt)
        head = lane_head == h
        qk = _per_stream(lambda a, b: lax.dot_general(a, b, _NT, preferred_element_type=F32),
                         jnp.where(head, q, jnp.zeros_like(q)), k)
        s = qk * w_intra
        den = jnp.sum(s, axis=2, keepdims=True) + w_inter * q_state[:, :, ML_WIDTH + h:ML_WIDTH + h + 1]
        inv = 1.0 / jnp.maximum(jnp.abs(den), jnp.exp(-mt))
        sv = _per_stream(lambda a, b: jnp.dot(a, b, preferred_element_type=F32), s.astype(BF16), v)
        num = jnp.where(head, (sv + w_inter * q_c) * inv, num)
        m_new = mt[:, L - 1:L, :]
        w_col = jnp.sum(jnp.where(eye, w_intra[:, L - 1:L, :], 0.0), axis=2, keepdims=True)
        w_state = jnp.where(head, w_col, w_state)
        dec = jnp.exp(f_col[:, L - 1:L, :] + m_prev - m_new)
        decay = jnp.where(st_lane_head == h, dec, decay)
        m_next = jnp.where(m_lane == h, m_new, m_next)

    kw = (k.astype(F32) * w_state).astype(BF16)
    v_aug = jnp.concatenate([v, jnp.ones((G, L, ML_STATE_W - ML_WIDTH), BF16)], axis=2)
    upd = _per_stream(lambda a, b: lax.dot_general(a, b, _TN, preferred_element_type=F32), kw, v_aug)
    s_row_head = lax.broadcasted_iota(jnp.int32, (1, ML_WIDTH, ML_STATE_W), 1) // D_M
    s_col = lax.broadcasted_iota(jnp.int32, (1, ML_WIDTH, ML_STATE_W), 2)
    s_col_head = jnp.where(s_col < ML_WIDTH, s_col // D_M, s_col - ML_WIDTH)
    s_ref[...] = decay * state + jnp.where(s_row_head == s_col_head, upd, 0.0)
    m_ref[...] = m_next

    hg = num * jax.nn.sigmoid(og_ref[...])
    scale = jnp.zeros((G, L, ML_WIDTH), F32)
    for h in range(H_M):
        head = lane_head == h
        ss = jnp.sum(jnp.where(head, hg * hg, 0.0), axis=2, keepdims=True)
        scale = jnp.where(head, lax.rsqrt(ss * (1.0 / D_M) + EPS), scale)
    hm_ref[...] = (hg * scale * gml_ref[...]).astype(BF16)

    u = c_ref[...] * hc_ref[...]
    prev = cv_ref[...]
    ri = lax.broadcasted_iota(jnp.int32, (1, L, CONV_CH), 1)
    u1 = jnp.where(ri == 0, prev[:, 1:2], _per_stream(lambda a: pltpu.roll(a, 1, 0), u))
    u2 = jnp.where(ri == 0, prev[:, 0:1],
                   jnp.where(ri == 1, prev[:, 1:2], _per_stream(lambda a: pltpu.roll(a, 2, 0), u)))
    w = cw_ref[...]
    oc_ref[...] = (b_ref[...] * (w[0:1] * u2 + w[1:2] * u1 + w[2:3] * u)).astype(BF16)
    cv_ref[:, 0:1, :] = u[:, L - 2:L - 1]
    cv_ref[:, 1:2, :] = u[:, L - 1:L]


def _recurrent(gate_b, qkvm, og, gt, bch, conv_w, g_ml, s0, m0, cv0, batch, seq, L, G):
    assert batch % G == 0 and seq % L == 0
    nc = seq // L
    tok = lambda j: pl.BlockSpec((G, L, ML_WIDTH), lambda b, c: (b, c, j))
    per_b = lambda shape: pl.BlockSpec((G,) + shape, lambda b, c: (b, 0, 0))
    return pl.pallas_call(
        functools.partial(_recurrent_kernel, L=L),
        grid=(batch // G, nc),
        in_specs=[pl.BlockSpec(memory_space=pltpu.SMEM),
                  tok(0), tok(1), tok(2), tok(0),
                  pl.BlockSpec((G, 1, 8, L), lambda b, c: (b, c, 0, 0)),
                  tok(0), tok(1), tok(2),
                  _const_spec((CONV_K, CONV_CH)), _const_spec((1, ML_WIDTH)),
                  per_b((ML_WIDTH, ML_STATE_W)), per_b((1, 128)), per_b((CONV_K - 1, CONV_CH))],
        out_specs=[tok(0), tok(0),
                   per_b((ML_WIDTH, ML_STATE_W)), per_b((1, 128)), per_b((CONV_K - 1, CONV_CH))],
        out_shape=[jax.ShapeDtypeStruct((batch, seq, ML_WIDTH), BF16),
                   jax.ShapeDtypeStruct((batch, seq, CONV_CH), BF16),
                   jax.ShapeDtypeStruct((batch, ML_WIDTH, ML_STATE_W), F32),
                   jax.ShapeDtypeStruct((batch, 1, 128), F32),
                   jax.ShapeDtypeStruct((batch, CONV_K - 1, CONV_CH), F32)],
        compiler_params=pltpu.CompilerParams(dimension_semantics=("parallel", "arbitrary"),
                                             vmem_limit_bytes=V7X_VMEM_LIMIT),
        name="recurrent",
    )(gate_b, qkvm, qkvm, qkvm, og, gt, bch, bch, bch, conv_w, g_ml, s0, m0, cv0)


def _mem_kv_kernel(x_ref, w_ref, kf_ref, vf_ref, kb_ref, vb_ref):
    x = x_ref[...].astype(BF16)
    n = H_X * D_X
    k = jnp.dot(x, w_ref[:, :n], preferred_element_type=F32)
    v = jnp.dot(x, w_ref[:, n:], preferred_element_type=F32)
    kf_ref[...] = k
    vf_ref[...] = v
    kb_ref[...] = k.astype(BF16)
    vb_ref[...] = v.astype(BF16)


def _mem_kv(mem, w_kv, tm):
    t = mem.shape[0]
    n = H_X * D_X
    row = pl.BlockSpec((tm, n), lambda i: (i, 0))
    return pl.pallas_call(
        _mem_kv_kernel,
        grid=(t // tm,),
        in_specs=[pl.BlockSpec((tm, D_MODEL), lambda i: (i, 0)), _const_spec((D_MODEL, 2 * n))],
        out_specs=[row, row, row, row],
        out_shape=[jax.ShapeDtypeStruct((t, n), F32), jax.ShapeDtypeStruct((t, n), F32),
                   jax.ShapeDtypeStruct((t, n), BF16), jax.ShapeDtypeStruct((t, n), BF16)],
        compiler_params=pltpu.CompilerParams(dimension_semantics=("parallel",),
                                             vmem_limit_bytes=V7X_VMEM_LIMIT),
        name="mem_kv",
    )(mem, w_kv)


FF_CHUNK = 256


def _post_kernel(x_ref, oa_ref, hm_ref, oc_ref, mk_ref, mv_ref, wout_ref, gc_ref, wcq_ref, wco_ref,
                 gf_ref, wg_ref, wu_ref, wd_ref, gfin_ref, o_ref, *, final):
    a, m = ATT_WIDTH, ML_WIDTH
    x = x_ref[...]
    x = x + (jnp.dot(oa_ref[...], wout_ref[0:a, :], preferred_element_type=F32)
             + jnp.dot(hm_ref[...], wout_ref[a:a + m, :], preferred_element_type=F32)
             + jnp.dot(oc_ref[...], wout_ref[a + m:, :], preferred_element_type=F32))

    xn = _rms(x, gc_ref[...]).astype(BF16)
    qc = (jnp.dot(xn, wcq_ref[...], preferred_element_type=F32) * (D_X ** -0.5)).astype(BF16)
    cross = jnp.zeros_like(x)
    n_streams = mk_ref.shape[0]
    rows = x.shape[0] // n_streams
    heads = [slice(h * D_X, (h + 1) * D_X) for h in range(H_X)]
    streams = [slice(b * rows, (b + 1) * rows) for b in range(n_streams)]
    scores = [[lax.dot_general(qc[r, sl], mk_ref[b, :, sl], _NT, preferred_element_type=F32)
               for b, r in enumerate(streams)] for sl in heads]
    for sl, s_head in zip(heads, scores):
        outs = []
        for b, s in enumerate(s_head):
            p = jnp.exp(s - jnp.max(s, axis=1, keepdims=True))
            p = p * (1.0 / jnp.sum(p, axis=1, keepdims=True))
            outs.append(jnp.dot(p.astype(BF16), mv_ref[b, :, sl], preferred_element_type=F32).astype(BF16))
        o = outs[0] if n_streams == 1 else jnp.concatenate(outs, axis=0)
        cross = cross + jnp.dot(o, wco_ref[sl, :], preferred_element_type=F32)
    x = x + cross

    xn = _rms(x, gf_ref[...]).astype(BF16)
    ff = jnp.zeros_like(x)
    for j in range(D_FF // FF_CHUNK):
        sl = slice(j * FF_CHUNK, (j + 1) * FF_CHUNK)
        g = jnp.dot(xn, wg_ref[:, sl], preferred_element_type=F32)
        u = jnp.dot(xn, wu_ref[:, sl], preferred_element_type=F32)
        act = (g * jax.nn.sigmoid(g) * u).astype(BF16)
        ff = ff + jnp.dot(act, wd_ref[sl, :], preferred_element_type=F32)
    x = x + ff
    if final:
        x = _rms(x, gfin_ref[...])
    o_ref[...] = x


def _post(x, oa, hm, oc, mk, mv, w_out, g_cross, w_cq, w_co, g_ffn, w_gate, w_up, w_down, g_final,
          tm, seq, final):
    t = x.shape[0]
    row = lambda n: pl.BlockSpec((tm, n), lambda i: (i, 0))
    if tm <= seq:
        tiles_per_stream = seq // tm
        mem = pl.BlockSpec((1,) + mk.shape[1:], lambda i: (i // tiles_per_stream, 0, 0))
    else:
        mem = pl.BlockSpec((tm // seq,) + mk.shape[1:], lambda i: (i, 0, 0))
    vec = _const_spec((1, D_MODEL))
    return pl.pallas_call(
        functools.partial(_post_kernel, final=final),
        grid=(t // tm,),
        in_specs=[row(D_MODEL), row(ATT_WIDTH), row(ML_WIDTH), row(CONV_CH), mem, mem,
                  _const_spec(w_out.shape), vec, _const_spec(w_cq.shape), _const_spec(w_co.shape),
                  vec, _const_spec(w_gate.shape), _const_spec(w_up.shape), _const_spec(w_down.shape), vec],
        out_specs=row(D_MODEL),
        out_shape=jax.ShapeDtypeStruct((t, D_MODEL), F32),
        compiler_params=pltpu.CompilerParams(dimension_semantics=("parallel",),
                                             vmem_limit_bytes=V7X_VMEM_LIMIT),
        name="post",
    )(x, oa, hm, oc, mk, mv, w_out, g_cross, w_cq, w_co, g_ffn, w_gate, w_up, w_down, g_final)


def _rel_bucket(rel):
    half = NUM_BUCKETS // 2
    max_exact = half // 2
    n = jnp.abs(rel)
    large = max_exact + (jnp.log(jnp.maximum(n, 1).astype(F32) / max_exact)
                         / math.log(MAX_DISTANCE / max_exact) * (half - max_exact)).astype(jnp.int32)
    large = jnp.minimum(large, half - 1)
    return jnp.where(rel > 0, half, 0) + jnp.where(n < max_exact, n, large)


def _bias_table(rel_bias, rel):
    bucket = _rel_bucket(rel)[None]
    table = rel_bias.astype(F32)
    out = jnp.zeros((table.shape[1],) + rel.shape, F32)
    for b in range(NUM_BUCKETS):
        out = jnp.where(bucket == b, table[b][:, None, None], out)
    return out


def _prompt_bias_tiles(rel_bias, t):
    key = jnp.arange(t, dtype=jnp.int32)[:, None]
    qry = jnp.arange(t, dtype=jnp.int32)[None, :]
    far = rel_bias[_rel_bucket(jnp.int32(-MAX_DISTANCE))].astype(F32)[:, None, None]
    prev = (_bias_table(rel_bias, key - t - qry) - far) * LOG2E
    diag = jnp.where((key // CHUNK) <= (qry // CHUNK), (_bias_table(rel_bias, key - qry) - far) * LOG2E, -jnp.inf)
    both_maps = lambda a: jnp.concatenate([a, a], axis=2)
    tiles = jnp.stack([jnp.concatenate([diag, diag], axis=1), jnp.concatenate([prev, diag], axis=1)], axis=1)
    return lax.optimization_barrier(both_maps(tiles.reshape(-1, 2 * t, t)).reshape(-1, 2, 2 * t, 2 * t))


def _block_diag_state(c, n):
    b = c.shape[0]
    eye = jnp.eye(H_M, dtype=F32)
    cbd = jnp.einsum('bhdv,hg->bhdgv', c.astype(F32), eye).reshape(b, ML_WIDTH, ML_WIDTH)
    ncol = jnp.einsum('bhd,hg->bhdg', n.astype(F32), eye).reshape(b, ML_WIDTH, H_M)
    pad = jnp.zeros((b, ML_WIDTH, ML_STATE_W - ML_WIDTH - H_M), F32)
    return jnp.concatenate([cbd, ncol, pad], axis=2)


def _unpack_state(s, m):
    b = s.shape[0]
    blocks = s[:, :, :ML_WIDTH].reshape(b, H_M, D_M, H_M, D_M)
    c = jnp.stack([blocks[:, h, :, h, :] for h in range(H_M)], axis=1)
    ncols = s[:, :, ML_WIDTH:ML_WIDTH + H_M].reshape(b, H_M, D_M, H_M)
    n = jnp.stack([ncols[:, h, :, h] for h in range(H_M)], axis=1)
    return c, n, m[:, 0, :H_M]


def _layer(x, layer, depth, kv_all, wts, attn_fn, mem_k, mem_v, ml_state, conv_prev, g_final,
           batch, seq, tm_proj, L, tm_post):
    final = layer == depth - 1
    (g_mix, w_proj, w_gt, gate_b, conv_w, lam_p, g_att, g_ml, w_out,
     g_cross, w_cq, w_co, g_ffn, w_gate, w_up, w_down) = wts
    t = batch * seq
    lam_init = 0.8 - 0.6 * math.exp(-0.3 * layer)
    qa, kf, kb, vf, vb, qkvm, og, bch, gt = _in_proj(x, g_mix, w_proj, w_gt, tm_proj, kv_all)
    oa = attn_fn(qa, kb, vb, lam_p, g_att, lam_init)
    gt = jnp.transpose(gt.reshape(8, batch, seq // L, L), (1, 2, 0, 3))
    tok3 = lambda a: a.reshape(batch, seq, a.shape[-1])
    hm, oc, s1, m1, cv1 = _recurrent(gate_b, tok3(qkvm), tok3(og), gt, tok3(bch), conv_w, g_ml,
                                     ml_state[0], ml_state[1], conv_prev, batch, seq, L,
                                     min(RECURRENT_STREAMS, batch))
    hm, oc = hm.reshape(t, ML_WIDTH), oc.reshape(t, CONV_CH)
    x = _post(x, oa, hm, oc, mem_k, mem_v, w_out, g_cross, w_cq, w_co, g_ffn, w_gate, w_up, w_down,
              g_final, tm_post, seq, final)
    c1, n1, mm1 = _unpack_state(s1, m1)
    return x, (kf, vf), c1, n1, mm1, cv1


def kernel(x_prompt, x_sample, mem_prompt, cache_att_k, cache_att_v, cache_mem_k, cache_mem_v,
           state_mlstm_C, state_mlstm_n, state_mlstm_m, state_conv,
           norm_mix, w_in, mlstm_gate_bias, conv_w, lambda_params, norm_att_heads, norm_mlstm_heads,
           w_out, norm_cross, w_cq, w_ck, w_cv, w_co, norm_ffn, w_gate, w_up, w_down,
           rel_bias, norm_final):
    bp, sp, _ = x_prompt.shape
    bs, ss, _ = x_sample.shape
    depth = w_in.shape[0]
    past = cache_att_k.shape[2]
    n_mem = mem_prompt.shape[1]
    xp = x_prompt.reshape(bp * sp, D_MODEL)
    xs = x_sample.reshape(bs * ss, D_MODEL)
    mem = mem_prompt.reshape(bp * n_mem, D_MODEL)
    g_final = norm_final.reshape(1, D_MODEL)

    bias_tiles = _prompt_bias_tiles(rel_bias, ATTN_TILE)
    rel_s = jnp.arange(past + ss, dtype=jnp.int32)[None, :] - (past + jnp.arange(ss, dtype=jnp.int32))[:, None]
    bias_s = _bias_table(rel_bias, rel_s) * LOG2E
    bias_s_past, bias_s_new = lax.optimization_barrier((bias_s[:, :, :past], bias_s[:, :, past:]))

    zero_state = (jnp.zeros((bp, ML_WIDTH, ML_STATE_W), F32), jnp.zeros((bp, 1, 128), F32))
    zero_conv = jnp.zeros((bp, CONV_K - 1, CONV_CH), F32)

    outs = {k: [] for k in ('pmk', 'pmv', 'pC', 'pn', 'pm', 'pcv', 'sC', 'sn', 'sm', 'scv')}
    kv_p = kv_s = None
    gate_lo = 3 * ATT_WIDTH + 4 * ML_WIDTH
    for l in range(depth):
        w = w_in[l]
        wts = (norm_mix[l].reshape(1, D_MODEL),
               jnp.concatenate([w[:, :gate_lo], w[:, gate_lo + 2 * H_M:]], axis=1).astype(BF16),
               w[:, gate_lo:gate_lo + 2 * H_M].T.astype(BF16),
               mlstm_gate_bias[l].astype(F32), conv_w[l].astype(F32), lambda_params[l].astype(F32),
               norm_att_heads[l].reshape(1, ATT_WIDTH), norm_mlstm_heads[l].reshape(1, ML_WIDTH),
               w_out[l].astype(BF16), norm_cross[l].reshape(1, D_MODEL),
               w_cq[l].astype(BF16), w_co[l].astype(BF16), norm_ffn[l].reshape(1, D_MODEL),
               w_gate[l].astype(BF16), w_up[l].astype(BF16), w_down[l].astype(BF16))

        w_kv = jnp.concatenate([w_ck[l], w_cv[l]], axis=1).astype(BF16)
        mkf, mvf, mkb, mvb = _mem_kv(mem, w_kv, TOKEN_TILE)
        attn_p = lambda q, k, v, lam_p, g, li: _attn_prompt(q, k, v, bias_tiles, lam_p, g, bp, sp, ATTN_TILE, li)
        xp, kv_p, c1, n1, m1, cv1 = _layer(
            xp, l, depth, kv_p, wts, attn_p, mkb.reshape(bp, n_mem, -1), mvb.reshape(bp, n_mem, -1),
            zero_state, zero_conv, g_final, bp, sp, TOKEN_TILE, MLSTM_CHUNK, TOKEN_TILE)
        outs['pmk'].append(mkf.reshape(bp, n_mem, H_X, D_X))
        outs['pmv'].append(mvf.reshape(bp, n_mem, H_X, D_X))
        outs['pC'].append(c1); outs['pn'].append(n1); outs['pm'].append(m1); outs['pcv'].append(cv1)

        attn_s = lambda q, k, v, lam_p, g, li, l=l: _attn_sample(q, k, v, cache_att_k, cache_att_v, l,
                                                                 bias_s_past, bias_s_new, lam_p, g, bs, ss, li)
        s0 = _block_diag_state(state_mlstm_C[l], state_mlstm_n[l])
        m0 = jnp.pad(state_mlstm_m[l].astype(F32), ((0, 0), (0, 128 - H_M))).reshape(bs, 1, 128)
        xs, kv_s, c1, n1, m1, cv1 = _layer(
            xs, l, depth, kv_s, wts, attn_s,
            cache_mem_k[l].reshape(bs, n_mem, -1).astype(BF16), cache_mem_v[l].reshape(bs, n_mem, -1).astype(BF16),
            (s0, m0), state_conv[l].astype(F32), g_final, bs, ss, bs * ss, ss, bs * ss)
        outs['sC'].append(c1); outs['sn'].append(n1); outs['sm'].append(m1); outs['scv'].append(cv1)

    st = lambda k: jnp.stack(outs[k])
    per_head = lambda a, b, s: a.reshape(depth, b, s, H_A, DV_A)
    return (xp.reshape(bp, sp, D_MODEL), xs.reshape(bs, ss, D_MODEL),
            per_head(kv_p[0], bp, sp), per_head(kv_p[1], bp, sp),
            st('pmk'), st('pmv'), st('pC'), st('pn'), st('pm'), st('pcv'),
            per_head(kv_s[0], bs, ss), per_head(kv_s[1], bs, ss),
            st('sC'), st('sn'), st('sm'), st('scv'))
```

```python
import functools
import math

import jax
import jax.numpy as jnp
from jax import lax
from jax.experimental import pallas as pl
from jax.experimental.pallas import tpu as pltpu

F32 = jnp.float32
BF16 = jnp.bfloat16

D_MODEL = 1024
CHUNK = 64
H_A = 4
D_A = 64
DV_A = 128
ATT_WIDTH = H_A * DV_A
H_M = 4
D_M = 64
ML_WIDTH = H_M * D_M
CONV_CH = 256
CONV_K = 3
H_X = 4
D_X = 256
D_FF = 2816
NUM_BUCKETS = 32
MAX_DISTANCE = 128
EPS = 1e-6

N_PROJ = 3 * ATT_WIDTH + 4 * ML_WIDTH + 3 * CONV_CH
ML_STATE_W = ML_WIDTH + 128

V7X_VMEM_LIMIT = 56 * 1024 * 1024

LOG2E = math.log2(math.e)
RECURRENT_STREAMS = 8
ATTN_HEADS_PER_STEP = 4
TOKEN_TILE = 512
ATTN_TILE = 256
MLSTM_CHUNK = 128

_NT = (((1,), (1,)), ((), ()))
_TN = (((0,), (0,)), ((), ()))


def _const_spec(shape):
    nd = len(shape)
    return pl.BlockSpec(shape, lambda *_: (0,) * nd, pipeline_mode=pl.Buffered(1))


def _rms(x, g):
    ms = jnp.mean(x * x, axis=-1, keepdims=True)
    return x * lax.rsqrt(ms + EPS) * g


def _in_proj_kernel(x_ref, g_ref, w_ref, wgt_ref, *refs):
    qa_ref, kf_ref, kb_ref, vf_ref, vb_ref, qkvm_ref, om_ref, bch_ref, gt_ref = refs[-9:]
    n_prev = kf_ref.shape[0] - 1
    if n_prev:
        kf_ref[0:n_prev] = refs[0][...]
        vf_ref[0:n_prev] = refs[1][...]
    xn = _rms(x_ref[...], g_ref[...]).astype(BF16)

    def proj(lo, hi):
        return jnp.dot(xn, w_ref[:, lo:hi], preferred_element_type=F32)

    a = ATT_WIDTH
    qa_ref[...] = (proj(0, a) * (D_A ** -0.5 * LOG2E)).astype(BF16)
    k = proj(a, 2 * a)
    kb_ref[...] = k.astype(BF16)
    v = proj(2 * a, 3 * a)
    for h in range(H_A):
        head_rows = pl.ds(h, k.shape[0], stride=H_A)
        kf_ref[n_prev, head_rows, :] = k[:, h * DV_A:(h + 1) * DV_A]
        vf_ref[n_prev, head_rows, :] = v[:, h * DV_A:(h + 1) * DV_A]
    vb_ref[...] = v.astype(BF16)
    o = 3 * a
    m = ML_WIDTH
    qkvm_ref[:, 0:m] = proj(o, o + m).astype(BF16)
    qkvm_ref[:, m:2 * m] = (proj(o + m, o + 2 * m) * (D_M ** -0.5)).astype(BF16)
    qkvm_ref[:, 2 * m:3 * m] = proj(o + 2 * m, o + 3 * m).astype(BF16)
    om_ref[...] = proj(o + 3 * m, o + 4 * m)
    bch_ref[...] = proj(o + 4 * m, N_PROJ)
    gt_ref[...] = lax.dot_general(wgt_ref[...], xn, _NT, preferred_element_type=F32)


def _in_proj(x, g, w, wgt, tm, kv_prev):
    t = x.shape[0]
    a, m = ATT_WIDTH, ML_WIDTH
    n_prev = 0 if kv_prev is None else kv_prev[0].shape[0]
    row = lambda n: pl.BlockSpec((tm, n), lambda i: (i, 0))
    slabs = lambda n: pl.BlockSpec((n, tm * H_A, DV_A), lambda i: (0, i, 0))
    outs = [
        (jax.ShapeDtypeStruct((t, a), BF16), row(a)),
        (jax.ShapeDtypeStruct((n_prev + 1, t * H_A, DV_A), F32), slabs(n_prev + 1)),
        (jax.ShapeDtypeStruct((t, a), BF16), row(a)),
        (jax.ShapeDtypeStruct((n_prev + 1, t * H_A, DV_A), F32), slabs(n_prev + 1)),
        (jax.ShapeDtypeStruct((t, a), BF16), row(a)),
        (jax.ShapeDtypeStruct((t, 3 * m), BF16), row(3 * m)),
        (jax.ShapeDtypeStruct((t, m), F32), row(m)),
        (jax.ShapeDtypeStruct((t, 3 * CONV_CH), F32), row(3 * CONV_CH)),
        (jax.ShapeDtypeStruct((8, t), F32), pl.BlockSpec((8, tm), lambda i: (0, i))),
    ]
    carried = [] if kv_prev is None else list(kv_prev)
    return pl.pallas_call(
        _in_proj_kernel,
        grid=(t // tm,),
        in_specs=[row(D_MODEL), _const_spec((1, D_MODEL)), _const_spec((D_MODEL, N_PROJ)),
                  _const_spec((8, D_MODEL))] + [slabs(n_prev)] * len(carried),
        out_specs=[s for _, s in outs],
        out_shape=[s for s, _ in outs],
        compiler_params=pltpu.CompilerParams(dimension_semantics=("parallel",),
                                             vmem_limit_bytes=V7X_VMEM_LIMIT),
        name="in_proj",
    )(x, g, w, wgt, *carried)


def _lambda_value(lp, lam_init):
    a = jnp.sum(lp[0:1] * lp[1:2], axis=1, keepdims=True)
    b = jnp.sum(lp[2:3] * lp[3:4], axis=1, keepdims=True)
    return jnp.exp(a) - jnp.exp(b) + lam_init


def _stack_maps(q):
    lane = lax.broadcasted_iota(jnp.int32, q.shape, 1)
    zero = jnp.zeros_like(q)
    return jnp.concatenate([jnp.where(lane < D_A, q, zero), jnp.where(lane >= D_A, q, zero)], axis=0)


def _attn_finish(acc, lam, g, lam_init, t):
    o = acc[:, :DV_A] / acc[:, DV_A:]
    d = o[:t] - lam * o[t:]
    return (_rms(d, g) * (1.0 - lam_init)).astype(BF16)


def _with_ones(v):
    return jnp.concatenate([v, jnp.ones_like(v)], axis=1)


def _attn_prompt_kernel(lam_ref, g_ref, q_ref, k_ref, v_ref, bias_ref, o_ref,
                        m_scr, l_scr, acc_scr, sa_scr, sb_scr, p_scr, *, t, hp, lam_init):
    i = pl.program_id(2)
    heads = [slice(hh * DV_A, (hh + 1) * DV_A) for hh in range(hp)]
    cols = [slice(hh * 2 * t, (hh + 1) * 2 * t) for hh in range(hp)]
    qz = [_stack_maps(q_ref[:, hd]) for hd in heads]

    def rows(kb):
        return pl.ds(pl.multiple_of(kb * t, t), t)

    def qk(kb):
        r = rows(kb)
        return jnp.concatenate([lax.dot_general(k_ref[r, hd], z, _NT, preferred_element_type=F32)
                                for hd, z in zip(heads, qz)], axis=1)

    def pv(p, kb):
        r = rows(kb)
        return jnp.concatenate([lax.dot_general(v_ref[r, hd], p[:, c], _TN, preferred_element_type=F32)
                                for hd, c in zip(heads, cols)], axis=1)

    def bias(lo):
        return jnp.concatenate([bias_ref[hh, 0, lo:lo + t, :] for hh in range(hp)], axis=1)

    def softmax_step(m_prev, s):
        m_new = jnp.maximum(m_prev, jnp.max(s, axis=0, keepdims=True))
        p = jnp.exp2(s - m_new)
        return m_new, jnp.exp2(m_prev - m_new), p, jnp.sum(p, axis=0, keepdims=True)

    n_far = jnp.maximum(i - 1, 0)
    last = n_far - 1
    kb_prev = jnp.maximum(i - 1, 0)

    s_diag = qk(i)
    s_prev = qk(kb_prev)
    sa_scr[...] = qk(0)
    s_diag = s_diag + bias(t)
    m = jnp.max(s_diag, axis=0, keepdims=True)
    p = jnp.exp2(s_diag - m)
    l = jnp.sum(p, axis=0, keepdims=True)
    acc = pv(p.astype(BF16), i)
    w_prev = (i >= 1).astype(F32)
    m, alpha, p, p_sum = softmax_step(m, s_prev + bias(0))
    m_scr[...] = m
    l_scr[...] = alpha * l + w_prev * p_sum
    acc_scr[...] = alpha * acc
    p_scr[...] = p.astype(BF16)

    def far_pair(j, carry):
        kb1 = jnp.minimum(2 * j + 1, last)
        w1 = (2 * j + 1 <= last).astype(F32)
        kb_pending = jnp.where(j == 0, kb_prev, 2 * j - 1)
        sb_scr[...] = qk(kb1)
        owed = pv(p_scr[...], kb_pending)
        m, alpha, p, p_sum = softmax_step(m_scr[...], sa_scr[...])
        l = alpha * l_scr[...] + p_sum
        acc = alpha * (acc_scr[...] + owed)
        sa_scr[...] = qk(jnp.minimum(2 * j + 2, last))
        owed = pv(p.astype(BF16), 2 * j)
        m, alpha, p, p_sum = softmax_step(m, sb_scr[...])
        m_scr[...] = m
        l_scr[...] = alpha * l + w1 * p_sum
        acc_scr[...] = alpha * (acc + owed)
        p_scr[...] = p.astype(BF16)
        return carry

    trips = (n_far + 1) // 2
    lax.fori_loop(0, trips, far_pair, 0)

    kb_pending = jnp.where(trips == 0, kb_prev, jnp.minimum(2 * trips - 1, last))
    w_pending = jnp.where(trips == 0, w_prev, (n_far % 2 == 0).astype(F32))
    acc = acc_scr[...] + w_pending * pv(p_scr[...], kb_pending)

    lam = _lambda_value(lam_ref[...], lam_init)
    o = acc * (1.0 / l_scr[...])
    for hd, c in zip(heads, cols):
        d = o[:, c][:, :t] - lam * o[:, c][:, t:]
        ms = jnp.mean(d * d, axis=0, keepdims=True)
        y = jnp.transpose(d * lax.rsqrt(ms + EPS))
        o_ref[:, hd] = (y * g_ref[:, hd] * (1.0 - lam_init)).astype(BF16)


def _attn_prompt(q, k, v, bias, lam_p, g_att, batch, seq, t, lam_init):
    nq = seq // t
    hp = ATTN_HEADS_PER_STEP
    w = hp * 2 * t
    kv_spec = pl.BlockSpec((seq, hp * DV_A), lambda b, h, i: (b, h))
    return pl.pallas_call(
        functools.partial(_attn_prompt_kernel, t=t, hp=hp, lam_init=lam_init),
        grid=(batch, H_A // hp, nq),
        in_specs=[_const_spec((4, D_A)),
                  pl.BlockSpec((1, hp * DV_A), lambda b, h, i: (0, h)),
                  pl.BlockSpec((t, hp * DV_A), lambda b, h, i: (b * nq + i, h)),
                  kv_spec, kv_spec,
                  pl.BlockSpec((hp, 1, 2 * t, 2 * t), lambda b, h, i: (h, jnp.minimum(i, 1), 0, 0))],
        out_specs=pl.BlockSpec((t, hp * DV_A), lambda b, h, i: (b * nq + i, h)),
        out_shape=jax.ShapeDtypeStruct((batch * seq, ATT_WIDTH), BF16),
        scratch_shapes=[pltpu.VMEM((1, w), F32), pltpu.VMEM((1, w), F32),
                        pltpu.VMEM((DV_A, w), F32),
                        pltpu.VMEM((t, w), F32), pltpu.VMEM((t, w), F32),
                        pltpu.VMEM((t, w), BF16)],
        compiler_params=pltpu.CompilerParams(dimension_semantics=("parallel", "parallel", "arbitrary"),
                                             vmem_limit_bytes=V7X_VMEM_LIMIT),
        name="attn_prompt",
    )(lam_p, g_att, q, k, v, bias)


def _attn_sample_kernel(lam_ref, g_ref, q_ref, kp_ref, vp_ref, kn_ref, vn_ref, bp_ref, bn_ref, o_ref,
                        *, lq, lam_init):
    lam = _lambda_value(lam_ref[...], lam_init)
    for h in range(H_A):
        hd = slice(h * DV_A, (h + 1) * DV_A)
        qz = _stack_maps(q_ref[:, hd])
        bp = bp_ref[h]
        bn = bn_ref[h]
        past_rows = pl.ds(h, kp_ref.shape[1] // H_A, stride=H_A)
        sp = lax.dot_general(qz, kp_ref[0, past_rows, :].astype(BF16), _NT, preferred_element_type=F32)
        sp = sp + jnp.concatenate([bp, bp], axis=0)
        sn = lax.dot_general(qz, kn_ref[:, hd], _NT, preferred_element_type=F32)
        sn = sn + jnp.concatenate([bn, bn], axis=0)
        m = jnp.maximum(jnp.max(sp, axis=1, keepdims=True), jnp.max(sn, axis=1, keepdims=True))
        pp = jnp.exp2(sp - m)
        pn = jnp.exp2(sn - m)
        acc = (jnp.dot(pp.astype(BF16), _with_ones(vp_ref[0, past_rows, :].astype(BF16)), preferred_element_type=F32)
               + jnp.dot(pn.astype(BF16), _with_ones(vn_ref[:, hd]), preferred_element_type=F32))
        o_ref[:, hd] = _attn_finish(acc, lam, g_ref[:, hd], lam_init, lq)


def _attn_sample(q, k_new, v_new, k_cache, v_cache, layer, bias_past, bias_new, lam_p, g_att, batch, lq, lam_init):
    depth, _, past = k_cache.shape[:3]
    k_cache = k_cache.reshape(depth, batch, past * H_A, DV_A)
    v_cache = v_cache.reshape(depth, batch, past * H_A, DV_A)
    new_spec = pl.BlockSpec((lq, ATT_WIDTH), lambda b: (b, 0))
    past_spec = pl.BlockSpec((None, 1, past * H_A, DV_A), lambda b: (layer, b, 0, 0))
    return pl.pallas_call(
        functools.partial(_attn_sample_kernel, lq=lq, lam_init=lam_init),
        grid=(batch,),
        in_specs=[_const_spec((4, D_A)), _const_spec((1, ATT_WIDTH)),
                  new_spec, past_spec, past_spec, new_spec, new_spec,
                  _const_spec((H_A, lq, past)), _const_spec((H_A, lq, lq))],
        out_specs=new_spec,
        out_shape=jax.ShapeDtypeStruct((batch * lq, ATT_WIDTH), BF16),
        compiler_params=pltpu.CompilerParams(dimension_semantics=("parallel",),
                                             vmem_limit_bytes=V7X_VMEM_LIMIT),
        name="attn_sample",
    )(lam_p, g_att, q, k_cache, v_cache, k_new, v_new, bias_past, bias_new)


def _per_stream(fn, *arrays):
    return jnp.stack([fn(*(a[g] for a in arrays)) for g in range(arrays[0].shape[0])])


def _recurrent_kernel(gb_ref, q_ref, k_ref, v_ref, og_ref, gt_ref, b_ref, c_ref, hc_ref, cw_ref, gml_ref,
                      s0_ref, m0_ref, cv0_ref,
                      hm_ref, oc_ref, s_ref, m_ref, cv_ref, *, L):
    @pl.when(pl.program_id(1) == 0)
    def _():
        s_ref[...] = s0_ref[...]
        m_ref[...] = m0_ref[...]
        cv_ref[...] = cv0_ref[...]

    q = q_ref[...]
    k = k_ref[...]
    v = v_ref[...]
    G = q.shape[0]
    state = s_ref[...]
    q_state = _per_stream(lambda a, b: jnp.dot(a, b, preferred_element_type=F32),
                          q, state.astype(BF16))
    q_c = q_state[:, :, :ML_WIDTH]

    lane_head = lax.broadcasted_iota(jnp.int32, (1, L, ML_WIDTH), 2) // D_M
    row = lax.broadcasted_iota(jnp.int32, (1, L, L), 1)
    col = lax.broadcasted_iota(jnp.int32, (1, L, L), 2)
    causal = col <= row
    eye = col == row
    st_lane = lax.broadcasted_iota(jnp.int32, (1, 1, ML_STATE_W), 2)
    st_lane_head = jnp.where(st_lane < ML_WIDTH, st_lane // D_M, st_lane - ML_WIDTH)
    m_lane = lax.broadcasted_iota(jnp.int32, (1, 1, 128), 2)

    gt = gt_ref[:, 0]
    m_all = m_ref[...]
    num = jnp.zeros((G, L, ML_WIDTH), F32)
    w_state = jnp.zeros((G, L, ML_WIDTH), F32)
    decay = jnp.zeros((G, 1, ML_STATE_W), F32)
    m_next = jnp.zeros((G, 1, 128), F32)
    for h in range(H_M):
        ig = gt[:, h:h + 1, :] + gb_ref[0, h]
        fz = gt[:, H_M + h:H_M + h + 1, :] + gb_ref[1, h]
        lf = jnp.minimum(fz, 0.0) - jnp.log1p(jnp.exp(-jnp.abs(fz)))
        f_col = jnp.sum(jnp.where(causal, lf, 0.0), axis=2, keepdims=True)
        f_row = jnp.sum(jnp.where(eye, f_col, 0.0), axis=1, keepdims=True)
        d = jnp.where(causal, f_col + (ig - f_row), -jnp.inf)
        m_prev = m_all[:, :, h:h + 1]
        inter = f_col + m_prev
        mt = jnp.maximum(inter, jnp.max(d, axis=2, keepdims=True))
        w_intra = jnp.exp(d - mt)
        w_inter = jnp.exp(inter - mt)
        head = lane_head == h
        qk = _per_stream(lambda a, b: lax.dot_general(a, b, _NT, preferred_element_type=F32),
                         jnp.where(head, q, jnp.zeros_like(q)), k)
        s = qk * w_intra
        den = jnp.sum(s, axis=2, keepdims=True) + w_inter * q_state[:, :, ML_WIDTH + h:ML_WIDTH + h + 1]
        inv = 1.0 / jnp.maximum(jnp.abs(den), jnp.exp(-mt))
        sv = _per_stream(lambda a, b: jnp.dot(a, b, preferred_element_type=F32), s.astype(BF16), v)
        num = jnp.where(head, (sv + w_inter * q_c) * inv, num)
        m_new = mt[:, L - 1:L, :]
        w_col = jnp.sum(jnp.where(eye, w_intra[:, L - 1:L, :], 0.0), axis=2, keepdims=True)
        w_state = jnp.where(head, w_col, w_state)
        dec = jnp.exp(f_col[:, L - 1:L, :] + m_prev - m_new)
        decay = jnp.where(st_lane_head == h, dec, decay)
        m_next = jnp.where(m_lane == h, m_new, m_next)

    kw = (k.astype(F32) * w_state).astype(BF16)
    v_aug = jnp.concatenate([v, jnp.ones((G, L, ML_STATE_W - ML_WIDTH), BF16)], axis=2)
    upd = _per_stream(lambda a, b: lax.dot_general(a, b, _TN, preferred_element_type=F32), kw, v_aug)
    s_row_head = lax.broadcasted_iota(jnp.int32, (1, ML_WIDTH, ML_STATE_W), 1) // D_M
    s_col = lax.broadcasted_iota(jnp.int32, (1, ML_WIDTH, ML_STATE_W), 2)
    s_col_head = jnp.where(s_col < ML_WIDTH, s_col // D_M, s_col - ML_WIDTH)
    s_ref[...] = decay * state + jnp.where(s_row_head == s_col_head, upd, 0.0)
    m_ref[...] = m_next

    hg = num * jax.nn.sigmoid(og_ref[...])
    scale = jnp.zeros((G, L, ML_WIDTH), F32)
    for h in range(H_M):
        head = lane_head == h
        ss = jnp.sum(jnp.where(head, hg * hg, 0.0), axis=2, keepdims=True)
        scale = jnp.where(head, lax.rsqrt(ss * (1.0 / D_M) + EPS), scale)
    hm_ref[...] = (hg * scale * gml_ref[...]).astype(BF16)

    u = c_ref[...] * hc_ref[...]
    prev = cv_ref[...]
    ri = lax.broadcasted_iota(jnp.int32, (1, L, CONV_CH), 1)
    u1 = jnp.where(ri == 0, prev[:, 1:2], _per_stream(lambda a: pltpu.roll(a, 1, 0), u))
    u2 = jnp.where(ri == 0, prev[:, 0:1],
                   jnp.where(ri == 1, prev[:, 1:2], _per_stream(lambda a: pltpu.roll(a, 2, 0), u)))
    w = cw_ref[...]
    oc_ref[...] = (b_ref[...] * (w[0:1] * u2 + w[1:2] * u1 + w[2:3] * u)).astype(BF16)
    cv_ref[:, 0:1, :] = u[:, L - 2:L - 1]
    cv_ref[:, 1:2, :] = u[:, L - 1:L]


def _recurrent(gate_b, qkvm, og, gt, bch, conv_w, g_ml, s0, m0, cv0, batch, seq, L, G):
    assert batch % G == 0 and seq % L == 0
    nc = seq // L
    tok = lambda j: pl.BlockSpec((G, L, ML_WIDTH), lambda b, c: (b, c, j))
    per_b = lambda shape: pl.BlockSpec((G,) + shape, lambda b, c: (b, 0, 0))
    return pl.pallas_call(
        functools.partial(_recurrent_kernel, L=L),
        grid=(batch // G, nc),
        in_specs=[pl.BlockSpec(memory_space=pltpu.SMEM),
                  tok(0), tok(1), tok(2), tok(0),
                  pl.BlockSpec((G, 1, 8, L), lambda b, c: (b, c, 0, 0)),
                  tok(0), tok(1), tok(2),
                  _const_spec((CONV_K, CONV_CH)), _const_spec((1, ML_WIDTH)),
                  per_b((ML_WIDTH, ML_STATE_W)), per_b((1, 128)), per_b((CONV_K - 1, CONV_CH))],
        out_specs=[tok(0), tok(0),
                   per_b((ML_WIDTH, ML_STATE_W)), per_b((1, 128)), per_b((CONV_K - 1, CONV_CH))],
        out_shape=[jax.ShapeDtypeStruct((batch, seq, ML_WIDTH), BF16),
                   jax.ShapeDtypeStruct((batch, seq, CONV_CH), BF16),
                   jax.ShapeDtypeStruct((batch, ML_WIDTH, ML_STATE_W), F32),
                   jax.ShapeDtypeStruct((batch, 1, 128), F32),
                   jax.ShapeDtypeStruct((batch, CONV_K - 1, CONV_CH), F32)],
        compiler_params=pltpu.CompilerParams(dimension_semantics=("parallel", "arbitrary"),
                                             vmem_limit_bytes=V7X_VMEM_LIMIT),
        name="recurrent",
    )(gate_b, qkvm, qkvm, qkvm, og, gt, bch, bch, bch, conv_w, g_ml, s0, m0, cv0)


def _mem_kv_kernel(x_ref, w_ref, kf_ref, vf_ref, kb_ref, vb_ref):
    x = x_ref[...].astype(BF16)
    n = H_X * D_X
    k = jnp.dot(x, w_ref[:, :n], preferred_element_type=F32)
    v = jnp.dot(x, w_ref[:, n:], preferred_element_type=F32)
    kf_ref[...] = k
    vf_ref[...] = v
    kb_ref[...] = k.astype(BF16)
    vb_ref[...] = v.astype(BF16)


def _mem_kv(mem, w_kv, tm):
    t = mem.shape[0]
    n = H_X * D_X
    row = pl.BlockSpec((tm, n), lambda i: (i, 0))
    return pl.pallas_call(
        _mem_kv_kernel,
        grid=(t // tm,),
        in_specs=[pl.BlockSpec((tm, D_MODEL), lambda i: (i, 0)), _const_spec((D_MODEL, 2 * n))],
        out_specs=[row, row, row, row],
        out_shape=[jax.ShapeDtypeStruct((t, n), F32), jax.ShapeDtypeStruct((t, n), F32),
                   jax.ShapeDtypeStruct((t, n), BF16), jax.ShapeDtypeStruct((t, n), BF16)],
        compiler_params=pltpu.CompilerParams(dimension_semantics=("parallel",),
                                             vmem_limit_bytes=V7X_VMEM_LIMIT),
        name="mem_kv",
    )(mem, w_kv)


FF_CHUNK = 256


def _post_kernel(x_ref, oa_ref, hm_ref, oc_ref, mk_ref, mv_ref, wout_ref, gc_ref, wcq_ref, wco_ref,
                 gf_ref, wg_ref, wu_ref, wd_ref, gfin_ref, o_ref, *, final):
    a, m = ATT_WIDTH, ML_WIDTH
    x = x_ref[...]
    x = x + (jnp.dot(oa_ref[...], wout_ref[0:a, :], preferred_element_type=F32)
             + jnp.dot(hm_ref[...], wout_ref[a:a + m, :], preferred_element_type=F32)
             + jnp.dot(oc_ref[...], wout_ref[a + m:, :], preferred_element_type=F32))

    xn = _rms(x, gc_ref[...]).astype(BF16)
    qc = (jnp.dot(xn, wcq_ref[...], preferred_element_type=F32) * (D_X ** -0.5)).astype(BF16)
    cross = jnp.zeros_like(x)
    n_streams = mk_ref.shape[0]
    rows = x.shape[0] // n_streams
    heads = [slice(h * D_X, (h + 1) * D_X) for h in range(H_X)]
    streams = [slice(b * rows, (b + 1) * rows) for b in range(n_streams)]
    scores = [[lax.dot_general(qc[r, sl], mk_ref[b, :, sl], _NT, preferred_element_type=F32)
               for b, r in enumerate(streams)] for sl in heads]
    for sl, s_head in zip(heads, scores):
        outs = []
        for b, s in enumerate(s_head):
            p = jnp.exp(s - jnp.max(s, axis=1, keepdims=True))
            p = p * (1.0 / jnp.sum(p, axis=1, keepdims=True))
            outs.append(jnp.dot(p.astype(BF16), mv_ref[b, :, sl], preferred_element_type=F32).astype(BF16))
        o = outs[0] if n_streams == 1 else jnp.concatenate(outs, axis=0)
        cross = cross + jnp.dot(o, wco_ref[sl, :], preferred_element_type=F32)
    x = x + cross

    xn = _rms(x, gf_ref[...]).astype(BF16)
    ff = jnp.zeros_like(x)
    for j in range(D_FF // FF_CHUNK):
        sl = slice(j * FF_CHUNK, (j + 1) * FF_CHUNK)
        g = jnp.dot(xn, wg_ref[:, sl], preferred_element_type=F32)
        u = jnp.dot(xn, wu_ref[:, sl], preferred_element_type=F32)
        act = (g * jax.nn.sigmoid(g) * u).astype(BF16)
        ff = ff + jnp.dot(act, wd_ref[sl, :], preferred_element_type=F32)
    x = x + ff
    if final:
        x = _rms(x, gfin_ref[...])
    o_ref[...] = x


def _post(x, oa, hm, oc, mk, mv, w_out, g_cross, w_cq, w_co, g_ffn, w_gate, w_up, w_down, g_final,
          tm, seq, final):
    t = x.shape[0]
    row = lambda n: pl.BlockSpec((tm, n), lambda i: (i, 0))
    if tm <= seq:
        tiles_per_stream = seq // tm
        mem = pl.BlockSpec((1,) + mk.shape[1:], lambda i: (i // tiles_per_stream, 0, 0))
    else:
        mem = pl.BlockSpec((tm // seq,) + mk.shape[1:], lambda i: (i, 0, 0))
    vec = _const_spec((1, D_MODEL))
    return pl.pallas_call(
        functools.partial(_post_kernel, final=final),
        grid=(t // tm,),
        in_specs=[row(D_MODEL), row(ATT_WIDTH), row(ML_WIDTH), row(CONV_CH), mem, mem,
                  _const_spec(w_out.shape), vec, _const_spec(w_cq.shape), _const_spec(w_co.shape),
                  vec, _const_spec(w_gate.shape), _const_spec(w_up.shape), _const_spec(w_down.shape), vec],
        out_specs=row(D_MODEL),
        out_shape=jax.ShapeDtypeStruct((t, D_MODEL), F32),
        compiler_params=pltpu.CompilerParams(dimension_semantics=("parallel",),
                                             vmem_limit_bytes=V7X_VMEM_LIMIT),
        name="post",
    )(x, oa, hm, oc, mk, mv, w_out, g_cross, w_cq, w_co, g_ffn, w_gate, w_up, w_down, g_final)


def _rel_bucket(rel):
    half = NUM_BUCKETS // 2
    max_exact = half // 2
    n = jnp.abs(rel)
    large = max_exact + (jnp.log(jnp.maximum(n, 1).astype(F32) / max_exact)
                         / math.log(MAX_DISTANCE / max_exact) * (half - max_exact)).astype(jnp.int32)
    large = jnp.minimum(large, half - 1)
    return jnp.where(rel > 0, half, 0) + jnp.where(n < max_exact, n, large)


def _bias_table(rel_bias, rel):
    bucket = _rel_bucket(rel)[None]
    table = rel_bias.astype(F32)
    out = jnp.zeros((table.shape[1],) + rel.shape, F32)
    for b in range(NUM_BUCKETS):
        out = jnp.where(bucket == b, table[b][:, None, None], out)
    return out


def _prompt_bias_tiles(rel_bias, t):
    key = jnp.arange(t, dtype=jnp.int32)[:, None]
    qry = jnp.arange(t, dtype=jnp.int32)[None, :]
    far = rel_bias[_rel_bucket(jnp.int32(-MAX_DISTANCE))].astype(F32)[:, None, None]
    prev = (_bias_table(rel_bias, key - t - qry) - far) * LOG2E
    diag = jnp.where((key // CHUNK) <= (qry // CHUNK), (_bias_table(rel_bias, key - qry) - far) * LOG2E, -jnp.inf)
    both_maps = lambda a: jnp.concatenate([a, a], axis=2)
    tiles = jnp.stack([jnp.concatenate([diag, diag], axis=1), jnp.concatenate([prev, diag], axis=1)], axis=1)
    return lax.optimization_barrier(both_maps(tiles.reshape(-1, 2 * t, t)).reshape(-1, 2, 2 * t, 2 * t))


def _block_diag_state(c, n):
    b = c.shape[0]
    eye = jnp.eye(H_M, dtype=F32)
    cbd = jnp.einsum('bhdv,hg->bhdgv', c.astype(F32), eye).reshape(b, ML_WIDTH, ML_WIDTH)
    ncol = jnp.einsum('bhd,hg->bhdg', n.astype(F32), eye).reshape(b, ML_WIDTH, H_M)
    pad = jnp.zeros((b, ML_WIDTH, ML_STATE_W - ML_WIDTH - H_M), F32)
    return jnp.concatenate([cbd, ncol, pad], axis=2)


def _unpack_state(s, m):
    b = s.shape[0]
    blocks = s[:, :, :ML_WIDTH].reshape(b, H_M, D_M, H_M, D_M)
    c = jnp.stack([blocks[:, h, :, h, :] for h in range(H_M)], axis=1)
    ncols = s[:, :, ML_WIDTH:ML_WIDTH + H_M].reshape(b, H_M, D_M, H_M)
    n = jnp.stack([ncols[:, h, :, h] for h in range(H_M)], axis=1)
    return c, n, m[:, 0, :H_M]


def _layer(x, layer, depth, kv_all, wts, attn_fn, mem_k, mem_v, ml_state, conv_prev, g_final,
           batch, seq, tm_proj, L, tm_post):
    final = layer == depth - 1
    (g_mix, w_proj, w_gt, gate_b, conv_w, lam_p, g_att, g_ml, w_out,
     g_cross, w_cq, w_co, g_ffn, w_gate, w_up, w_down) = wts
    t = batch * seq
    lam_init = 0.8 - 0.6 * math.exp(-0.3 * layer)
    qa, kf, kb, vf, vb, qkvm, og, bch, gt = _in_proj(x, g_mix, w_proj, w_gt, tm_proj, kv_all)
    oa = attn_fn(qa, kb, vb, lam_p, g_att, lam_init)
    gt = jnp.transpose(gt.reshape(8, batch, seq // L, L), (1, 2, 0, 3))
    tok3 = lambda a: a.reshape(batch, seq, a.shape[-1])
    hm, oc, s1, m1, cv1 = _recurrent(gate_b, tok3(qkvm), tok3(og), gt, tok3(bch), conv_w, g_ml,
                                     ml_state[0], ml_state[1], conv_prev, batch, seq, L,
                                     min(RECURRENT_STREAMS, batch))
    hm, oc = hm.reshape(t, ML_WIDTH), oc.reshape(t, CONV_CH)
    x = _post(x, oa, hm, oc, mem_k, mem_v, w_out, g_cross, w_cq, w_co, g_ffn, w_gate, w_up, w_down,
              g_final, tm_post, seq, final)
    c1, n1, mm1 = _unpack_state(s1, m1)
    return x, (kf, vf), c1, n1, mm1, cv1


def kernel(x_prompt, x_sample, mem_prompt, cache_att_k, cache_att_v, cache_mem_k, cache_mem_v,
           state_mlstm_C, state_mlstm_n, state_mlstm_m, state_conv,
           norm_mix, w_in, mlstm_gate_bias, conv_w, lambda_params, norm_att_heads, norm_mlstm_heads,
           w_out, norm_cross, w_cq, w_ck, w_cv, w_co, norm_ffn, w_gate, w_up, w_down,
           rel_bias, norm_final):
    bp, sp, _ = x_prompt.shape
    bs, ss, _ = x_sample.shape
    depth = w_in.shape[0]
    past = cache_att_k.shape[2]
    n_mem = mem_prompt.shape[1]
    xp = x_prompt.reshape(bp * sp, D_MODEL)
    xs = x_sample.reshape(bs * ss, D_MODEL)
    mem = mem_prompt.reshape(bp * n_mem, D_MODEL)
    g_final = norm_final.reshape(1, D_MODEL)

    bias_tiles = _prompt_bias_tiles(rel_bias, ATTN_TILE)
    rel_s = jnp.arange(past + ss, dtype=jnp.int32)[None, :] - (past + jnp.arange(ss, dtype=jnp.int32))[:, None]
    bias_s = _bias_table(rel_bias, rel_s) * LOG2E
    bias_s_past, bias_s_new = lax.optimization_barrier((bias_s[:, :, :past], bias_s[:, :, past:]))

    zero_state = (jnp.zeros((bp, ML_WIDTH, ML_STATE_W), F32), jnp.zeros((bp, 1, 128), F32))
    zero_conv = jnp.zeros((bp, CONV_K - 1, CONV_CH), F32)

    outs = {k: [] for k in ('pmk', 'pmv', 'pC', 'pn', 'pm', 'pcv', 'sC', 'sn', 'sm', 'scv')}
    kv_p = kv_s = None
    gate_lo = 3 * ATT_WIDTH + 4 * ML_WIDTH
    for l in range(depth):
        w = w_in[l]
        wts = (norm_mix[l].reshape(1, D_MODEL),
               jnp.concatenate([w[:, :gate_lo], w[:, gate_lo + 2 * H_M:]], axis=1).astype(BF16),
               w[:, gate_lo:gate_lo + 2 * H_M].T.astype(BF16),
               mlstm_gate_bias[l].astype(F32), conv_w[l].astype(F32), lambda_params[l].astype(F32),
               norm_att_heads[l].reshape(1, ATT_WIDTH), norm_mlstm_heads[l].reshape(1, ML_WIDTH),
               w_out[l].astype(BF16), norm_cross[l].reshape(1, D_MODEL),
               w_cq[l].astype(BF16), w_co[l].astype(BF16), norm_ffn[l].reshape(1, D_MODEL),
               w_gate[l].astype(BF16), w_up[l].astype(BF16), w_down[l].astype(BF16))

        w_kv = jnp.concatenate([w_ck[l], w_cv[l]], axis=1).astype(BF16)
        mkf, mvf, mkb, mvb = _mem_kv(mem, w_kv, TOKEN_TILE)
        attn_p = lambda q, k, v, lam_p, g, li: _attn_prompt(q, k, v, bias_tiles, lam_p, g, bp, sp, ATTN_TILE, li)
        xp, kv_p, c1, n1, m1, cv1 = _layer(
            xp, l, depth, kv_p, wts, attn_p, mkb.reshape(bp, n_mem, -1), mvb.reshape(bp, n_mem, -1),
            zero_state, zero_conv, g_final, bp, sp, TOKEN_TILE, MLSTM_CHUNK, TOKEN_TILE)
        outs['pmk'].append(mkf.reshape(bp, n_mem, H_X, D_X))
        outs['pmv'].append(mvf.reshape(bp, n_mem, H_X, D_X))
        outs['pC'].append(c1); outs['pn'].append(n1); outs['pm'].append(m1); outs['pcv'].append(cv1)

        attn_s = lambda q, k, v, lam_p, g, li, l=l: _attn_sample(q, k, v, cache_att_k, cache_att_v, l,
                                                                 bias_s_past, bias_s_new, lam_p, g, bs, ss, li)
        s0 = _block_diag_state(state_mlstm_C[l], state_mlstm_n[l])
        m0 = jnp.pad(state_mlstm_m[l].astype(F32), ((0, 0), (0, 128 - H_M))).reshape(bs, 1, 128)
        xs, kv_s, c1, n1, m1, cv1 = _layer(
            xs, l, depth, kv_s, wts, attn_s,
            cache_mem_k[l].reshape(bs, n_mem, -1).astype(BF16), cache_mem_v[l].reshape(bs, n_mem, -1).astype(BF16),
            (s0, m0), state_conv[l].astype(F32), g_final, bs, ss, bs * ss, ss, bs * ss)
        outs['sC'].append(c1); outs['sn'].append(n1); outs['sm'].append(m1); outs['scv'].append(cv1)

    st = lambda k: jnp.stack(outs[k])
    per_head = lambda a, b, s: a.reshape(depth, b, s, H_A, DV_A)
    return (xp.reshape(bp, sp, D_MODEL), xs.reshape(bs, ss, D_MODEL),
            per_head(kv_p[0], bp, sp), per_head(kv_p[1], bp, sp),
            st('pmk'), st('pmv'), st('pC'), st('pn'), st('pm'), st('pcv'),
            per_head(kv_s[0], bs, ss), per_head(kv_s[1], bs, ss),
            st('sC'), st('sn'), st('sm'), st('scv'))
```
